```python
import math
import jax, jax.numpy as jnp
from jax import lax
import numpy as np

D_MODEL = 1024
BATCH = 4
SEQ = 4096
DEPTH = 1
DEC_BATCH = 128
DEC_SEQ = 1
PAST_LEN = 8192
PAGE_SIZE = 128

D_INNER = 2 * D_MODEL
SSM_HEAD_DIM = 64
N_SSM_HEADS = D_INNER // SSM_HEAD_DIM
N_SSM_GROUPS = 4
D_STATE = 128
D_CONV = 4
SSD_CHUNK = 128
CONV_DIM = D_INNER + 2 * N_SSM_GROUPS * D_STATE

NSA_HEAD_DIM = 64
N_Q_HEADS = 16
N_KV_HEADS = 4
Q_PER_KV = N_Q_HEADS // N_KV_HEADS
NSA_WIDTH = N_Q_HEADS * NSA_HEAD_DIM
KV_WIDTH = N_KV_HEADS * NSA_HEAD_DIM
CMP_BLOCK = 32
CMP_STRIDE = 16
SEL_BLOCK = 64
N_SEL_BLOCKS = 16
WINDOW = 512
Q_BLOCK = 128

N_BUCKETS = 32
MAX_DISTANCE = 128

N_EXPERTS = 32
TOP_K = 4
D_FF = D_MODEL
SWIGLU_LIMIT = 7.0
SWIGLU_ALPHA = 1.702

RMS_EPS = 1e-6
NEG_INF = -1e30
FORCE_SCORE = 1e9

IN_SPLITS = (D_INNER, CONV_DIM, N_SSM_HEADS, NSA_WIDTH, 2 * KV_WIDTH, 2 * KV_WIDTH, 2 * KV_WIDTH,
             3 * N_Q_HEADS, D_MODEL, D_MODEL)
D_IN_PROJ = sum(IN_SPLITS)

kernel_name = 'hybrid_ssd_nsa_moe_step'


def rmsnorm(x, w):
    xf = x.astype(jnp.float32)
    y = xf * lax.rsqrt(jnp.mean(xf * xf, axis=-1, keepdims=True) + RMS_EPS)
    return (y * w.astype(jnp.float32)).astype(x.dtype)


def masked_softmax(s, mask):
    p = jax.nn.softmax(jnp.where(mask, s, NEG_INF), axis=-1)
    return jnp.where(mask, p, 0.0)


def t5_bucket(dist):
    n = jnp.maximum(dist, 0)
    max_exact = N_BUCKETS // 2
    nf = jnp.maximum(n, 1).astype(jnp.float32)
    large = max_exact + (jnp.log(nf / max_exact) / math.log(MAX_DISTANCE / max_exact)
                         * (N_BUCKETS - max_exact)).astype(jnp.int32)
    large = jnp.minimum(large, N_BUCKETS - 1)
    return jnp.where(n < max_exact, n, large)


def rel_bias_shared(tbl, dist):
    b = tbl[t5_bucket(dist)].reshape(dist.shape + (N_KV_HEADS, Q_PER_KV))
    return jnp.transpose(b, (0, 2, 3, 1))


def project(x, norm_w, w_in, q_norm_w, k_sel_norm_w, k_win_norm_w):
    b, L, _ = x.shape
    p = rmsnorm(x, norm_w) @ w_in
    z, xbc, dt, q, kv_c, kv_s, kv_w, br_g, g_ssm, g_nsa = jnp.split(
        p, np.cumsum(IN_SPLITS)[:-1].tolist(), axis=-1)
    q = rmsnorm(q.reshape(b, L, N_Q_HEADS, NSA_HEAD_DIM), q_norm_w) * (NSA_HEAD_DIM ** -0.5)
    q = q.reshape(b, L, N_KV_HEADS, Q_PER_KV, NSA_HEAD_DIM)
    kv_c = kv_c.reshape(b, L, 2, N_KV_HEADS, NSA_HEAD_DIM)
    kv_s = kv_s.reshape(b, L, 2, N_KV_HEADS, NSA_HEAD_DIM)
    kv_s = jnp.stack([rmsnorm(kv_s[:, :, 0], k_sel_norm_w), kv_s[:, :, 1]], axis=2)
    kv_w = kv_w.reshape(b, L, 2, N_KV_HEADS, NSA_HEAD_DIM)
    kv_w = jnp.stack([rmsnorm(kv_w[:, :, 0], k_win_norm_w), kv_w[:, :, 1]], axis=2)
    br_g = jax.nn.sigmoid(br_g).reshape(b, L, N_KV_HEADS, Q_PER_KV, 3)
    return z, xbc, dt, q, kv_c, kv_s, kv_w, br_g, g_ssm, g_nsa


def ssd_scan(x, dt, A, Bm, Cm, h0):
    f32 = jnp.float32
    b, L, H, P = x.shape
    G, N = Bm.shape[2], Bm.shape[3]
    R = H // G
    Q = min(SSD_CHUNK, L)
    pad = (-L) % Q
    padL = lambda a: jnp.pad(a.astype(f32), ((0, 0), (0, pad)) + ((0, 0),) * (a.ndim - 2))
    x, dt, Bm, Cm = padL(x), padL(dt), padL(Bm), padL(Cm)
    nc = (L + pad) // Q
    x = x.reshape(b, nc, Q, G, R, P)
    dt = dt.reshape(b, nc, Q, G, R)
    Bm = Bm.reshape(b, nc, Q, G, N)
    Cm = Cm.reshape(b, nc, Q, G, N)
    acum = jnp.cumsum(dt * A.astype(f32).reshape(G, R), axis=2)
    causal = jnp.tril(jnp.ones((Q, Q), bool))[:, :, None, None]
    seg = acum[:, :, :, None] - acum[:, :, None, :]
    decay = jnp.where(causal, jnp.exp(jnp.where(causal, seg, 0.0)), 0.0)
    xdt = x * dt[..., None]
    cb = jnp.einsum('bcign,bcjgn->bcijg', Cm, Bm)
    y = jnp.einsum('bcijg,bcijgr,bcjgrp->bcigrp', cb, decay, xdt)
    to_end = jnp.exp(acum[:, :, -1:] - acum)
    states = jnp.einsum('bcjgn,bcjgr,bcjgrp->bcgrpn', Bm, to_end, xdt)

    def step(h, inp):
        st, d = inp
        return d[..., None, None] * h + st, h

    h_fin, h_in = lax.scan(step, h0.astype(f32).reshape(b, G, R, P, N),
                           (jnp.moveaxis(states, 1, 0), jnp.moveaxis(jnp.exp(acum[:, :, -1]), 1, 0)))
    y = y + jnp.einsum('bcign,cbgrpn,bcigr->bcigrp', Cm, h_in, jnp.exp(acum))
    return y.reshape(b, nc * Q, H, P)[:, :L], h_fin.reshape(b, H, P, N)


def mamba_mixer(z, xbc, dt, conv_prev, h0, conv_w, conv_b, dt_bias, a_log, d_skip, ssm_norm_w):
    b, L, _ = xbc.shape
    xpad = jnp.concatenate([conv_prev.astype(xbc.dtype), xbc], axis=1)
    conv_state = xpad[:, -(D_CONV - 1):]
    u = lax.conv_general_dilated(xpad, conv_w.astype(xpad.dtype)[:, None, :], window_strides=(1,),
                                 padding='VALID', dimension_numbers=('NWC', 'WIO', 'NWC'),
                                 feature_group_count=CONV_DIM) + conv_b
    u = jax.nn.silu(u)
    xs, Bm, Cm = jnp.split(u, [D_INNER, D_INNER + N_SSM_GROUPS * D_STATE], axis=-1)
    xs = xs.reshape(b, L, N_SSM_HEADS, SSM_HEAD_DIM)
    Bm = Bm.reshape(b, L, N_SSM_GROUPS, D_STATE)
    Cm = Cm.reshape(b, L, N_SSM_GROUPS, D_STATE)
    dt = jax.nn.softplus((dt + dt_bias).astype(jnp.float32))
    A = -jnp.exp(a_log.astype(jnp.float32))
    y, h = ssd_scan(xs, dt, A, Bm, Cm, h0)
    y = y + d_skip[:, None] * xs
    y = y.reshape(b, L, D_INNER) * jax.nn.silu(z)
    y = rmsnorm(y.reshape(b, L, N_SSM_GROUPS, -1), ssm_norm_w.reshape(N_SSM_GROUPS, -1))
    return y.reshape(b, L, D_INNER), conv_state, h


def cmp_partials(rows, pe, w):
    b, L, G, Dh = rows.shape
    sub = rows.reshape(b, L // CMP_STRIDE, CMP_STRIDE, G, Dh)
    lo = jnp.einsum('bnigd,ide->bnge', sub + pe[:CMP_STRIDE, None, :], w[:CMP_STRIDE])
    hi = jnp.einsum('bnigd,ide->bnge', sub + pe[CMP_STRIDE:, None, :], w[CMP_STRIDE:])
    return lo, hi


def nsa_attend(q, gates, tq, kc, vc, ec, fetch_sel, n_sel, kw, vw, pw, rel_bias):
    f32 = jnp.float32
    tbl = rel_bias.astype(f32)
    dc = tq[:, None] - ec[None, :]
    s = jnp.einsum('btgrd,bngd->btgrn', q, kc).astype(f32) + rel_bias_shared(tbl, dc)[None]
    p_cmp = masked_softmax(s, (dc >= 0)[None, :, None, None, :])
    o_cmp = jnp.einsum('btgrn,bngd->btgrd', p_cmp.astype(vc.dtype), vc)
    cs = jnp.arange(ec.shape[0]) * CMP_STRIDE
    ss = jnp.arange(n_sel) * SEL_BLOCK
    cover = ((cs[:, None] < ss[None, :] + SEL_BLOCK) & (cs[:, None] + CMP_BLOCK > ss[None, :])).astype(f32)
    imp = jnp.einsum('btgn,nj->btgj', p_cmp.sum(axis=3), cover)
    j = jnp.arange(n_sel)[None, :]
    cur = (tq // SEL_BLOCK)[:, None]
    valid = ss[None, :] <= tq[:, None]
    forced = valid & ((j == 0) | (j == cur) | (j == cur - 1))
    imp = jnp.where(forced[None, :, None, :], FORCE_SCORE,
                    jnp.where(valid[None, :, None, :], imp, NEG_INF))
    _, idx = lax.top_k(imp, min(N_SEL_BLOCKS, n_sel))
    ks, vs = fetch_sel(idx)
    ks = ks.reshape(ks.shape[:3] + (-1, ks.shape[-1]))
    vs = vs.reshape(vs.shape[:3] + (-1, vs.shape[-1]))
    ps = (idx[..., None] * SEL_BLOCK + jnp.arange(SEL_BLOCK)).reshape(idx.shape[:3] + (-1,))
    ds = tq[None, :, None, None] - ps
    b_sel = tbl.reshape(N_BUCKETS, N_KV_HEADS, Q_PER_KV)[t5_bucket(ds), jnp.arange(N_KV_HEADS)[None, None, :, None]]
    s = jnp.einsum('btgrd,btgld->btgrl', q, ks).astype(f32) + jnp.moveaxis(b_sel, -1, 3)
    p = masked_softmax(s, (ds >= 0)[:, :, :, None, :])
    o_sel = jnp.einsum('btgrl,btgld->btgrd', p.astype(vs.dtype), vs)
    dw = tq[:, None] - pw[None, :]
    mw = (dw >= 0) & (dw < WINDOW) & (pw[None, :] >= 0)
    s = jnp.einsum('btgrd,blgd->btgrl', q, kw).astype(f32) + rel_bias_shared(tbl, dw)[None]
    p = masked_softmax(s, mw[None, :, None, None, :])
    o_win = jnp.einsum('btgrl,blgd->btgrd', p.astype(vw.dtype), vw)
    return gates[..., 0:1] * o_cmp + gates[..., 1:2] * o_sel + gates[..., 2:3] * o_win


def nsa_prompt(q, kv_c, kv_s, kv_w, gates, k_cmp_norm_w, cmp_pe_k, cmp_w_k, cmp_pe_v, cmp_w_v, rel_bias):
    b, S = q.shape[:2]
    lo_k, hi_k = cmp_partials(kv_c[:, :, 0], cmp_pe_k, cmp_w_k)
    lo_v, hi_v = cmp_partials(kv_c[:, :, 1], cmp_pe_v, cmp_w_v)
    kc = rmsnorm(lo_k[:, :-1] + hi_k[:, 1:], k_cmp_norm_w)
    vc = lo_v[:, :-1] + hi_v[:, 1:]
    ec = jnp.arange(kc.shape[1]) * CMP_STRIDE + (CMP_BLOCK - 1)
    n_sel = S // SEL_BLOCK
    blk = kv_s.reshape(b, n_sel, SEL_BLOCK, 2, N_KV_HEADS, NSA_HEAD_DIM).transpose(0, 4, 1, 2, 3, 5)
    bi = jnp.arange(b)[:, None, None, None]
    gi = jnp.arange(N_KV_HEADS)[None, None, :, None]

    def fetch(idx):
        kv = blk[bi, gi, idx]
        return kv[..., 0, :], kv[..., 1, :]

    kv_w_pad = jnp.pad(kv_w, ((0, 0), (WINDOW, 0), (0, 0), (0, 0), (0, 0)))

    def block(i):
        t0 = i * Q_BLOCK
        qb = lax.dynamic_slice_in_dim(q, t0, Q_BLOCK, axis=1)
        gb = lax.dynamic_slice_in_dim(gates, t0, Q_BLOCK, axis=1)
        wb = lax.dynamic_slice_in_dim(kv_w_pad, t0, WINDOW + Q_BLOCK, axis=1)
        tq = t0 + jnp.arange(Q_BLOCK)
        pw = t0 - WINDOW + jnp.arange(WINDOW + Q_BLOCK)
        return nsa_attend(qb, gb, tq, kc, vc, ec, fetch, n_sel, wb[:, :, 0], wb[:, :, 1], pw, rel_bias)

    o = lax.map(block, jnp.arange(S // Q_BLOCK))
    return jnp.moveaxis(o, 0, 1).reshape(b, S, NSA_WIDTH)


def nsa_sample(q, kv_c, kv_s, kv_w, gates, cache_cmp, cache_sel, win_buf, page_table, layer,
               k_cmp_norm_w, cmp_pe_k, cmp_w_k, cmp_pe_v, cmp_w_v, rel_bias):
    b, T = q.shape[:2]
    G, Dh = N_KV_HEADS, NSA_HEAD_DIM
    past_len = page_table.shape[1] * PAGE_SIZE
    tq = past_len + jnp.arange(T)
    new_c = jnp.pad(kv_c, ((0, 0), (0, (-T) % CMP_STRIDE), (0, 0), (0, 0), (0, 0)))

    def compressed(jkv, pe, w):
        past = cache_cmp[layer, page_table, :, jkv].reshape(b, past_len, G, Dh)
        lo_p, hi_p = cmp_partials(past.astype(new_c.dtype), pe, w)
        lo_n, hi_n = cmp_partials(new_c[:, :, jkv], pe, w)
        return jnp.concatenate([lo_p, lo_n], axis=1)[:, :-1] + jnp.concatenate([hi_p, hi_n], axis=1)[:, 1:]

    kc = rmsnorm(compressed(0, cmp_pe_k, cmp_w_k), k_cmp_norm_w)
    vc = compressed(1, cmp_pe_v, cmp_w_v)
    ec = jnp.arange(kc.shape[1]) * CMP_STRIDE + (CMP_BLOCK - 1)
    n_past_sel = past_len // SEL_BLOCK
    new_s = jnp.pad(kv_s, ((0, 0), (0, (-T) % SEL_BLOCK), (0, 0), (0, 0), (0, 0)))
    n_new_sel = new_s.shape[1] // SEL_BLOCK
    new_s = new_s.reshape(b, n_new_sel, SEL_BLOCK, 2, G, Dh)
    bi = jnp.arange(b)[:, None, None, None]
    gi = jnp.arange(G)[None, None, :, None]
    rows = jnp.arange(SEL_BLOCK)
    blocks_per_page = PAGE_SIZE // SEL_BLOCK

    def fetch(idx):
        jp = jnp.minimum(idx, n_past_sel - 1)
        phys = page_table[bi, jp // blocks_per_page]
        off = (jp % blocks_per_page) * SEL_BLOCK
        past = cache_sel[layer, phys[..., None], off[..., None] + rows, :, gi[..., None]]
        jn = jnp.clip(idx - n_past_sel, 0, n_new_sel - 1)
        new = new_s[bi[..., None], jn[..., None], rows, :, gi[..., None]]
        kv = jnp.where((idx < n_past_sel)[..., None, None, None], past.astype(new.dtype), new)
        return kv[..., 0, :], kv[..., 1, :]

    w_buf = win_buf.shape[1]
    win_all = jnp.concatenate([win_buf.astype(kv_w.dtype), kv_w], axis=1)
    pw = past_len - w_buf + jnp.arange(w_buf + T)
    o = nsa_attend(q, gates, tq, kc, vc, ec, fetch, n_past_sel + n_new_sel,
                   win_all[:, :, 0], win_all[:, :, 1], pw, rel_bias)
    return o.reshape(b, T, NSA_WIDTH), win_all[:, -w_buf:]


def merge(x, y_ssm, y_nsa, g_ssm, g_nsa, w_ssm_out, w_nsa_out, w_out):
    u = jax.nn.sigmoid(g_ssm) * (y_ssm @ w_ssm_out) + jax.nn.sigmoid(g_nsa) * (y_nsa @ w_nsa_out)
    return x + u @ w_out


def moe(h, w_router, b_router, w_gate_up, b_gate_up, w_down, b_down):
    logits = (h @ w_router).astype(jnp.float32) + b_router.astype(jnp.float32)
    top_v, top_i = lax.top_k(logits, TOP_K)
    top_w = jax.nn.softmax(top_v, axis=-1)
    gate = jnp.einsum('nk,nke->en', top_w, jax.nn.one_hot(top_i, N_EXPERTS, dtype=jnp.float32)).astype(h.dtype)

    def expert(acc, p):
        w1, b1, w2, b2, g = p
        gt, up = jnp.split(h @ w1 + b1, 2, axis=-1)
        gt = jnp.minimum(gt, SWIGLU_LIMIT)
        up = jnp.clip(up, -SWIGLU_LIMIT, SWIGLU_LIMIT)
        act = (up + 1.0) * gt * jax.nn.sigmoid(SWIGLU_ALPHA * gt)
        return acc + (g[:, None] * (act @ w2 + b2)).astype(acc.dtype), None

    out, _ = lax.scan(expert, jnp.zeros_like(h), (w_gate_up, b_gate_up, w_down, b_down, gate))
    return out


def setup_inputs(seed: int = 0) -> dict:
    key = jax.random.key(seed)
    ks = iter(jax.random.split(key, 48))
    f32 = jnp.float32

    def nrm(shape, scale):
        return scale * jax.random.normal(next(ks), shape, f32)

    n_pages = PAST_LEN // PAGE_SIZE
    n_pool = (DEC_BATCH * n_pages * 5) // 4
    w_buf = min(WINDOW, PAST_LEN)
    page_table = jax.random.permutation(next(ks), n_pool)[: DEC_BATCH * n_pages].reshape(
        DEC_BATCH, n_pages).astype(jnp.int32)
    dt0 = jnp.exp(jax.random.uniform(next(ks), (DEPTH, N_SSM_HEADS), f32, math.log(1e-3), math.log(1e-1)))
    dt_bias = dt0 + jnp.log(-jnp.expm1(-dt0))
    a_log = jnp.log(jax.random.uniform(next(ks), (DEPTH, N_SSM_HEADS), f32, 1.0, 16.0))
    kvshape = (2, N_KV_HEADS, NSA_HEAD_DIM)
    return {
        'x_prompt': nrm((BATCH, SEQ, D_MODEL), 1.0),
        'x_sample': nrm((DEC_BATCH, DEC_SEQ, D_MODEL), 1.0),
        'cache_cmp': nrm((DEPTH, n_pool, PAGE_SIZE) + kvshape, 1.0),
        'cache_sel': nrm((DEPTH, n_pool, PAGE_SIZE) + kvshape, 1.0),
        'cache_win': nrm((DEPTH, DEC_BATCH, w_buf) + kvshape, 1.0),
        'state_ssm': nrm((DEPTH, DEC_BATCH, N_SSM_HEADS, SSM_HEAD_DIM, D_STATE), 0.1),
        'state_conv': nrm((DEPTH, DEC_BATCH, D_CONV - 1, CONV_DIM), 1.0),
        'page_table': page_table,
        'norm_mix_w': 1.0 + nrm((DEPTH, D_MODEL), 0.02),
        'w_in': nrm((DEPTH, D_MODEL, D_IN_PROJ), D_MODEL ** -0.5),
        'conv_w': nrm((DEPTH, D_CONV, CONV_DIM), D_CONV ** -0.5),
        'conv_b': nrm((DEPTH, CONV_DIM), 0.02),
        'dt_bias': dt_bias,
        'a_log': a_log,
        'd_skip': 1.0 + nrm((DEPTH, N_SSM_HEADS), 0.1),
        'ssm_norm_w': 1.0 + nrm((DEPTH, D_INNER), 0.02),
        'w_ssm_out': nrm((DEPTH, D_INNER, D_MODEL), D_INNER ** -0.5),
        'q_norm_w': 1.0 + nrm((DEPTH, NSA_HEAD_DIM), 0.02),
        'k_cmp_norm_w': 1.0 + nrm((DEPTH, NSA_HEAD_DIM), 0.02),
        'k_sel_norm_w': 1.0 + nrm((DEPTH, NSA_HEAD_DIM), 0.02),
        'k_win_norm_w': 1.0 + nrm((DEPTH, NSA_HEAD_DIM), 0.02),
        'cmp_pe_k': nrm((DEPTH, CMP_BLOCK, NSA_HEAD_DIM), 0.2),
        'cmp_w_k': nrm((DEPTH, CMP_BLOCK, NSA_HEAD_DIM, NSA_HEAD_DIM), (CMP_BLOCK * NSA_HEAD_DIM) ** -0.5),
        'cmp_pe_v': nrm((DEPTH, CMP_BLOCK, NSA_HEAD_DIM), 0.2),
        'cmp_w_v': nrm((DEPTH, CMP_BLOCK, NSA_HEAD_DIM, NSA_HEAD_DIM), (CMP_BLOCK * NSA_HEAD_DIM) ** -0.5),
        'rel_bias': nrm((N_BUCKETS, N_Q_HEADS), 0.2),
        'w_nsa_out': nrm((DEPTH, NSA_WIDTH, D_MODEL), NSA_WIDTH ** -0.5),
        'w_out': nrm((DEPTH, D_MODEL, D_MODEL), D_MODEL ** -0.5),
        'norm_ffn_w': 1.0 + nrm((DEPTH, D_MODEL), 0.02),
        'w_router': nrm((DEPTH, D_MODEL, N_EXPERTS), D_MODEL ** -0.5),
        'b_router': nrm((DEPTH, N_EXPERTS), 0.01),
        'w_gate_up': nrm((DEPTH, N_EXPERTS, D_MODEL, 2 * D_FF), D_MODEL ** -0.5),
        'b_gate_up': nrm((DEPTH, N_EXPERTS, 2 * D_FF), 0.01),
        'w_down': nrm((DEPTH, N_EXPERTS, D_FF, D_MODEL), D_FF ** -0.5),
        'b_down': nrm((DEPTH, N_EXPERTS, D_MODEL), 0.01),
    }


def reference(x_prompt, x_sample, cache_cmp, cache_sel, cache_win, state_ssm, state_conv, page_table,
              norm_mix_w, w_in, conv_w, conv_b, dt_bias, a_log, d_skip, ssm_norm_w, w_ssm_out,
              q_norm_w, k_cmp_norm_w, k_sel_norm_w, k_win_norm_w, cmp_pe_k, cmp_w_k, cmp_pe_v, cmp_w_v,
              rel_bias, w_nsa_out, w_out, norm_ffn_w, w_router, b_router, w_gate_up, b_gate_up, w_down, b_down):
    xp, xs = x_prompt, x_sample
    B, S, _ = xp.shape
    DB, T, _ = xs.shape
    p_cmp, p_sel, p_win, p_ssm, p_conv = [], [], [], [], []
    s_cmp, s_sel, s_win, s_ssm, s_conv = [], [], [], [], []
    for l in range(DEPTH):
        z, xbc, dt, q, kv_c, kv_s, kv_w, br_g, g_ssm, g_nsa = project(
            xp, norm_mix_w[l], w_in[l], q_norm_w[l], k_sel_norm_w[l], k_win_norm_w[l])
        y_ssm, conv_new, h_new = mamba_mixer(
            z, xbc, dt, jnp.zeros((B, D_CONV - 1, CONV_DIM), xbc.dtype),
            jnp.zeros((B, N_SSM_HEADS, SSM_HEAD_DIM, D_STATE), jnp.float32),
            conv_w[l], conv_b[l], dt_bias[l], a_log[l], d_skip[l], ssm_norm_w[l])
        y_nsa = nsa_prompt(q, kv_c, kv_s, kv_w, br_g, k_cmp_norm_w[l], cmp_pe_k[l], cmp_w_k[l],
                           cmp_pe_v[l], cmp_w_v[l], rel_bias)
        hp = merge(xp, y_ssm, y_nsa, g_ssm, g_nsa, w_ssm_out[l], w_nsa_out[l], w_out[l])
        p_cmp.append(kv_c)
        p_sel.append(kv_s)
        p_win.append(kv_w[:, -min(WINDOW, S):])
        p_ssm.append(h_new)
        p_conv.append(conv_new)
        z, xbc, dt, q, kv_c, kv_s, kv_w, br_g, g_ssm, g_nsa = project(
            xs, norm_mix_w[l], w_in[l], q_norm_w[l], k_sel_norm_w[l], k_win_norm_w[l])
        y_ssm, conv_new, h_new = mamba_mixer(
            z, xbc, dt, state_conv[l], state_ssm[l],
            conv_w[l], conv_b[l], dt_bias[l], a_log[l], d_skip[l], ssm_norm_w[l])
        y_nsa, win_new = nsa_sample(q, kv_c, kv_s, kv_w, br_g, cache_cmp, cache_sel, cache_win[l], page_table, l,
                                    k_cmp_norm_w[l], cmp_pe_k[l], cmp_w_k[l], cmp_pe_v[l], cmp_w_v[l], rel_bias)
        hs = merge(xs, y_ssm, y_nsa, g_ssm, g_nsa, w_ssm_out[l], w_nsa_out[l], w_out[l])
        s_cmp.append(kv_c)
        s_sel.append(kv_s)
        s_win.append(win_new)
        s_ssm.append(h_new)
        s_conv.append(conv_new)
        h_all = jnp.concatenate([hp.reshape(B * S, D_MODEL), hs.reshape(DB * T, D_MODEL)], axis=0)
        y_all = h_all + moe(rmsnorm(h_all, norm_ffn_w[l]), w_router[l], b_router[l], w_gate_up[l],
                            b_gate_up[l], w_down[l], b_down[l])
        xp = y_all[: B * S].reshape(B, S, D_MODEL)
        xs = y_all[B * S:].reshape(DB, T, D_MODEL)
    return (xp, xs,
            jnp.stack(p_cmp), jnp.stack(p_sel), jnp.stack(p_win), jnp.stack(p_ssm), jnp.stack(p_conv),
            jnp.stack(s_cmp), jnp.stack(s_sel), jnp.stack(s_win), jnp.stack(s_ssm), jnp.stack(s_conv))
```

```python
import functools
import math

import jax
import jax.numpy as jnp
import numpy as np
from jax import lax
from jax.experimental import pallas as pl
from jax.experimental.pallas import tpu as pltpu

F32 = jnp.float32
BF16 = jnp.bfloat16

SSM_HEAD_DIM = 64
N_SSM_GROUPS = 4
D_STATE = 128
D_CONV = 4
SSD_CHUNK = 128
NSA_HEAD_DIM = 64
N_Q_HEADS = 16
N_KV_HEADS = 4
Q_PER_KV = N_Q_HEADS // N_KV_HEADS
CMP_BLOCK = 32
CMP_STRIDE = 16
SEL_BLOCK = 64
N_SEL_BLOCKS = 16
WINDOW = 512
N_BUCKETS = 32
MAX_DISTANCE = 128
TOP_K = 4
SWIGLU_LIMIT = 7.0
SWIGLU_ALPHA = 1.702
RMS_EPS = 1e-6
NEG = -1e30
FORCE_SCORE = 1e9

LANES = 128
SUBLANES = 8
QT = 128
VMEM_LIMIT = 56 * 1024 * 1024


def _cparams(sem):
    return pltpu.CompilerParams(dimension_semantics=sem, vmem_limit_bytes=VMEM_LIMIT)


def _bdot(a, b):
    return jnp.dot(a.astype(BF16), b.astype(BF16), preferred_element_type=F32)


def _bdot_nt(a, b):
    return lax.dot_general(a.astype(BF16), b.astype(BF16), (((1,), (1,)), ((), ())),
                           preferred_element_type=F32)


def _split3(a):
    hi = a.astype(BF16)
    r = a - hi.astype(F32)
    mid = r.astype(BF16)
    lo = (r - mid.astype(F32)).astype(BF16)
    return hi, mid, lo


def _dot3(a, b):
    hi, mid, lo = _split3(a)
    d = lambda p: jnp.dot(p, b, preferred_element_type=F32)
    return (d(hi) + d(mid)) + d(lo)


def _dot3_l(a, b):
    hi, mid, lo = _split3(b)
    d = lambda p: jnp.dot(a, p, preferred_element_type=F32)
    return (d(hi) + d(mid)) + d(lo)


def _dot3_nt_l(a, b):
    hi, mid, lo = _split3(b)
    d = lambda p: lax.dot_general(a, p, (((1,), (1,)), ((), ())), preferred_element_type=F32)
    return (d(hi) + d(mid)) + d(lo)


def _seg_sum(y, ones_blk):
    c = ones_blk.shape[0]
    outs = []
    for k in range(y.shape[1] // c):
        outs.append(_dot3(y[:, k * c:(k + 1) * c], ones_blk))
    return outs[0] if len(outs) == 1 else jnp.concatenate(outs, axis=1)


def _sigmoid(x):
    return 1.0 / (1.0 + jnp.exp(-x))


def _rms_rows(x, w):
    ms = jnp.mean(x * x, axis=-1, keepdims=True)
    return (x * lax.rsqrt(ms + RMS_EPS)) * w


def _head_rms(y, ones64, w, seg):
    ms = _seg_sum(y * y, ones64) * (1.0 / seg)
    return (y * lax.rsqrt(ms + RMS_EPS)) * w


def _proj_kernel(kinds, x_ref, nw_ref, ones_ref, hw_ref, *refs):
    n = len(kinds)
    w_refs, o_refs = refs[:n], refs[n:]
    xn = _rms_rows(x_ref[...], nw_ref[...]).astype(BF16)
    ones64 = ones_ref[...]
    for kind, w_ref, o_ref in zip(kinds, w_refs, o_refs):
        y = jnp.dot(xn, w_ref[...], preferred_element_type=F32)
        if kind == "sig":
            y = _sigmoid(y)
        elif kind == "q":
            y = _head_rms(y, ones64, hw_ref[0:1, :], NSA_HEAD_DIM) * (NSA_HEAD_DIM ** -0.5)
        elif kind in ("ks", "kw"):
            row = 1 if kind == "ks" else 2
            half = y.shape[1] // 2
            k = _head_rms(y[:, :half], ones64, hw_ref[row:row + 1, :half], NSA_HEAD_DIM)
            y = jnp.concatenate([k, y[:, half:]], axis=1)
        o_ref[...] = y


def _proj(x, norm_w, ones64, head_w, weights, kinds, tm):
    n, d = x.shape
    assert n % tm == 0
    const = lambda i: (0, 0)
    in_specs = [pl.BlockSpec((tm, d), lambda i: (i, 0)),
                pl.BlockSpec((1, d), const),
                pl.BlockSpec(ones64.shape, const),
                pl.BlockSpec(head_w.shape, const)]
    in_specs += [pl.BlockSpec(w.shape, const) for w in weights]
    out_specs = [pl.BlockSpec((tm, w.shape[1]), lambda i: (i, 0)) for w in weights]
    out_shape = [jax.ShapeDtypeStruct((n, w.shape[1]), F32) for w in weights]
    return pl.pallas_call(
        functools.partial(_proj_kernel, tuple(kinds)),
        grid=(n // tm,), in_specs=in_specs, out_specs=out_specs, out_shape=out_shape,
        compiler_params=_cparams(("parallel",)), name="in_proj",
    )(x, norm_w, ones64, head_w, *weights)


def _softplus(x):
    return jnp.maximum(x, 0.0) + jnp.log1p(jnp.exp(-jnp.abs(x)))


def _ssd_kernel(xbc_ref, z_ref, dt_ref, sg_ref, cw_ref, cb_ref, hp_ref, dsk_ref, nw_ref, e_ref, tri_ref,
                onesg_ref, wout_ref, o_ref, ht_ref, ht_s, ext_s):
    c = pl.program_id(1)
    q = SSD_CHUNK
    d_inner = z_ref.shape[1]
    gw = d_inner // N_SSM_GROUPS
    hpg = gw // SSM_HEAD_DIM
    tail = SUBLANES

    @pl.when(c == 0)
    def _():
        ht_s[...] = jnp.zeros_like(ht_s)
        ext_s[0:tail, :] = jnp.zeros((tail, ext_s.shape[1]), F32)

    xb = xbc_ref[...]
    ext_s[tail:tail + q, :] = xb
    u = cb_ref[...] + cw_ref[D_CONV - 1:D_CONV, :] * xb
    for k in range(1, D_CONV):
        u = u + cw_ref[D_CONV - 1 - k:D_CONV - k, :] * ext_s[tail - k:tail - k + q, :]
    ext_s[0:tail, :] = xb[q - tail:q, :]
    u = u * _sigmoid(u)
    xs = u[:, :d_inner]
    bm = u[:, d_inner:d_inner + N_SSM_GROUPS * D_STATE]
    cm = u[:, d_inner + N_SSM_GROUPS * D_STATE:]

    dtv = _softplus(dt_ref[...] + hp_ref[0:1, :])
    a = dtv * (-jnp.exp(hp_ref[1:2, :]))
    acum = _dot3_l(tri_ref[...], a)
    acum_t = acum.T
    eacum = jnp.exp(acum)
    w_end = jnp.exp(acum[q - 1:q, :] - acum) * dtv
    e = e_ref[...]
    dt_x = _dot3(dtv, e)
    we_x = _dot3(w_end, e)
    ea_x = _dot3(eacum, e)
    xdt = (xs * dt_x).astype(BF16)
    xdtw = (xs * we_x).astype(BF16)
    ii = lax.broadcasted_iota(jnp.int32, (q, q), 0)
    jj = lax.broadcasted_iota(jnp.int32, (q, q), 1)
    causal = jj <= ii

    ys = []
    for g in range(N_SSM_GROUPS):
        cg = cm[:, g * D_STATE:(g + 1) * D_STATE].astype(BF16)
        bg = bm[:, g * D_STATE:(g + 1) * D_STATE]
        cbm = _bdot_nt(cg, bg)
        bg_t = bg.T.astype(BF16)
        sl = slice(g * gw, (g + 1) * gw)
        st = jnp.dot(bg_t, xdtw[:, sl], preferred_element_type=F32)
        hg = ht_s[:, sl]
        y_g = jnp.dot(cg, hg.astype(BF16), preferred_element_type=F32) * ea_x[:, sl]
        yh = []
        for r in range(hpg):
            h = g * hpg + r
            seg = acum[:, h:h + 1] - acum_t[h:h + 1, :]
            dec = jnp.exp(jnp.where(causal, seg, NEG))
            m = (cbm * dec).astype(BF16)
            yh.append(jnp.dot(m, xdt[:, h * SSM_HEAD_DIM:(h + 1) * SSM_HEAD_DIM], preferred_element_type=F32))
        ys.append(y_g + jnp.concatenate(yh, axis=1))
        ht_s[:, sl] = hg * ea_x[q - 1:q, sl] + st
    y = jnp.concatenate(ys, axis=1) + dsk_ref[...] * xs
    zz = z_ref[...]
    y = y * (zz * _sigmoid(zz))
    ms = _seg_sum(y * y, onesg_ref[...]) * (1.0 / gw)
    y = (y * lax.rsqrt(ms + RMS_EPS)) * nw_ref[...]
    o_ref[...] = sg_ref[...] * _bdot(y, wout_ref[...])

    @pl.when(c == pl.num_programs(1) - 1)
    def _():
        ht_ref[0] = ht_s[...]


def _ssd_prompt(xbc, z, dt, sg, conv_w, conv_b, headp, dsk, nw, expand, tri, onesg, wout, batch):
    n, conv_dim = xbc.shape
    d_inner = z.shape[1]
    d_model = wout.shape[1]
    q = SSD_CHUNK
    nc = n // batch // q
    const = lambda b, c: (0, 0)
    rows = lambda b, c: (b * nc + c, 0)
    in_specs = [pl.BlockSpec((q, conv_dim), rows), pl.BlockSpec((q, d_inner), rows),
                pl.BlockSpec((q, LANES), rows), pl.BlockSpec((q, d_model), rows),
                pl.BlockSpec(conv_w.shape, const), pl.BlockSpec(conv_b.shape, const),
                pl.BlockSpec(headp.shape, const), pl.BlockSpec(dsk.shape, const),
                pl.BlockSpec(nw.shape, const), pl.BlockSpec(expand.shape, const),
                pl.BlockSpec(tri.shape, const), pl.BlockSpec(onesg.shape, const),
                pl.BlockSpec(wout.shape, const)]
    out_specs = [pl.BlockSpec((q, d_model), rows),
                 pl.BlockSpec((1, D_STATE, d_inner), lambda b, c: (b, 0, 0))]
    out_shape = [jax.ShapeDtypeStruct((n, d_model), F32),
                 jax.ShapeDtypeStruct((batch, D_STATE, d_inner), F32)]
    return pl.pallas_call(
        _ssd_kernel, grid=(batch, nc), in_specs=in_specs, out_specs=out_specs, out_shape=out_shape,
        scratch_shapes=[pltpu.VMEM((D_STATE, d_inner), F32), pltpu.VMEM((SUBLANES + q, conv_dim), F32)],
        compiler_params=_cparams(("parallel", "arbitrary")), name="ssd_prompt",
    )(xbc, z, dt, sg, conv_w, conv_b, headp, dsk, nw, expand, tri, onesg, wout)


def _cmp_build_kernel(kvc_ref, wk_ref, wv_ref, pek_ref, pev_ref, ones_ref, nw_ref, kc_ref, vc_ref):
    ns = kc_ref.shape[0]
    kw = kc_ref.shape[1]
    lo_k = hi_k = lo_v = hi_v = None
    add = lambda acc, v: v if acc is None else acc + v
    nch = 2 * kw // LANES
    for i in range(CMP_STRIDE):
        x = jnp.concatenate([kvc_ref[pl.ds(nch * i + c, ns, stride=nch * CMP_STRIDE), :] for c in range(nch)], axis=1)
        xk, xv = x[:, :kw], x[:, kw:]
        j = CMP_STRIDE + i
        lo_k = add(lo_k, _bdot(xk + pek_ref[i:i + 1, :], wk_ref[i]))
        hi_k = add(hi_k, _bdot(xk + pek_ref[j:j + 1, :], wk_ref[j]))
        lo_v = add(lo_v, _bdot(xv + pev_ref[i:i + 1, :], wv_ref[i]))
        hi_v = add(hi_v, _bdot(xv + pev_ref[j:j + 1, :], wv_ref[j]))
    kc = lo_k + pltpu.roll(hi_k, ns - 1, axis=0)
    vc_ref[...] = lo_v + pltpu.roll(hi_v, ns - 1, axis=0)
    kc_ref[...] = _head_rms(kc, ones_ref[...], nw_ref[...], NSA_HEAD_DIM)


def _cmp_build(kvc, wk, wv, pek, pev, ones64, nw, batch):
    n, w = kvc.shape
    s = n // batch
    ns = s // CMP_STRIDE
    kw = w // 2
    c2 = lambda b: (0, 0)
    c3 = lambda b: (0, 0, 0)
    nch = w // LANES
    kvc = kvc.reshape(n * nch, LANES)
    return pl.pallas_call(
        _cmp_build_kernel, grid=(batch,),
        in_specs=[pl.BlockSpec((s * nch, LANES), lambda b: (b, 0)), pl.BlockSpec(wk.shape, c3),
                  pl.BlockSpec(wv.shape, c3),
                  pl.BlockSpec(pek.shape, c2), pl.BlockSpec(pev.shape, c2), pl.BlockSpec(ones64.shape, c2),
                  pl.BlockSpec(nw.shape, c2)],
        out_specs=[pl.BlockSpec((ns, kw), lambda b: (b, 0)), pl.BlockSpec((ns, kw), lambda b: (b, 0))],
        out_shape=[jax.ShapeDtypeStruct((batch * ns, kw), F32)] * 2,
        compiler_params=_cparams(("parallel",)), name="cmp_build",
    )(kvc, wk, wv, pek, pev, ones64, nw)


def _rank_select(v, n_keep):
    n = v.shape[0]
    jidx = lax.broadcasted_iota(jnp.int32, v.shape, 0)
    cnt = jnp.zeros(v.shape, F32)
    for i in range(n):
        ri = v[i:i + 1, :]
        cnt = cnt + jnp.where(jidx > i, jnp.where(ri >= v, 1.0, 0.0), jnp.where(ri > v, 1.0, 0.0))
    return jnp.where(cnt < n_keep, 1.0, 0.0)


def _cmp_attn_kernel(q_ref, brg_ref, kc_ref, vc_ref, tb_ref, gexp_ref, ocmp_ref, sel_ref):
    qi = pl.program_id(1)
    t0 = qi * QT
    ns = kc_ref.shape[0]
    nsel = sel_ref.shape[2]
    nb = tb_ref.shape[2]
    band = 2 * CMP_STRIDE
    tt = t0 + lax.broadcasted_iota(jnp.int32, (QT, ns), 0)
    nn = lax.broadcasted_iota(jnp.int32, (QT, ns), 1)
    mask = (CMP_STRIDE * nn + (CMP_BLOCK - 1)) <= tt
    first = (QT // CMP_STRIDE) * qi - band // 2
    cc = lax.broadcasted_iota(jnp.int32, (nb, ns), 0)
    n2 = lax.broadcasted_iota(jnp.int32, (nb, ns), 1)
    shift = jnp.where(((cc < band) & (n2 == first + cc)) | ((cc == band) & (n2 < first)), 1.0, 0.0).astype(BF16)
    jb = lax.broadcasted_iota(jnp.int32, (nsel, ns), 0) * SEL_BLOCK
    cs = lax.broadcasted_iota(jnp.int32, (nsel, ns), 1) * CMP_STRIDE
    cover_t = jnp.where((cs < jb + SEL_BLOCK) & (cs + CMP_BLOCK > jb), 1.0, 0.0).astype(BF16)
    jidx = lax.broadcasted_iota(jnp.int32, (nsel, QT), 0)
    tq = t0 + lax.broadcasted_iota(jnp.int32, (nsel, QT), 1)
    valid = jidx * SEL_BLOCK <= tq
    cur = tq // SEL_BLOCK
    forced = valid & ((jidx == 0) | (jidx == cur) | (jidx == cur - 1))

    q = q_ref[...]
    outs = []
    for g in range(N_KV_HEADS):
        kg = kc_ref[:, g * NSA_HEAD_DIM:(g + 1) * NSA_HEAD_DIM].astype(BF16)
        vg = vc_ref[:, g * NSA_HEAD_DIM:(g + 1) * NSA_HEAD_DIM].astype(BF16)
        psum = jnp.zeros((QT, ns), F32)
        for r in range(Q_PER_KV):
            h = g * Q_PER_KV + r
            s = _bdot_nt(q[:, h * NSA_HEAD_DIM:(h + 1) * NSA_HEAD_DIM], kg)
            s = jnp.where(mask, s + _dot3(tb_ref[h], shift), NEG)
            m = jnp.max(s, axis=-1, keepdims=True)
            p = jnp.where(mask, jnp.exp(s - m), 0.0)
            l = jnp.sum(p, axis=-1, keepdims=True)
            p = p * jnp.where(l > 0.0, 1.0 / l, 0.0)
            outs.append(_bdot(p, vg))
            psum = psum + p
        imp_t = _dot3_nt_l(cover_t, psum)
        v = jnp.where(forced, FORCE_SCORE, jnp.where(valid, imp_t, NEG))
        sel_ref[0, g] = _rank_select(v, N_SEL_BLOCKS)
    ocmp_ref[...] = jnp.concatenate(outs, axis=1) * _dot3(brg_ref[...], gexp_ref[0])


def _cmp_attn(q, brg, kc, vc, tb, gexp, batch):
    n, qw = q.shape
    s = n // batch
    nq = s // QT
    ns, kw = kc.shape[0] // batch, kc.shape[1]
    nsel = s // SEL_BLOCK
    rows = lambda b, i: (b * nq + i, 0)
    per_b = lambda b, i: (b, 0)
    return pl.pallas_call(
        _cmp_attn_kernel, grid=(batch, nq),
        in_specs=[pl.BlockSpec((QT, qw), rows), pl.BlockSpec((QT, LANES), rows),
                  pl.BlockSpec((ns, kw), per_b), pl.BlockSpec((ns, kw), per_b),
                  pl.BlockSpec(tb.shape, lambda b, i: (0, 0, 0)),
                  pl.BlockSpec((1,) + gexp.shape[1:], lambda b, i: (0, 0, 0))],
        out_specs=[pl.BlockSpec((QT, qw), rows), pl.BlockSpec((1, N_KV_HEADS, nsel, QT), lambda b, i: (b, 0, 0, i))],
        out_shape=[jax.ShapeDtypeStruct((n, qw), F32), jax.ShapeDtypeStruct((batch, N_KV_HEADS, nsel, s), F32)],
        compiler_params=_cparams(("parallel", "parallel")), name="cmp_attn",
    )(q, brg, kc, vc, tb, gexp)


def _attn_first(s, vt):
    m = jnp.max(s, axis=0, keepdims=True)
    p = jnp.exp(s - m)
    return m, jnp.sum(p, axis=0, keepdims=True), jnp.dot(vt, p.astype(BF16), preferred_element_type=F32)


def _attn_next(state, s, vt):
    m, l, acc = state
    m_new = jnp.maximum(m, jnp.max(s, axis=0, keepdims=True))
    alpha = jnp.exp(m - m_new)
    p = jnp.exp(s - m_new)
    return (m_new, alpha * l + jnp.sum(p, axis=0, keepdims=True),
            alpha * acc + jnp.dot(vt, p.astype(BF16), preferred_element_type=F32))


def _masked_scores(s, bias, keep):
    parts = [jnp.where(keep, s[:, r * QT:(r + 1) * QT] + bias[r], NEG) for r in range(len(bias))]
    return jnp.concatenate(parts, axis=1)


def _nsa_main_kernel(c31_ref, q_ref, brg_ref, sel_ref, ocmp_ref, ks_ref, vs_ref, kw_ref, vw_ref, b0_ref, b1_ref,
                     mssm_ref, sgn_ref, x_ref, wn_ref, wo_ref, h_ref):
    qi = pl.program_id(1)
    nwt = WINDOW // QT
    blk_per_tile = QT // SEL_BLOCK
    jk = lax.broadcasted_iota(jnp.int32, (QT, QT), 0)
    iq = lax.broadcasted_iota(jnp.int32, (QT, QT), 1)
    causal_t = jk <= iq
    q_t = q_ref[...].T.astype(BF16)
    gates_t = brg_ref[...].T

    def sel_keep(g, kt):
        rows = [jnp.broadcast_to(sel_ref[0, g, pl.ds(blk_per_tile * kt + b, 1), :], (SEL_BLOCK, QT))
                for b in range(blk_per_tile)]
        return jnp.concatenate(rows, axis=0) > 0.5

    def key_tile(ref, g, kt):
        return ref[0, g, pl.ds(pl.multiple_of(kt * QT, QT), QT), :]

    o_t = []
    for g in range(N_KV_HEADS):
        heads = [g * Q_PER_KV + r for r in range(Q_PER_KV)]
        qg = jnp.concatenate([q_t[h * NSA_HEAD_DIM:(h + 1) * NSA_HEAD_DIM, :] for h in heads], axis=1)
        far = [c31_ref[h] for h in heads]
        near0 = [b0_ref[h] for h in heads]
        near1 = [b1_ref[h] for h in heads]
        scores = lambda ref, kt: jnp.dot(key_tile(ref, g, kt), qg, preferred_element_type=F32)
        prev = jnp.maximum(qi - 1, 0)

        st = _attn_first(_masked_scores(scores(ks_ref, qi), near0, causal_t & sel_keep(g, qi)), vs_ref[0, g, qi])
        st = _attn_next(st, _masked_scores(scores(ks_ref, prev), near1, sel_keep(g, prev) & (qi >= 1)),
                        vs_ref[0, g, prev])

        def sel_body(kt, state):
            return _attn_next(state, _masked_scores(scores(ks_ref, kt), far, sel_keep(g, kt)), vs_ref[0, g, kt])

        m_s, l_s, acc_s = lax.fori_loop(0, qi - 1, sel_body, st)

        st = _attn_first(_masked_scores(scores(kw_ref, qi), near0, causal_t), vw_ref[0, g, qi])
        st = _attn_next(st, _masked_scores(scores(kw_ref, prev), near1, jnp.broadcast_to(qi >= 1, (QT, QT))),
                        vw_ref[0, g, prev])
        for back in range(2, nwt + 1):
            kt = jnp.maximum(qi - back, 0)
            keep = jnp.broadcast_to(qi >= back, (QT, QT))
            if back == nwt:
                keep = keep & (jk > iq)
            st = _attn_next(st, _masked_scores(scores(kw_ref, kt), far, keep), vw_ref[0, g, kt])
        m_w, l_w, acc_w = st

        o_s = acc_s * (1.0 / l_s)
        o_w = acc_w * (1.0 / l_w)
        for r, h in enumerate(heads):
            sl = slice(r * QT, (r + 1) * QT)
            o_t.append(gates_t[3 * h + 1:3 * h + 2, :] * o_s[:, sl] + gates_t[3 * h + 2:3 * h + 3, :] * o_w[:, sl])

    pairs = [jnp.concatenate(o_t[2 * k:2 * k + 2], axis=0).T for k in range(len(o_t) // 2)]
    y_nsa = jnp.concatenate(pairs, axis=1) + ocmp_ref[...]
    u = mssm_ref[...] + sgn_ref[...] * _bdot(y_nsa, wn_ref[...])
    h_ref[...] = x_ref[...] + _bdot(u, wo_ref[...])


def _nsa_main(c31, q, brg, sel, ocmp, ks, vs_t, kw, vw_t, b0, b1, mssm, sgn, x, wn, wo, batch):
    n, qw = q.shape
    d = x.shape[1]
    s = n // batch
    nq = s // QT
    rows = lambda b, i: (b * nq + i, 0)
    kspec = pl.BlockSpec((1,) + ks.shape[1:], lambda b, i: (b, 0, 0, 0))
    vspec = pl.BlockSpec((1,) + vs_t.shape[1:], lambda b, i: (b, 0, 0, 0, 0))
    c2 = lambda b, i: (0, 0)
    c3 = lambda b, i: (0, 0, 0)
    return pl.pallas_call(
        _nsa_main_kernel, grid=(batch, nq),
        in_specs=[pl.BlockSpec(memory_space=pltpu.SMEM),
                  pl.BlockSpec((QT, qw), rows), pl.BlockSpec((QT, LANES), rows),
                  pl.BlockSpec((1,) + sel.shape[1:3] + (QT,), lambda b, i: (b, 0, 0, i)),
                  pl.BlockSpec((QT, qw), rows), kspec, vspec, kspec, vspec,
                  pl.BlockSpec(b0.shape, c3), pl.BlockSpec(b1.shape, c3),
                  pl.BlockSpec((QT, d), rows), pl.BlockSpec((QT, d), rows), pl.BlockSpec((QT, d), rows),
                  pl.BlockSpec(wn.shape, c2), pl.BlockSpec(wo.shape, c2)],
        out_specs=pl.BlockSpec((QT, d), rows), out_shape=jax.ShapeDtypeStruct((n, d), F32),
        compiler_params=_cparams(("parallel", "parallel")), name="nsa_main",
    )(c31, q, brg, sel, ocmp, ks, vs_t, kw, vw_t, b0, b1, mssm, sgn, x, wn, wo)


def _router_kernel(h_ref, nw_ref, wr_ref, br_ref, hn_ref, gate_ref):
    hn = _rms_rows(h_ref[...], nw_ref[...])
    hn_ref[...] = hn.astype(BF16)
    a_hi = hn.astype(BF16)
    a_lo = (hn - a_hi.astype(F32)).astype(BF16)
    w = wr_ref[...]
    w_hi = w.astype(BF16)
    w_lo = (w - w_hi.astype(F32)).astype(BF16)
    d = lambda a, b: jnp.dot(a, b, preferred_element_type=F32)
    v = (d(a_hi, w_hi) + (d(a_hi, w_lo) + d(a_lo, w_hi))) + br_ref[...]
    lane = lax.broadcasted_iota(jnp.int32, v.shape, 1)
    tops, hots = [], []
    for _ in range(TOP_K):
        m = jnp.max(v, axis=-1, keepdims=True)
        idx = jnp.min(jnp.where(v == m, lane, LANES), axis=-1, keepdims=True)
        hot = lane == idx
        tops.append(m)
        hots.append(hot)
        v = jnp.where(hot, NEG, v)
    es = [jnp.exp(t - tops[0]) for t in tops]
    inv = 1.0 / functools.reduce(lambda a, b: a + b, es)
    gate = jnp.zeros(v.shape, F32)
    for e, hot in zip(es, hots):
        gate = gate + jnp.where(hot, e * inv, 0.0)
    gate_ref[...] = gate


def _router(h, nw, wr, br, tm):
    n, d = h.shape
    c2 = lambda i: (0, 0)
    return pl.pallas_call(
        _router_kernel, grid=(n // tm,),
        in_specs=[pl.BlockSpec((tm, d), lambda i: (i, 0)), pl.BlockSpec(nw.shape, c2),
                  pl.BlockSpec(wr.shape, c2), pl.BlockSpec(br.shape, c2)],
        out_specs=[pl.BlockSpec((tm, d), lambda i: (i, 0)), pl.BlockSpec((tm, LANES), lambda i: (i, 0))],
        out_shape=[jax.ShapeDtypeStruct((n, d), BF16), jax.ShapeDtypeStruct((n, LANES), F32)],
        compiler_params=_cparams(("parallel",)), name="moe_router",
    )(h, nw, wr, br)


def _moe_dense_kernel(hn_ref, gate_ref, h_ref, w1_ref, b1_ref, w2_ref, b2_ref, o_ref, acc_s):
    e = pl.program_id(1)

    @pl.when(e == 0)
    def _():
        acc_s[...] = jnp.zeros_like(acc_s)

    d_ff = w2_ref.shape[1]
    y1 = jnp.dot(hn_ref[...], w1_ref[0], preferred_element_type=F32) + b1_ref[0]
    gt = jnp.minimum(y1[:, :d_ff], SWIGLU_LIMIT)
    up = jnp.clip(y1[:, d_ff:], -SWIGLU_LIMIT, SWIGLU_LIMIT)
    act = (up + 1.0) * gt * _sigmoid(SWIGLU_ALPHA * gt)
    y2 = _bdot(act, w2_ref[0]) + b2_ref[0]
    pick = jnp.where(lax.broadcasted_iota(jnp.int32, (LANES, LANES), 0) == e, 1.0, 0.0).astype(BF16)
    gcol = _dot3(gate_ref[...], pick)
    acc_s[...] += jnp.concatenate([gcol] * (y2.shape[1] // LANES), axis=1) * y2

    @pl.when(e == pl.num_programs(1) - 1)
    def _():
        o_ref[...] = h_ref[...] + acc_s[...]


def _moe_dense(hn, gate, h, w1, b1, w2, b2, tm):
    n, d = h.shape
    ne = w1.shape[0]
    rows = lambda i, e: (i, 0)
    per_e = lambda i, e: (e, 0, 0)
    return pl.pallas_call(
        _moe_dense_kernel, grid=(n // tm, ne),
        in_specs=[pl.BlockSpec((tm, d), rows), pl.BlockSpec((tm, LANES), rows), pl.BlockSpec((tm, d), rows),
                  pl.BlockSpec((1,) + w1.shape[1:], per_e), pl.BlockSpec((1,) + b1.shape[1:], per_e),
                  pl.BlockSpec((1,) + w2.shape[1:], per_e), pl.BlockSpec((1,) + b2.shape[1:], per_e)],
        out_specs=pl.BlockSpec((tm, d), rows), out_shape=jax.ShapeDtypeStruct((n, d), F32),
        scratch_shapes=[pltpu.VMEM((tm, d), F32)],
        compiler_params=_cparams(("parallel", "arbitrary")), name="moe_experts",
    )(hn, gate, h, w1, b1, w2, b2)


def _mamba_prep_kernel(xbc_ref, sconv_ref, dt_ref, cw_ref, cb_ref, hp_ref, e_ref,
                       conv_ref, xs_ref, bm_ref, ct_ref, dtx_ref, dec_ref):
    cdim = xbc_ref.shape[1]
    d_inner = xs_ref.shape[1]
    xb = xbc_ref[...]
    u = cb_ref[...] + cw_ref[D_CONV - 1:D_CONV, :] * xb
    for k in range(D_CONV - 1):
        u = u + cw_ref[k:k + 1, :] * sconv_ref[:, k * cdim:(k + 1) * cdim]
    conv_ref[:, :(D_CONV - 2) * cdim] = sconv_ref[:, cdim:]
    conv_ref[:, (D_CONV - 2) * cdim:] = xb
    u = u * _sigmoid(u)
    xs = u[:, :d_inner]
    xs_ref[...] = xs
    bm_ref[...] = u[:, d_inner:d_inner + N_SSM_GROUPS * D_STATE]
    cm = u[:, d_inner + N_SSM_GROUPS * D_STATE:]
    for g in range(N_SSM_GROUPS):
        ct_ref[g] = cm[:, g * D_STATE:(g + 1) * D_STATE].T
    dtv = _softplus(dt_ref[...] + hp_ref[0:1, :])
    dec_ref[...] = jnp.exp(dtv * (-jnp.exp(hp_ref[1:2, :])))
    dtx_ref[...] = (xs * _dot3(dtv, e_ref[...])).T


def _mamba_prep(xbc, sconv, dt, conv_w, conv_b, headp, expand):
    n, cdim = xbc.shape
    d_inner = expand.shape[1]
    gn = N_SSM_GROUPS * D_STATE
    out_shape = [jax.ShapeDtypeStruct(sconv.shape, F32), jax.ShapeDtypeStruct((n, d_inner), F32),
                 jax.ShapeDtypeStruct((n, gn), F32), jax.ShapeDtypeStruct((N_SSM_GROUPS, D_STATE, n), F32),
                 jax.ShapeDtypeStruct((d_inner, n), F32), jax.ShapeDtypeStruct((n, LANES), F32)]
    return pl.pallas_call(_mamba_prep_kernel, out_shape=out_shape,
                          compiler_params=pltpu.CompilerParams(vmem_limit_bytes=VMEM_LIMIT), name="mamba_prep",
                          )(xbc, sconv, dt, conv_w, conv_b, headp, expand)


def _bf16x3(a, b):
    a_hi = a.astype(BF16)
    a_lo = (a - a_hi.astype(F32)).astype(BF16)
    b_hi = b.astype(BF16)
    b_lo = (b - b_hi.astype(F32)).astype(BF16)
    d = lambda x, y: jnp.dot(x, y, preferred_element_type=F32)
    return d(a_hi, b_hi) + (d(a_hi, b_lo) + d(a_lo, b_hi))


def _mamba_state_kernel(dec_ref, h0_ref, dtx_ref, bm_ref, ct_ref, hn_ref, yt_ref):
    s = pl.program_id(0)
    n = bm_ref.shape[0]
    d_inner = dtx_ref.shape[0]
    gw = d_inner // N_SSM_GROUPS
    hpg = gw // SSM_HEAD_DIM

    @pl.when(s == 0)
    def _():
        yt_ref[...] = jnp.zeros_like(yt_ref)

    row_is_s = lax.broadcasted_iota(jnp.int32, (n, D_STATE), 0) == s
    col_is_s = lax.broadcasted_iota(jnp.int32, (D_STATE, n), 1) == s
    for g in range(N_SSM_GROUPS):
        b_s = jnp.where(row_is_s, bm_ref[:, g * D_STATE:(g + 1) * D_STATE], 0.0)
        st = _bf16x3(dtx_ref[g * gw:(g + 1) * gw, :], b_s)
        parts = []
        for r in range(hpg):
            h = g * hpg + r
            rows = slice(h * SSM_HEAD_DIM, (h + 1) * SSM_HEAD_DIM)
            parts.append(h0_ref[0, rows, :] * dec_ref[s, h] + st[r * SSM_HEAD_DIM:(r + 1) * SSM_HEAD_DIM, :])
        hn = jnp.concatenate(parts, axis=0)
        hn_ref[0, g * gw:(g + 1) * gw, :] = hn
        c_s = jnp.where(col_is_s, ct_ref[g], 0.0)
        yt_ref[g * gw:(g + 1) * gw, :] += _bdot(hn, c_s)


def _mamba_state(dec, h0, dtx_t, bm, ct):
    n, rows, ns = h0.shape
    c2 = lambda s: (0, 0)
    return pl.pallas_call(
        _mamba_state_kernel, grid=(n,),
        in_specs=[pl.BlockSpec(memory_space=pltpu.SMEM),
                  pl.BlockSpec((1, rows, ns), lambda s: (s, 0, 0)), pl.BlockSpec(dtx_t.shape, c2),
                  pl.BlockSpec(bm.shape, c2), pl.BlockSpec(ct.shape, lambda s: (0, 0, 0))],
        out_specs=[pl.BlockSpec((1, rows, ns), lambda s: (s, 0, 0)), pl.BlockSpec(dtx_t.shape, c2)],
        out_shape=[jax.ShapeDtypeStruct(h0.shape, F32), jax.ShapeDtypeStruct(dtx_t.shape, F32)],
        compiler_params=_cparams(("arbitrary",)), name="mamba_state",
    )(dec, h0, dtx_t, bm, ct)


def _page_cmp_kernel(pt_ref, cache_ref, new_ref, w_ref, pe_ref, ones_ref, nw_ref, out_ref, buf, rows_s, sem):
    t = pl.program_id(0)
    nsteps = pl.num_programs(0)
    n_pages = pt_ref.shape[1]
    page = buf.shape[3]
    nstr = out_ref.shape[2]
    half = out_ref.shape[3]
    nch = half // LANES
    past = n_pages * page

    def copies(step, slot):
        seq, kv = step // 2, step % 2
        return [pltpu.make_async_copy(cache_ref.at[pt_ref[seq, p], pl.ds(nch * kv, nch)], buf.at[slot, p], sem.at[slot])
                for p in range(n_pages)]

    @pl.when(t == 0)
    def _():
        for c in range(nch):
            rows_s[c, past:, :] = jnp.zeros((rows_s.shape[1] - past, LANES), F32)
        for cp in copies(t, 0):
            cp.start()

    slot = t % 2

    @pl.when(t + 1 < nsteps)
    def _():
        for cp in copies(t + 1, 1 - slot):
            cp.start()

    for cp in copies(t, slot):
        cp.wait()

    kv = t % 2
    def to_rows(p, carry):
        for c in range(nch):
            rows_s[c, pl.ds(pl.multiple_of(p * page, page), page), :] = buf[slot, p, c].T
        return carry

    lax.fori_loop(0, n_pages, to_rows, 0)
    new = new_ref[0]
    for c in range(nch):
        k_part = new[:, c * LANES:(c + 1) * LANES]
        v_part = new[:, half + c * LANES:half + (c + 1) * LANES]
        rows_s[c, past:past + 1, :] = jnp.where(kv == 0, k_part, v_part)

    lo = hi = None
    for i in range(CMP_STRIDE):
        x = jnp.concatenate([rows_s[c, pl.ds(i, nstr, stride=CMP_STRIDE), :] for c in range(nch)], axis=1)
        j = CMP_STRIDE + i
        a = _bdot(x + pe_ref[0, i:i + 1, :], w_ref[0, i])
        b = _bdot(x + pe_ref[0, j:j + 1, :], w_ref[0, j])
        lo = a if lo is None else lo + a
        hi = b if hi is None else hi + b
    tok = lo + pltpu.roll(hi, nstr - 1, axis=0)
    out_ref[0, 0] = jnp.where(kv == 0, _head_rms(tok, ones_ref[...], nw_ref[...], NSA_HEAD_DIM), tok)


def _page_cmp(page_table, cache_t, new_rows, w_kv, pe_kv, ones64, nw, nstr):
    nseq, n_pages = page_table.shape
    page = cache_t.shape[3]
    half = w_kv.shape[2]
    nch = half // LANES
    grid_spec = pltpu.PrefetchScalarGridSpec(
        num_scalar_prefetch=1, grid=(2 * nseq,),
        in_specs=[pl.BlockSpec(memory_space=pl.ANY),
                  pl.BlockSpec((1, 1, 2 * half), lambda t, pt: (t // 2, 0, 0)),
                  pl.BlockSpec((1,) + w_kv.shape[1:], lambda t, pt: (t % 2, 0, 0, 0)),
                  pl.BlockSpec((1,) + pe_kv.shape[1:], lambda t, pt: (t % 2, 0, 0)),
                  pl.BlockSpec(ones64.shape, lambda t, pt: (0, 0)), pl.BlockSpec(nw.shape, lambda t, pt: (0, 0))],
        out_specs=pl.BlockSpec((1, 1, nstr, half), lambda t, pt: (t // 2, t % 2, 0, 0)),
        scratch_shapes=[pltpu.VMEM((2, n_pages, nch, LANES, page), F32),
                        pltpu.VMEM((nch, nstr * CMP_STRIDE, LANES), F32),
                        pltpu.SemaphoreType.DMA((2,))])
    return pl.pallas_call(
        _page_cmp_kernel, grid_spec=grid_spec, out_shape=jax.ShapeDtypeStruct((nseq, 2, nstr, half), F32),
        compiler_params=_cparams(("arbitrary",)), name="page_cmp",
    )(page_table, cache_t, new_rows, w_kv, pe_kv, ones64, nw)


def _rows8(x):
    return jnp.broadcast_to(x, (SUBLANES, x.shape[1]))


def _group_q(q_row, g):
    parts = [q_row[:, (g * Q_PER_KV + r) * NSA_HEAD_DIM:(g * Q_PER_KV + r + 1) * NSA_HEAD_DIM]
             for r in range(Q_PER_KV)]
    parts.append(jnp.zeros((SUBLANES - Q_PER_KV, NSA_HEAD_DIM), F32))
    return jnp.concatenate(parts, axis=0)


def _heads_to_row(o):
    return jnp.concatenate([o[r:r + 1, :] for r in range(Q_PER_KV)], axis=1)


def _softmax_rows(s, keep):
    s = jnp.where(keep, s, NEG)
    m = jnp.max(s, axis=-1, keepdims=True)
    p = jnp.where(keep, jnp.exp(s - m), 0.0)
    l = jnp.sum(p, axis=-1, keepdims=True)
    return p * jnp.where(l > 0.0, 1.0 / l, 0.0)


def _sample_cw_kernel(q_ref, brg_ref, kvc_ref, win_ref, wnew_ref, tbc_ref, tbw_ref, gexp_ref,
                      o_ref, sel_ref, wout_ref, wall_s, *, tq, past_w):
    nstr = kvc_ref.shape[2]
    half = kvc_ref.shape[3]
    wlen = win_ref.shape[1]
    nsel_pad = sel_ref.shape[2]
    wrows = wall_s.shape[0]
    q_row = q_ref[0]
    wall_s[0:wlen, :] = win_ref[0]
    wall_s[wlen:wlen + 1, :] = wnew_ref[0]
    wall_s[wlen + 1:, :] = jnp.zeros((wrows - wlen - 1, wall_s.shape[1]), F32)
    wout_ref[0] = wall_s[1:wlen + 1, :]

    nn = lax.broadcasted_iota(jnp.int32, (SUBLANES, nstr), 1)
    keep_c = (CMP_STRIDE * nn + (CMP_BLOCK - 1)) <= tq
    wi = lax.broadcasted_iota(jnp.int32, (SUBLANES, wrows), 1)
    dw = tq - (past_w + wi)
    keep_w = (dw >= 0) & (dw < WINDOW) & (past_w + wi >= 0) & (wi <= wlen)
    jb = lax.broadcasted_iota(jnp.int32, (nsel_pad, nstr), 0) * SEL_BLOCK
    cs = lax.broadcasted_iota(jnp.int32, (nsel_pad, nstr), 1) * CMP_STRIDE
    cover_t = jnp.where((cs < jb + SEL_BLOCK) & (cs + CMP_BLOCK > jb), 1.0, 0.0).astype(BF16)
    ji = lax.broadcasted_iota(jnp.int32, (nsel_pad, LANES), 0)
    valid = ji * SEL_BLOCK <= tq
    cur = tq // SEL_BLOCK
    forced = valid & ((ji == 0) | (ji == cur) | (ji == cur - 1))
    ii = lax.broadcasted_iota(jnp.int32, (nsel_pad, nsel_pad), 0)
    jj = lax.broadcasted_iota(jnp.int32, (nsel_pad, nsel_pad), 1)

    sel_ref[0] = jnp.zeros(sel_ref.shape[1:], F32)
    oc, ow = [], []
    for g in range(N_KV_HEADS):
        qg = _group_q(q_row, g)
        lanes = slice(g * NSA_HEAD_DIM, (g + 1) * NSA_HEAD_DIM)
        vl = slice(half + g * NSA_HEAD_DIM, half + (g + 1) * NSA_HEAD_DIM)
        p = _softmax_rows(_bdot_nt(qg, kvc_ref[0, 0, :, lanes]) + tbc_ref[g], keep_c)
        oc.append(_heads_to_row(_bdot(p, kvc_ref[0, 1, :, lanes])))
        psum = jnp.sum(p[0:Q_PER_KV, :], axis=0, keepdims=True)
        imp = _dot3_nt_l(cover_t, jnp.broadcast_to(psum, (LANES, nstr)))
        v_col = jnp.where(forced, FORCE_SCORE, jnp.where(valid, imp, NEG))
        v_row = jnp.concatenate([v_col[k * LANES:(k + 1) * LANES, :].T for k in range(nsel_pad // LANES)], axis=1)
        a = jnp.broadcast_to(v_row[0:1, :], (nsel_pad, nsel_pad))
        b = jnp.concatenate([v_col] * (nsel_pad // LANES), axis=1)
        beats = jnp.where(ii < jj, jnp.where(b >= a, 1.0, 0.0), jnp.where(b > a, 1.0, 0.0))
        cnt = jnp.sum(beats, axis=0, keepdims=True)
        sel_ref[0, g:g + 1, :] = jnp.where(cnt < N_SEL_BLOCKS, 1.0, 0.0)
        pw = _softmax_rows(_bdot_nt(qg, wall_s[:, lanes]) + tbw_ref[g], keep_w)
        ow.append(_heads_to_row(_bdot(pw, wall_s[:, vl])))
    gates = _rows8(brg_ref[0])
    o = (jnp.concatenate(oc, axis=1) * _dot3(gates, gexp_ref[0])[0:1, :]
         + jnp.concatenate(ow, axis=1) * _dot3(gates, gexp_ref[2])[0:1, :])
    o_ref[0] = o


def _sample_cw(q, brg, kvc, win, wnew, tbc, tbw, gexp, tq, past_w, nsel_pad):
    nseq = q.shape[0]
    qw = q.shape[2]
    wlen, ww = win.shape[1], win.shape[2]
    wrows = -(-(wlen + 1) // SUBLANES) * SUBLANES
    per3 = lambda s: (s, 0, 0)
    c3 = lambda s: (0, 0, 0)
    return pl.pallas_call(
        functools.partial(_sample_cw_kernel, tq=tq, past_w=past_w), grid=(nseq,),
        in_specs=[pl.BlockSpec((1, 1, qw), per3), pl.BlockSpec((1, 1, LANES), per3),
                  pl.BlockSpec((1,) + kvc.shape[1:], lambda s: (s, 0, 0, 0)),
                  pl.BlockSpec((1, wlen, ww), per3), pl.BlockSpec((1, 1, ww), per3),
                  pl.BlockSpec(tbc.shape, c3), pl.BlockSpec(tbw.shape, c3), pl.BlockSpec(gexp.shape, c3)],
        out_specs=[pl.BlockSpec((1, 1, qw), per3), pl.BlockSpec((1, SUBLANES, nsel_pad), per3),
                   pl.BlockSpec((1, wlen, ww), per3)],
        out_shape=[jax.ShapeDtypeStruct((nseq, 1, qw), F32), jax.ShapeDtypeStruct((nseq, SUBLANES, nsel_pad), F32),
                   jax.ShapeDtypeStruct((nseq, wlen, ww), F32)],
        scratch_shapes=[pltpu.VMEM((wrows, ww), F32)],
        compiler_params=_cparams(("parallel",)), name="sample_cmp_win",
    )(q, brg, kvc, win, wnew, tbc, tbw, gexp)


def _sample_sel_kernel(pg_ref, c31_ref, f0_ref, cache_ref, q_ref, brg_ref, snew_ref, code_ref, ocw_ref, tbl_ref, gexp_ref,
                       o_ref, buf, sem):
    s = pl.program_id(0)
    nseq = pl.num_programs(0)
    nblk = pg_ref.shape[1] // N_KV_HEADS
    page = cache_ref.shape[4]
    half = snew_ref.shape[2] // 2

    def copies(seq, slot):
        out = []
        for g in range(N_KV_HEADS):
            for k in range(nblk):
                src = cache_ref.at[pg_ref[seq, g * nblk + k]]
                for kv in range(2):
                    out.append(pltpu.make_async_copy(src.at[kv, g], buf.at[slot, kv, g, :, pl.ds(k * page, page)],
                                                     sem.at[slot]))
        return out

    @pl.when(s == 0)
    def _():
        for cp in copies(s, 0):
            cp.start()

    slot = s % 2

    @pl.when(s + 1 < nseq)
    def _():
        for cp in copies(s + 1, 1 - slot):
            cp.start()

    for cp in copies(s, slot):
        cp.wait()

    q_row = q_ref[0]
    new = snew_ref[0]
    outs = []
    for g in range(N_KV_HEADS):
        qg = _group_q(q_row, g)
        heads = [g * Q_PER_KV + r for r in range(Q_PER_KV)]
        code = _rows8(code_ref[0, g:g + 1, :])
        far = jnp.concatenate([jnp.full((1, 1), c31_ref[h], F32) for h in heads]
                              + [jnp.zeros((SUBLANES - Q_PER_KV, 1), F32)], axis=0)
        near = jnp.concatenate([tbl_ref[g]] * nblk, axis=1)
        sc = jnp.dot(qg.astype(BF16), buf[slot, 0, g].astype(BF16), preferred_element_type=F32)
        sc = jnp.where(code > 0.5, sc + jnp.where(code > 1.5, near, far), NEG)
        k_new = new[:, g * NSA_HEAD_DIM:(g + 1) * NSA_HEAD_DIM]
        v_new = new[:, half + g * NSA_HEAD_DIM:half + (g + 1) * NSA_HEAD_DIM]
        f0 = jnp.concatenate([jnp.full((1, 1), f0_ref[h], F32) for h in heads]
                             + [jnp.zeros((SUBLANES - Q_PER_KV, 1), F32)], axis=0)
        s_new = jnp.sum(qg * _rows8(k_new), axis=-1, keepdims=True) + f0
        new_on = _rows8(code_ref[0, N_KV_HEADS + g:N_KV_HEADS + g + 1, 0:1]) > 0.5
        s_new = jnp.where(new_on, s_new, NEG)
        m = jnp.maximum(jnp.max(sc, axis=-1, keepdims=True), s_new)
        p = jnp.where(code > 0.5, jnp.exp(sc - m), 0.0)
        p_new = jnp.where(new_on, jnp.exp(s_new - m), 0.0)
        l = jnp.sum(p, axis=-1, keepdims=True) + p_new
        inv = jnp.where(l > 0.0, 1.0 / l, 0.0)
        o = _bdot_nt(p, buf[slot, 1, g]) + p_new * _rows8(v_new)
        outs.append(_heads_to_row(o * inv))
    gates = _dot3(_rows8(brg_ref[0]), gexp_ref[1])[0:1, :]
    o_ref[0] = ocw_ref[0] + jnp.concatenate(outs, axis=1) * gates


def _sample_sel(pages, c31, f0, cache_t, q, brg, snew, code, ocw, tbl, gexp):
    nseq, qw = q.shape[0], q.shape[2]
    nblk = pages.shape[1] // N_KV_HEADS
    page = cache_t.shape[4]
    per3 = lambda s, *_: (s, 0, 0)
    c3 = lambda s, *_: (0, 0, 0)
    grid_spec = pltpu.PrefetchScalarGridSpec(
        num_scalar_prefetch=1, grid=(nseq,),
        in_specs=[pl.BlockSpec(memory_space=pltpu.SMEM), pl.BlockSpec(memory_space=pltpu.SMEM),
                  pl.BlockSpec(memory_space=pl.ANY),
                  pl.BlockSpec((1, 1, qw), per3), pl.BlockSpec((1, 1, LANES), per3),
                  pl.BlockSpec((1, 1, snew.shape[2]), per3), pl.BlockSpec((1,) + code.shape[1:], per3),
                  pl.BlockSpec((1, 1, qw), per3), pl.BlockSpec(tbl.shape, c3), pl.BlockSpec(gexp.shape, c3)],
        out_specs=pl.BlockSpec((1, 1, qw), per3),
        scratch_shapes=[pltpu.VMEM((2, 2, N_KV_HEADS, NSA_HEAD_DIM, nblk * page), F32),
                        pltpu.SemaphoreType.DMA((2,))])
    return pl.pallas_call(
        _sample_sel_kernel, grid_spec=grid_spec, out_shape=jax.ShapeDtypeStruct((nseq, 1, qw), F32),
        compiler_params=_cparams(("arbitrary",)), name="sample_sel",
    )(pages, c31, f0, cache_t, q, brg, snew, code, ocw, tbl, gexp)


def _sample_merge_kernel(yt_ref, xs_ref, z_ref, sgs_ref, dsk_ref, nw_ref, onesg_ref, ws_ref, ynsa_ref, sgn_ref, x_ref,
                         wn_ref, wo_ref, h_ref):
    gw = onesg_ref.shape[0]
    y = yt_ref[...].T + dsk_ref[...] * xs_ref[...]
    zz = z_ref[...]
    y = y * (zz * _sigmoid(zz))
    ms = _seg_sum(y * y, onesg_ref[...]) * (1.0 / gw)
    y = (y * lax.rsqrt(ms + RMS_EPS)) * nw_ref[...]
    u = sgs_ref[...] * _bdot(y, ws_ref[...]) + sgn_ref[...] * _bdot(ynsa_ref[...], wn_ref[...])
    h_ref[...] = x_ref[...] + _bdot(u, wo_ref[...])


def _sample_merge(y_t, xs, z, sgs, dsk, nw, onesg, ws, ynsa, sgn, x, wn, wo):
    return pl.pallas_call(_sample_merge_kernel, out_shape=jax.ShapeDtypeStruct(x.shape, F32),
                          compiler_params=pltpu.CompilerParams(vmem_limit_bytes=VMEM_LIMIT), name="sample_merge",
                          )(y_t, xs, z, sgs, dsk, nw, onesg, ws, ynsa, sgn, x, wn, wo)


def _bucket_lut():
    n = np.arange(MAX_DISTANCE + 1)
    max_exact = N_BUCKETS // 2
    nf = np.maximum(n, 1).astype(np.float32)
    large = max_exact + (np.log(nf / max_exact) / math.log(MAX_DISTANCE / max_exact)
                         * (N_BUCKETS - max_exact)).astype(np.int32)
    return np.where(n < max_exact, n, np.minimum(large, N_BUCKETS - 1))


def _bias_of_dist(rel_bias, dist):
    lut = _bucket_lut()
    idx = lut[np.clip(dist, 0, MAX_DISTANCE)]
    b = jnp.moveaxis(rel_bias.astype(F32)[idx], -1, 0)
    return jnp.where(jnp.asarray(dist >= 0), b, 0.0)


def _block_diag(w, reps):
    n, d, e = w.shape
    eye = jnp.eye(reps, dtype=w.dtype)
    return jnp.einsum("ab,nde->nadbe", eye, w).reshape(n, reps * d, reps * e)


def _ones_blocks(size, seg):
    return jnp.asarray(np.kron(np.eye(size // seg), np.ones((seg, seg))), BF16)


def _pad_cols(w, width):
    return jnp.pad(w, ((0, 0), (0, width - w.shape[1])))


def _prep(p):
    d_model = p["w_in"].shape[1]
    d_inner = p["w_ssm_out"].shape[1]
    n_heads = d_inner // SSM_HEAD_DIM
    conv_dim = p["conv_w"].shape[2]
    qw = N_Q_HEADS * NSA_HEAD_DIM
    kvw = 2 * N_KV_HEADS * NSA_HEAD_DIM
    splits = (d_inner, conv_dim, n_heads, qw, kvw, kvw, kvw, 3 * N_Q_HEADS, d_model, d_model)
    offs = np.concatenate([[0], np.cumsum(splits)])
    w_in = p["w_in"][0]
    seg = lambda k: w_in[:, offs[k]:offs[k + 1]]
    bf = lambda a: a.astype(BF16)
    o = {}
    o["w_ssm_in"] = [bf(seg(0)), bf(seg(1)), bf(_pad_cols(seg(2), LANES))]
    o["w_nsa_in"] = [bf(seg(3)), bf(seg(4)), bf(seg(5)), bf(seg(6)), bf(_pad_cols(seg(7), LANES)), bf(seg(8)),
                     bf(seg(9))]
    o["norm_mix"] = p["norm_mix_w"][0][None, :]
    kv_half = kvw // 2
    head_w = jnp.zeros((SUBLANES, qw), F32)
    head_w = head_w.at[0].set(jnp.tile(p["q_norm_w"][0], N_Q_HEADS))
    head_w = head_w.at[1, :kv_half].set(jnp.tile(p["k_sel_norm_w"][0], N_KV_HEADS))
    head_w = head_w.at[2, :kv_half].set(jnp.tile(p["k_win_norm_w"][0], N_KV_HEADS))
    o["head_w"] = head_w
    o["ones64"] = _ones_blocks(kv_half, NSA_HEAD_DIM)
    o["conv_w"] = p["conv_w"][0]
    o["conv_b"] = p["conv_b"][0][None, :]
    headp = jnp.zeros((SUBLANES, LANES), F32)
    headp = headp.at[0, :n_heads].set(p["dt_bias"][0]).at[1, :n_heads].set(p["a_log"][0])
    o["headp"] = headp
    o["dsk"] = jnp.repeat(p["d_skip"][0], SSM_HEAD_DIM)[None, :]
    o["ssm_nw"] = p["ssm_norm_w"][0][None, :]
    expand = np.zeros((LANES, d_inner), np.float32)
    for h in range(n_heads):
        expand[h, h * SSM_HEAD_DIM:(h + 1) * SSM_HEAD_DIM] = 1.0
    o["expand"] = jnp.asarray(expand, BF16)
    o["tri"] = jnp.asarray(np.tril(np.ones((SSD_CHUNK, SSD_CHUNK), np.float32)), BF16)
    o["onesg"] = jnp.ones((d_inner // N_SSM_GROUPS,) * 2, BF16)
    o["w_ssm_out"] = bf(p["w_ssm_out"][0])
    o["cmp_wk"] = bf(_block_diag(p["cmp_w_k"][0], N_KV_HEADS))
    o["cmp_wv"] = bf(_block_diag(p["cmp_w_v"][0], N_KV_HEADS))
    o["cmp_pek"] = jnp.tile(p["cmp_pe_k"][0], (1, N_KV_HEADS))
    o["cmp_pev"] = jnp.tile(p["cmp_pe_v"][0], (1, N_KV_HEADS))
    o["kc_nw"] = jnp.tile(p["k_cmp_norm_w"][0], N_KV_HEADS)[None, :]
    rel = p["rel_bias"]
    band = 2 * CMP_STRIDE
    i = np.arange(QT)[:, None]
    c = np.arange(band)[None, :]
    d_band = i + CMP_STRIDE * (band // 2) - CMP_STRIDE * c - (CMP_BLOCK - 1)
    d_band = np.concatenate([d_band, np.full((QT, 1), MAX_DISTANCE)], axis=1)
    tb = _bias_of_dist(rel, d_band)
    o["cmp_tb"] = jnp.pad(tb, ((0, 0), (0, 0), (0, 2 * band - tb.shape[2])))
    jk = np.arange(QT)[:, None]
    iq = np.arange(QT)[None, :]
    o["b0"] = _bias_of_dist(rel, iq - jk)
    o["b1"] = _bias_of_dist(rel, QT + iq - jk)
    o["c31"] = rel[N_BUCKETS - 1].astype(F32)
    gexp = np.zeros((3, LANES, qw), np.float32)
    for h in range(N_Q_HEADS):
        for k in range(3):
            gexp[k, 3 * h + k, h * NSA_HEAD_DIM:(h + 1) * NSA_HEAD_DIM] = 1.0
    o["gexp"] = jnp.asarray(gexp, BF16)
    o["w_nsa_out"] = bf(p["w_nsa_out"][0])
    o["w_out"] = bf(p["w_out"][0])
    o["norm_ffn"] = p["norm_ffn_w"][0][None, :]
    ne = p["w_router"].shape[2]
    o["w_router"] = _pad_cols(p["w_router"][0], LANES)
    o["b_router"] = jnp.full((1, LANES), NEG, F32).at[0, :ne].set(p["b_router"][0])
    o["w1"] = bf(p["w_gate_up"][0])
    o["b1e"] = p["b_gate_up"][0][:, None, :]
    o["w2"] = bf(p["w_down"][0])
    o["b2e"] = p["b_down"][0][:, None, :]
    return o


def _kv_layouts(kv, batch):
    n = kv.shape[0]
    s = n // batch
    half = kv.shape[1] // 2
    k = kv[:, :half].astype(BF16).reshape(batch, s, N_KV_HEADS, NSA_HEAD_DIM).transpose(0, 2, 1, 3)
    v = kv[:, half:].astype(BF16).reshape(batch, s // QT, QT, N_KV_HEADS, NSA_HEAD_DIM).transpose(0, 3, 1, 4, 2)
    return k, v


def _prompt_mixer(x, o, batch):
    z, xbc, dt = _proj(x, o["norm_mix"], o["ones64"], o["head_w"], o["w_ssm_in"], ["raw", "raw", "raw"], 256)
    q, kvc, kvs, kvw, brg, sg_ssm, sg_nsa = _proj(
        x, o["norm_mix"], o["ones64"], o["head_w"], o["w_nsa_in"], ["q", "raw", "ks", "kw", "sig", "sig", "sig"], 256)
    m_ssm, h_t = _ssd_prompt(xbc, z, dt, sg_ssm, o["conv_w"], o["conv_b"], o["headp"], o["dsk"], o["ssm_nw"],
                             o["expand"], o["tri"], o["onesg"], o["w_ssm_out"], batch)
    kc, vc = _cmp_build(kvc, o["cmp_wk"], o["cmp_wv"], o["cmp_pek"], o["cmp_pev"], o["ones64"], o["kc_nw"], batch)
    ocmp, sel = _cmp_attn(q, brg, kc, vc, o["cmp_tb"], o["gexp"], batch)
    ks, vs_t = _kv_layouts(kvs, batch)
    kw, vw_t = _kv_layouts(kvw, batch)
    h = _nsa_main(o["c31"], q, brg, sel, ocmp, ks, vs_t, kw, vw_t, o["b0"], o["b1"], m_ssm, sg_nsa, x,
                  o["w_nsa_out"], o["w_out"], batch)
    return h, (kvc, kvs, kvw, h_t, xbc)


def _moe(h_all, o):
    n = h_all.shape[0]
    tm = 384
    n_pad = -(-n // tm) * tm
    hp = jnp.pad(h_all, ((0, n_pad - n), (0, 0)))
    hn, gate = _router(hp, o["norm_ffn"], o["w_router"], o["b_router"], tm)
    return _moe_dense(hn, gate, hp, o["w1"], o["b1e"], o["w2"], o["b2e"], tm)[:n]


def kernel(x_prompt, x_sample, cache_cmp, cache_sel, cache_win, state_ssm, state_conv, page_table, norm_mix_w, w_in,
           conv_w, conv_b, dt_bias, a_log, d_skip, ssm_norm_w, w_ssm_out, q_norm_w, k_cmp_norm_w, k_sel_norm_w,
           k_win_norm_w, cmp_pe_k, cmp_w_k, cmp_pe_v, cmp_w_v, rel_bias, w_nsa_out, w_out, norm_ffn_w, w_router,
           b_router, w_gate_up, b_gate_up, w_down, b_down):
    params = dict(norm_mix_w=norm_mix_w, w_in=w_in, conv_w=conv_w, conv_b=conv_b, dt_bias=dt_bias, a_log=a_log,
                  d_skip=d_skip, ssm_norm_w=ssm_norm_w, w_ssm_out=w_ssm_out, q_norm_w=q_norm_w,
                  k_cmp_norm_w=k_cmp_norm_w, k_sel_norm_w=k_sel_norm_w, k_win_norm_w=k_win_norm_w,
                  cmp_pe_k=cmp_pe_k, cmp_w_k=cmp_w_k, cmp_pe_v=cmp_pe_v, cmp_w_v=cmp_w_v, rel_bias=rel_bias,
                  w_nsa_out=w_nsa_out, w_out=w_out, norm_ffn_w=norm_ffn_w, w_router=w_router, b_router=b_router,
                  w_gate_up=w_gate_up, b_gate_up=b_gate_up, w_down=w_down, b_down=b_down)
    o = _prep(params)
    bsz, s, d = x_prompt.shape
    db, t, _ = x_sample.shape
    kvshape = (2, N_KV_HEADS, NSA_HEAD_DIM)
    hp, (kvc, kvs, kvw, h_t, xbc) = _prompt_mixer(x_prompt.reshape(bsz * s, d), o, bsz)
    assert t == 1, "the sample group decodes one token per sequence"
    hs, (kvc_s, kvs_s, win_s, h_s, conv_s) = _sample_mixer(
        x_sample.reshape(db, d), o, cache_cmp[0], cache_sel[0], cache_win[0], state_ssm[0], state_conv[0],
        page_table, rel_bias)
    y_all = _moe(jnp.concatenate([hp, hs], axis=0), o)
    wlen = min(WINDOW, s)
    n_heads = h_t.shape[2] // SSM_HEAD_DIM
    outs_p = (kvc.reshape((1, bsz, s) + kvshape), kvs.reshape((1, bsz, s) + kvshape),
              kvw.reshape((bsz, s) + kvshape)[None, :, s - wlen:],
              h_t.reshape(bsz, D_STATE, n_heads, SSM_HEAD_DIM).transpose(0, 2, 3, 1)[None],
              xbc.reshape(bsz, s, -1)[None, :, s - (D_CONV - 1):])
    outs_s = (kvc_s.reshape((1, db, t) + kvshape), kvs_s.reshape((1, db, t) + kvshape),
              win_s.reshape((1, db, win_s.shape[1]) + kvshape), h_s.reshape((1, db, n_heads, SSM_HEAD_DIM, D_STATE)),
              conv_s.reshape(1, db, D_CONV - 1, -1))
    return (y_all[:bsz * s].reshape(bsz, s, d), y_all[bsz * s:].reshape(db, t, d)) + outs_p + outs_s


def _sample_mixer(x, o, cache_cmp, cache_sel, cache_win, state_ssm, state_conv, page_table, rel):
    n = x.shape[0]
    z, xbc, dt = _proj(x, o["norm_mix"], o["ones64"], o["head_w"], o["w_ssm_in"], ["raw", "raw", "raw"], n)
    q, kvc, kvs, kvw, brg, sg_ssm, sg_nsa = _proj(
        x, o["norm_mix"], o["ones64"], o["head_w"], o["w_nsa_in"], ["q", "raw", "ks", "kw", "sig", "sig", "sig"], n)
    d_inner = z.shape[1]
    n_heads = d_inner // SSM_HEAD_DIM
    conv_new, xs, bm, ct, dtx_t, dec = _mamba_prep(xbc, state_conv.reshape(n, -1), dt, o["conv_w"], o["conv_b"],
                                                   o["headp"], o["expand"])
    h_new, y_t = _mamba_state(dec[:, :n_heads], state_ssm.reshape(n, d_inner, D_STATE), dtx_t, bm, ct)
    pool, page = cache_cmp.shape[0], cache_cmp.shape[1]
    n_pages = page_table.shape[1]
    past = n_pages * page
    tq = past
    assert page >= MAX_DISTANCE and page % SEL_BLOCK == 0 and past % CMP_STRIDE == 0
    half = kvc.shape[1] // 2
    nch = half // LANES
    cmp_t = jnp.transpose(cache_cmp, (0, 2, 3, 4, 1)).reshape(pool, 2 * nch, LANES, page)
    sel_t = jnp.transpose(cache_sel, (0, 2, 3, 4, 1))
    nstr = -(-(past // CMP_STRIDE + 1) // SUBLANES) * SUBLANES
    w_kv = jnp.stack([o["cmp_wk"], o["cmp_wv"]])
    pe_kv = jnp.stack([o["cmp_pek"], o["cmp_pev"]])
    tok = _page_cmp(page_table, cmp_t, kvc[:, None, :], w_kv, pe_kv, o["ones64"], o["kc_nw"], nstr)
    pad_heads = lambda b: jnp.pad(b.reshape(N_KV_HEADS, Q_PER_KV, -1), ((0, 0), (0, SUBLANES - Q_PER_KV), (0, 0)))
    tbc = pad_heads(_bias_of_dist(rel, tq - (CMP_STRIDE * np.arange(nstr) + CMP_BLOCK - 1)))
    wlen = cache_win.shape[1]
    wrows = -(-(wlen + 1) // SUBLANES) * SUBLANES
    past_w = past - wlen
    tbw = pad_heads(_bias_of_dist(rel, tq - (past_w + np.arange(wrows))))
    n_past_sel = past // SEL_BLOCK
    nsel = n_past_sel + 1
    nsel_pad = -(-nsel // LANES) * LANES
    o_cw, selmask, win_new = _sample_cw(q[:, None, :], brg[:, None, :], tok, cache_win.reshape(n, wlen, -1),
                                        kvw[:, None, :], tbc, tbw, o["gexp"], tq, past_w, nsel_pad)
    nblk = min(N_SEL_BLOCKS, nsel)
    _, idx = lax.top_k(selmask[:, :N_KV_HEADS, :nsel], nblk)
    is_past = idx < n_past_sel
    jp = jnp.minimum(idx, n_past_sel - 1)
    per_page = page // SEL_BLOCK
    pg = jp // per_page
    phys = jnp.take_along_axis(page_table, pg.reshape(n, -1), axis=1).astype(jnp.int32)
    lane_blk = (np.arange(page) // SEL_BLOCK)[None, None, None, :]
    attended = is_past[..., None] & (lane_blk == (jp % per_page)[..., None])
    code = jnp.where(attended, jnp.where((pg == n_pages - 1)[..., None], 2.0, 1.0), 0.0).reshape(n, N_KV_HEADS, -1)
    new_on = jnp.any(idx >= n_past_sel, axis=-1).astype(F32)
    code = jnp.concatenate([code, jnp.broadcast_to(new_on[..., None], code.shape)], axis=1)
    tbl = pad_heads(_bias_of_dist(rel, tq - ((n_pages - 1) * page + np.arange(page))))
    f0 = rel[_bucket_lut()[0]].astype(F32)
    y_nsa = _sample_sel(phys, o["c31"], f0, sel_t, q[:, None, :], brg[:, None, :], kvs[:, None, :], code, o_cw, tbl,
                        o["gexp"])
    h = _sample_merge(y_t, xs, z, sg_ssm, o["dsk"], o["ssm_nw"], o["onesg"], o["w_ssm_out"], y_nsa[:, 0], sg_nsa, x,
                      o["w_nsa_out"], o["w_out"])
    return h, (kvc, kvs, win_new, h_new, conv_new)
```

```python
import functools
import math

import jax
import jax.numpy as jnp
import numpy as np
from jax import lax
from jax.experimental import pallas as pl
from jax.experimental.pallas import tpu as pltpu

F32 = jnp.float32
BF16 = jnp.bfloat16

SSM_HEAD_DIM = 64
N_SSM_GROUPS = 4
D_STATE = 128
D_CONV = 4
SSD_CHUNK = 128
NSA_HEAD_DIM = 64
N_Q_HEADS = 16
N_KV_HEADS = 4
Q_PER_KV = N_Q_HEADS // N_KV_HEADS
CMP_BLOCK = 32
CMP_STRIDE = 16
SEL_BLOCK = 64
N_SEL_BLOCKS = 16
WINDOW = 512
N_BUCKETS = 32
MAX_DISTANCE = 128
TOP_K = 4
SWIGLU_LIMIT = 7.0
SWIGLU_ALPHA = 1.702
RMS_EPS = 1e-6
LOG2E = math.log2(math.e)
NEG = -1e30
FORCE_SCORE = 1e9

LANES = 128
SUBLANES = 8
QT = 128
VMEM_LIMIT = 56 * 1024 * 1024


def _cparams(sem):
    return pltpu.CompilerParams(dimension_semantics=sem, vmem_limit_bytes=VMEM_LIMIT)


def _bdot(a, b):
    return jnp.dot(a.astype(BF16), b.astype(BF16), preferred_element_type=F32)


def _bdot_nt(a, b):
    return lax.dot_general(a.astype(BF16), b.astype(BF16), (((1,), (1,)), ((), ())),
                           preferred_element_type=F32)


def _split3(a):
    hi = a.astype(BF16)
    r = a - hi.astype(F32)
    mid = r.astype(BF16)
    lo = (r - mid.astype(F32)).astype(BF16)
    return hi, mid, lo


def _dot3(a, b):
    hi, mid, lo = _split3(a)
    d = lambda p: jnp.dot(p, b, preferred_element_type=F32)
    return (d(hi) + d(mid)) + d(lo)


def _dot3_l(a, b):
    hi, mid, lo = _split3(b)
    d = lambda p: jnp.dot(a, p, preferred_element_type=F32)
    return (d(hi) + d(mid)) + d(lo)


def _dot3_nt_l(a, b):
    hi, mid, lo = _split3(b)
    d = lambda p: lax.dot_general(a, p, (((1,), (1,)), ((), ())), preferred_element_type=F32)
    return (d(hi) + d(mid)) + d(lo)


def _seg_sum(y, ones_blk):
    c = ones_blk.shape[0]
    outs = []
    for k in range(y.shape[1] // c):
        outs.append(_dot3(y[:, k * c:(k + 1) * c], ones_blk))
    return outs[0] if len(outs) == 1 else jnp.concatenate(outs, axis=1)


def _sigmoid(x):
    return 1.0 / (1.0 + jnp.exp(-x))


def _rms_rows(x, w):
    ms = jnp.mean(x * x, axis=-1, keepdims=True)
    return (x * lax.rsqrt(ms + RMS_EPS)) * w


def _head_rms(y, ones64, w, seg):
    ms = _seg_sum(y * y, ones64) * (1.0 / seg)
    return (y * lax.rsqrt(ms + RMS_EPS)) * w


def _proj_kernel(kinds, x_ref, nw_ref, ones_ref, hw_ref, *refs):
    n = len(kinds)
    w_refs, o_refs = refs[:n], refs[n:]
    xn = _rms_rows(x_ref[...], nw_ref[...]).astype(BF16)
    ones64 = ones_ref[...]
    for kind, w_ref, o_ref in zip(kinds, w_refs, o_refs):
        y = jnp.dot(xn, w_ref[...], preferred_element_type=F32)
        if kind == "sig":
            y = _sigmoid(y)
        elif kind == "q":
            y = _head_rms(y, ones64, hw_ref[0:1, :], NSA_HEAD_DIM) * (NSA_HEAD_DIM ** -0.5 * LOG2E)
        elif kind in ("ks", "kw"):
            row = 1 if kind == "ks" else 2
            half = y.shape[1] // 2
            k = _head_rms(y[:, :half], ones64, hw_ref[row:row + 1, :half], NSA_HEAD_DIM)
            y = jnp.concatenate([k, y[:, half:]], axis=1)
        o_ref[...] = y


def _proj(x, norm_w, ones64, head_w, weights, kinds, tm):
    n, d = x.shape
    assert n % tm == 0
    const = lambda i: (0, 0)
    in_specs = [pl.BlockSpec((tm, d), lambda i: (i, 0)),
                pl.BlockSpec((1, d), const),
                pl.BlockSpec(ones64.shape, const),
                pl.BlockSpec(head_w.shape, const)]
    in_specs += [pl.BlockSpec(w.shape, const) for w in weights]
    out_specs = [pl.BlockSpec((tm, w.shape[1]), lambda i: (i, 0)) for w in weights]
    out_shape = [jax.ShapeDtypeStruct((n, w.shape[1]), F32) for w in weights]
    return pl.pallas_call(
        functools.partial(_proj_kernel, tuple(kinds)),
        grid=(n // tm,), in_specs=in_specs, out_specs=out_specs, out_shape=out_shape,
        compiler_params=_cparams(("parallel",)), name="in_proj",
    )(x, norm_w, ones64, head_w, *weights)


def _softplus(x):
    return jnp.maximum(x, 0.0) + jnp.log1p(jnp.exp(-jnp.abs(x)))


def _ssd_kernel(xbc_ref, z_ref, dt_ref, sg_ref, cw_ref, cb_ref, hp_ref, dsk_ref, nw_ref, e_ref, tri_ref,
                onesg_ref, wout_ref, o_ref, ht_ref, ht_s, ext_s):
    c = pl.program_id(1)
    q = SSD_CHUNK
    d_inner = z_ref.shape[1]
    gw = d_inner // N_SSM_GROUPS
    hpg = gw // SSM_HEAD_DIM
    tail = SUBLANES

    @pl.when(c == 0)
    def _():
        ht_s[...] = jnp.zeros_like(ht_s)
        ext_s[0:tail, :] = jnp.zeros((tail, ext_s.shape[1]), F32)

    xb = xbc_ref[...]
    ext_s[tail:tail + q, :] = xb
    u = cb_ref[...] + cw_ref[D_CONV - 1:D_CONV, :] * xb
    for k in range(1, D_CONV):
        u = u + cw_ref[D_CONV - 1 - k:D_CONV - k, :] * ext_s[tail - k:tail - k + q, :]
    ext_s[0:tail, :] = xb[q - tail:q, :]
    u = u * _sigmoid(u)
    xs = u[:, :d_inner]
    bm = u[:, d_inner:d_inner + N_SSM_GROUPS * D_STATE]
    cm = u[:, d_inner + N_SSM_GROUPS * D_STATE:]

    dtv = _softplus(dt_ref[...] + hp_ref[0:1, :])
    a = dtv * (-jnp.exp(hp_ref[1:2, :]))
    acum = _dot3_l(tri_ref[...], a)
    acum_t = acum.T
    eacum = jnp.exp(acum)
    w_end = jnp.exp(acum[q - 1:q, :] - acum) * dtv
    e = e_ref[...]
    dt_x = _dot3(dtv, e)
    we_x = _dot3(w_end, e)
    ea_x = _dot3(eacum, e)
    xdt = (xs * dt_x).astype(BF16)
    xdtw = (xs * we_x).astype(BF16)
    ii = lax.broadcasted_iota(jnp.int32, (q, q), 0)
    jj = lax.broadcasted_iota(jnp.int32, (q, q), 1)
    causal = jj <= ii

    ys = []
    for g in range(N_SSM_GROUPS):
        cg = cm[:, g * D_STATE:(g + 1) * D_STATE].astype(BF16)
        bg = bm[:, g * D_STATE:(g + 1) * D_STATE]
        cbm = _bdot_nt(cg, bg)
        bg_t = bg.T.astype(BF16)
        sl = slice(g * gw, (g + 1) * gw)
        st = jnp.dot(bg_t, xdtw[:, sl], preferred_element_type=F32)
        hg = ht_s[:, sl]
        y_g = jnp.dot(cg, hg.astype(BF16), preferred_element_type=F32) * ea_x[:, sl]
        yh = []
        for r in range(hpg):
            h = g * hpg + r
            seg = acum[:, h:h + 1] - acum_t[h:h + 1, :]
            dec = jnp.exp(jnp.where(causal, seg, NEG))
            m = (cbm * dec).astype(BF16)
            yh.append(jnp.dot(m, xdt[:, h * SSM_HEAD_DIM:(h + 1) * SSM_HEAD_DIM], preferred_element_type=F32))
        ys.append(y_g + jnp.concatenate(yh, axis=1))
        ht_s[:, sl] = hg * ea_x[q - 1:q, sl] + st
    y = jnp.concatenate(ys, axis=1) + dsk_ref[...] * xs
    zz = z_ref[...]
    y = y * (zz * _sigmoid(zz))
    ms = _seg_sum(y * y, onesg_ref[...]) * (1.0 / gw)
    y = (y * lax.rsqrt(ms + RMS_EPS)) * nw_ref[...]
    o_ref[...] = sg_ref[...] * _bdot(y, wout_ref[...])

    @pl.when(c == pl.num_programs(1) - 1)
    def _():
        ht_ref[0] = ht_s[...]


def _ssd_prompt(xbc, z, dt, sg, conv_w, conv_b, headp, dsk, nw, expand, tri, onesg, wout, batch):
    n, conv_dim = xbc.shape
    d_inner = z.shape[1]
    d_model = wout.shape[1]
    q = SSD_CHUNK
    nc = n // batch // q
    const = lambda b, c: (0, 0)
    rows = lambda b, c: (b * nc + c, 0)
    in_specs = [pl.BlockSpec((q, conv_dim), rows), pl.BlockSpec((q, d_inner), rows),
                pl.BlockSpec((q, LANES), rows), pl.BlockSpec((q, d_model), rows),
                pl.BlockSpec(conv_w.shape, const), pl.BlockSpec(conv_b.shape, const),
                pl.BlockSpec(headp.shape, const), pl.BlockSpec(dsk.shape, const),
                pl.BlockSpec(nw.shape, const), pl.BlockSpec(expand.shape, const),
                pl.BlockSpec(tri.shape, const), pl.BlockSpec(onesg.shape, const),
                pl.BlockSpec(wout.shape, const)]
    out_specs = [pl.BlockSpec((q, d_model), rows),
                 pl.BlockSpec((1, D_STATE, d_inner), lambda b, c: (b, 0, 0))]
    out_shape = [jax.ShapeDtypeStruct((n, d_model), F32),
                 jax.ShapeDtypeStruct((batch, D_STATE, d_inner), F32)]
    return pl.pallas_call(
        _ssd_kernel, grid=(batch, nc), in_specs=in_specs, out_specs=out_specs, out_shape=out_shape,
        scratch_shapes=[pltpu.VMEM((D_STATE, d_inner), F32), pltpu.VMEM((SUBLANES + q, conv_dim), F32)],
        compiler_params=_cparams(("parallel", "arbitrary")), name="ssd_prompt",
    )(xbc, z, dt, sg, conv_w, conv_b, headp, dsk, nw, expand, tri, onesg, wout)


def _cmp_build_kernel(kvc_ref, wk_ref, wv_ref, pek_ref, pev_ref, ones_ref, nw_ref, kc_ref, vc_ref):
    ns = kc_ref.shape[0]
    kw = kc_ref.shape[1]
    lo_k = hi_k = lo_v = hi_v = None
    add = lambda acc, v: v if acc is None else acc + v
    nch = 2 * kw // LANES
    for i in range(CMP_STRIDE):
        x = jnp.concatenate([kvc_ref[pl.ds(nch * i + c, ns, stride=nch * CMP_STRIDE), :] for c in range(nch)], axis=1)
        xk, xv = x[:, :kw], x[:, kw:]
        j = CMP_STRIDE + i
        lo_k = add(lo_k, _bdot(xk + pek_ref[i:i + 1, :], wk_ref[i]))
        hi_k = add(hi_k, _bdot(xk + pek_ref[j:j + 1, :], wk_ref[j]))
        lo_v = add(lo_v, _bdot(xv + pev_ref[i:i + 1, :], wv_ref[i]))
        hi_v = add(hi_v, _bdot(xv + pev_ref[j:j + 1, :], wv_ref[j]))
    kc = lo_k + pltpu.roll(hi_k, ns - 1, axis=0)
    vc_ref[...] = lo_v + pltpu.roll(hi_v, ns - 1, axis=0)
    kc_ref[...] = _head_rms(kc, ones_ref[...], nw_ref[...], NSA_HEAD_DIM)


def _cmp_build(kvc, wk, wv, pek, pev, ones64, nw, batch):
    n, w = kvc.shape
    s = n // batch
    ns = s // CMP_STRIDE
    kw = w // 2
    c2 = lambda b: (0, 0)
    c3 = lambda b: (0, 0, 0)
    nch = w // LANES
    kvc = kvc.reshape(n * nch, LANES)
    return pl.pallas_call(
        _cmp_build_kernel, grid=(batch,),
        in_specs=[pl.BlockSpec((s * nch, LANES), lambda b: (b, 0)), pl.BlockSpec(wk.shape, c3),
                  pl.BlockSpec(wv.shape, c3),
                  pl.BlockSpec(pek.shape, c2), pl.BlockSpec(pev.shape, c2), pl.BlockSpec(ones64.shape, c2),
                  pl.BlockSpec(nw.shape, c2)],
        out_specs=[pl.BlockSpec((ns, kw), lambda b: (b, 0)), pl.BlockSpec((ns, kw), lambda b: (b, 0))],
        out_shape=[jax.ShapeDtypeStruct((batch * ns, kw), F32)] * 2,
        compiler_params=_cparams(("parallel",)), name="cmp_build",
    )(kvc, wk, wv, pek, pev, ones64, nw)


def _rank_select(v, n_keep):
    n = v.shape[0]
    jidx = lax.broadcasted_iota(jnp.int32, v.shape, 0)
    cnt = jnp.zeros(v.shape, F32)
    for i in range(n):
        ri = v[i:i + 1, :]
        cnt = cnt + jnp.where(jidx > i, jnp.where(ri >= v, 1.0, 0.0), jnp.where(ri > v, 1.0, 0.0))
    return jnp.where(cnt < n_keep, 1.0, 0.0)


def _cmp_attn_kernel(q_ref, brg_ref, kc_ref, vc_ref, tb_ref, gexp_ref, ocmp_ref, sel_ref):
    qi = pl.program_id(1)
    t0 = qi * QT
    ns = kc_ref.shape[0]
    nsel = sel_ref.shape[2]
    nb = tb_ref.shape[2]
    band = 2 * CMP_STRIDE
    tt = t0 + lax.broadcasted_iota(jnp.int32, (QT, ns), 0)
    nn = lax.broadcasted_iota(jnp.int32, (QT, ns), 1)
    mask = (CMP_STRIDE * nn + (CMP_BLOCK - 1)) <= tt
    first = (QT // CMP_STRIDE) * qi - band // 2
    cc = lax.broadcasted_iota(jnp.int32, (nb, ns), 0)
    n2 = lax.broadcasted_iota(jnp.int32, (nb, ns), 1)
    shift = jnp.where(((cc < band) & (n2 == first + cc)) | ((cc == band) & (n2 < first)), 1.0, 0.0).astype(BF16)
    jb = lax.broadcasted_iota(jnp.int32, (nsel, ns), 0) * SEL_BLOCK
    cs = lax.broadcasted_iota(jnp.int32, (nsel, ns), 1) * CMP_STRIDE
    cover_t = jnp.where((cs < jb + SEL_BLOCK) & (cs + CMP_BLOCK > jb), 1.0, 0.0).astype(BF16)
    jidx = lax.broadcasted_iota(jnp.int32, (nsel, QT), 0)
    tq = t0 + lax.broadcasted_iota(jnp.int32, (nsel, QT), 1)
    valid = jidx * SEL_BLOCK <= tq
    cur = tq // SEL_BLOCK
    forced = valid & ((jidx == 0) | (jidx == cur) | (jidx == cur - 1))

    q = q_ref[...]
    outs = []
    for g in range(N_KV_HEADS):
        kg = kc_ref[:, g * NSA_HEAD_DIM:(g + 1) * NSA_HEAD_DIM].astype(BF16)
        vg = vc_ref[:, g * NSA_HEAD_DIM:(g + 1) * NSA_HEAD_DIM].astype(BF16)
        psum = jnp.zeros((QT, ns), F32)
        for r in range(Q_PER_KV):
            h = g * Q_PER_KV + r
            s = _bdot_nt(q[:, h * NSA_HEAD_DIM:(h + 1) * NSA_HEAD_DIM], kg)
            s = jnp.where(mask, s + _dot3(tb_ref[h], shift), NEG)
            m = jnp.max(s, axis=-1, keepdims=True)
            p = jnp.where(mask, jnp.exp2(s - m), 0.0)
            l = jnp.sum(p, axis=-1, keepdims=True)
            p = p * jnp.where(l > 0.0, 1.0 / l, 0.0)
            outs.append(_bdot(p, vg))
            psum = psum + p
        imp_t = _dot3_nt_l(cover_t, psum)
        v = jnp.where(forced, FORCE_SCORE, jnp.where(valid, imp_t, NEG))
        sel_ref[0, g] = _rank_select(v, N_SEL_BLOCKS)
    ocmp_ref[...] = jnp.concatenate(outs, axis=1) * _dot3(brg_ref[...], gexp_ref[0])


def _cmp_attn(q, brg, kc, vc, tb, gexp, batch):
    n, qw = q.shape
    s = n // batch
    nq = s // QT
    ns, kw = kc.shape[0] // batch, kc.shape[1]
    nsel = s // SEL_BLOCK
    rows = lambda b, i: (b * nq + i, 0)
    per_b = lambda b, i: (b, 0)
    return pl.pallas_call(
        _cmp_attn_kernel, grid=(batch, nq),
        in_specs=[pl.BlockSpec((QT, qw), rows), pl.BlockSpec((QT, LANES), rows),
                  pl.BlockSpec((ns, kw), per_b), pl.BlockSpec((ns, kw), per_b),
                  pl.BlockSpec(tb.shape, lambda b, i: (0, 0, 0)),
                  pl.BlockSpec((1,) + gexp.shape[1:], lambda b, i: (0, 0, 0))],
        out_specs=[pl.BlockSpec((QT, qw), rows), pl.BlockSpec((1, N_KV_HEADS, nsel, QT), lambda b, i: (b, 0, 0, i))],
        out_shape=[jax.ShapeDtypeStruct((n, qw), F32), jax.ShapeDtypeStruct((batch, N_KV_HEADS, nsel, s), F32)],
        compiler_params=_cparams(("parallel", "parallel")), name="cmp_attn",
    )(q, brg, kc, vc, tb, gexp)


def _attn_round(states, raw, keeps, biases, vts, shifts):
    stats = []
    for st, s, keep, bias, shift in zip(states, raw, keeps, biases, shifts):
        s = _masked_scores(s, keep, bias)
        tile_max = jnp.max(s, axis=0, keepdims=True)
        if shift is not None:
            tile_max = tile_max + shift
        m_new = tile_max if st is None else jnp.maximum(st[0], tile_max)
        alpha = None if st is None else jnp.exp2(st[0] - m_new)
        p = jnp.exp2(s - (m_new if shift is None else m_new - shift)).astype(BF16)
        stats.append((m_new, alpha, p))
    out = []
    for st, (m_new, alpha, p), vt in zip(states, stats, vts):
        pv = jnp.dot(vt, p, preferred_element_type=F32)
        out.append((m_new, pv if st is None else alpha * st[1] + pv))
    return out


def _masked_scores(s, keep, bias=None):
    parts = []
    for r in range(s.shape[1] // QT):
        v = s[:, r * QT:(r + 1) * QT]
        parts.append(jnp.where(keep, v if bias is None else v + bias[r], NEG))
    return jnp.concatenate(parts, axis=1)


def _nsa_main_kernel(c31_ref, q_ref, brg_ref, sel_ref, ocmp_ref, ks_ref, vs_ref, kw_ref, vw_ref, b0_ref, b1_ref,
                     mssm_ref, sgn_ref, x_ref, wn_ref, wo_ref, h_ref):
    qi = pl.program_id(1)
    nwt = WINDOW // QT
    blk_per_tile = QT // SEL_BLOCK
    jk = lax.broadcasted_iota(jnp.int32, (QT, QT), 0)
    iq = lax.broadcasted_iota(jnp.int32, (QT, QT), 1)
    causal_t = jk <= iq
    q_t = q_ref[...].T.astype(BF16)
    gates_t = brg_ref[...].T

    def sel_keep(g, kt, tiles=1):
        rows = [jnp.broadcast_to(sel_ref[0, g, pl.ds(blk_per_tile * kt + b, 1), :], (SEL_BLOCK, QT))
                for b in range(blk_per_tile * tiles)]
        return jnp.concatenate(rows, axis=0) > 0.5

    def key_tile(ref, g, kt):
        return ref[0, g, pl.ds(pl.multiple_of(kt * QT, QT), QT), :]

    groups = range(N_KV_HEADS)
    heads = [[g * Q_PER_KV + r for r in range(Q_PER_KV)] for g in groups]
    qg = [jnp.concatenate([q_t[h * NSA_HEAD_DIM:(h + 1) * NSA_HEAD_DIM, :] for h in heads[g]], axis=1) for g in groups]
    far = [jnp.concatenate([jnp.full((1, QT), c31_ref[h], F32) for h in heads[g]], axis=1) for g in groups]
    near0 = [[b0_ref[h] for h in heads[g]] for g in groups]
    near1 = [[b1_ref[h] for h in heads[g]] for g in groups]
    scores = lambda ref, g, kt: jnp.dot(key_tile(ref, g, kt), qg[g], preferred_element_type=F32)
    prev = jnp.maximum(qi - 1, 0)
    has_prev = jnp.broadcast_to(qi >= 1, (QT, QT))

    none4 = [None] * N_KV_HEADS
    st = _attn_round(
        none4 + none4,
        [scores(ks_ref, g, qi) for g in groups] + [scores(kw_ref, g, qi) for g in groups],
        [causal_t & sel_keep(g, qi) for g in groups] + [causal_t] * N_KV_HEADS, near0 + near0,
        [vs_ref[0, g, qi] for g in groups] + [vw_ref[0, g, qi] for g in groups], none4 + none4)
    st = _attn_round(
        st, [scores(ks_ref, g, prev) for g in groups] + [scores(kw_ref, g, prev) for g in groups],
        [sel_keep(g, prev) & has_prev for g in groups] + [has_prev] * N_KV_HEADS, near1 + near1,
        [vs_ref[0, g, prev] for g in groups] + [vw_ref[0, g, prev] for g in groups], none4 + none4)
    sel, win = st[:N_KV_HEADS], st[N_KV_HEADS:]
    for back in range(2, nwt + 1):
        kt = jnp.maximum(qi - back, 0)
        keep = jnp.broadcast_to(qi >= back, (QT, QT))
        if back == nwt:
            keep = keep & (jk > iq)
        win = _attn_round(win, [scores(kw_ref, g, kt) for g in groups], [keep] * N_KV_HEADS, none4,
                          [vw_ref[0, g, kt] for g in groups], far)

    def sel_body(kt, state):
        return tuple(_attn_round(list(state), [scores(ks_ref, g, kt) for g in groups],
                                 [sel_keep(g, kt) for g in groups], none4, [vs_ref[0, g, kt] for g in groups], far))

    def sel_body2(j, state):
        kt = 2 * j
        raw = [jnp.dot(ks_ref[0, g, pl.ds(pl.multiple_of(kt * QT, QT), 2 * QT), :], qg[g],
                       preferred_element_type=F32) for g in groups]
        keeps = [sel_keep(g, kt, 2) for g in groups]
        vts = [jnp.concatenate([vs_ref[0, g, kt], vs_ref[0, g, kt + 1]], axis=1) for g in groups]
        return tuple(_attn_round(list(state), raw, keeps, none4, vts, far))

    n_far = jnp.maximum(qi - 1, 0)
    sel = lax.fori_loop(0, n_far // 2, sel_body2, tuple(sel))
    sel = lax.fori_loop(2 * (n_far // 2), n_far, sel_body, sel)

    o_t = []
    dh = NSA_HEAD_DIM
    for g in groups:
        o_s = sel[g][1][0:dh, :] * (1.0 / sel[g][1][dh:dh + 1, :])
        o_w = win[g][1][0:dh, :] * (1.0 / win[g][1][dh:dh + 1, :])
        for r, h in enumerate(heads[g]):
            sl = slice(r * QT, (r + 1) * QT)
            o_t.append(gates_t[3 * h + 1:3 * h + 2, :] * o_s[:, sl] + gates_t[3 * h + 2:3 * h + 3, :] * o_w[:, sl])

    pairs = [jnp.concatenate(o_t[2 * k:2 * k + 2], axis=0).T for k in range(len(o_t) // 2)]
    y_nsa = jnp.concatenate(pairs, axis=1) + ocmp_ref[...]
    u = mssm_ref[...] + sgn_ref[...] * _bdot(y_nsa, wn_ref[...])
    h_ref[...] = x_ref[...] + _bdot(u, wo_ref[...])


def _nsa_main(c31, q, brg, sel, ocmp, ks, vs_t, kw, vw_t, b0, b1, mssm, sgn, x, wn, wo, batch):
    n, qw = q.shape
    d = x.shape[1]
    s = n // batch
    nq = s // QT
    rows = lambda b, i: (b * nq + i, 0)
    kspec = pl.BlockSpec((1,) + ks.shape[1:], lambda b, i: (b, 0, 0, 0))
    vspec = pl.BlockSpec((1,) + vs_t.shape[1:], lambda b, i: (b, 0, 0, 0, 0))
    c2 = lambda b, i: (0, 0)
    c3 = lambda b, i: (0, 0, 0)
    return pl.pallas_call(
        _nsa_main_kernel, grid=(batch, nq),
        in_specs=[pl.BlockSpec(memory_space=pltpu.SMEM),
                  pl.BlockSpec((QT, qw), rows), pl.BlockSpec((QT, LANES), rows),
                  pl.BlockSpec((1,) + sel.shape[1:3] + (QT,), lambda b, i: (b, 0, 0, i)),
                  pl.BlockSpec((QT, qw), rows), kspec, vspec, kspec, vspec,
                  pl.BlockSpec(b0.shape, c3), pl.BlockSpec(b1.shape, c3),
                  pl.BlockSpec((QT, d), rows), pl.BlockSpec((QT, d), rows), pl.BlockSpec((QT, d), rows),
                  pl.BlockSpec(wn.shape, c2), pl.BlockSpec(wo.shape, c2)],
        out_specs=pl.BlockSpec((QT, d), rows), out_shape=jax.ShapeDtypeStruct((n, d), F32),
        compiler_params=_cparams(("parallel", "parallel")), name="nsa_main",
    )(c31, q, brg, sel, ocmp, ks, vs_t, kw, vw_t, b0, b1, mssm, sgn, x, wn, wo)


META_EXPERT, META_RANK, META_GATE = 0, TOP_K, 2 * TOP_K


def _router_kernel(h_ref, nw_ref, wr_ref, br_ref, tri_ref, hn_ref, meta_ref, cnt_ref, base_s, *, n_real):
    i = pl.program_id(0)
    tm = h_ref.shape[0]

    @pl.when(i == 0)
    def _():
        base_s[...] = jnp.zeros_like(base_s)

    hn = _rms_rows(h_ref[...], nw_ref[...])
    hn_ref[...] = hn.astype(BF16)
    a_hi = hn.astype(BF16)
    a_lo = (hn - a_hi.astype(F32)).astype(BF16)
    w = wr_ref[...]
    w_hi = w.astype(BF16)
    w_lo = (w - w_hi.astype(F32)).astype(BF16)
    d = lambda a, b: jnp.dot(a, b, preferred_element_type=F32)
    v = (d(a_hi, w_hi) + (d(a_hi, w_lo) + d(a_lo, w_hi))) + br_ref[...]
    lane = lax.broadcasted_iota(jnp.int32, v.shape, 1)
    tops, idxs, hots = [], [], []
    for _ in range(TOP_K):
        m = jnp.max(v, axis=-1, keepdims=True)
        idx = jnp.min(jnp.where(v == m, lane, LANES), axis=-1, keepdims=True)
        hot = lane == idx
        tops.append(m)
        idxs.append(idx)
        hots.append(hot)
        v = jnp.where(hot, NEG, v)
    es = [jnp.exp(t - tops[0]) for t in tops]
    inv = 1.0 / functools.reduce(lambda a, b: a + b, es)
    row = i * tm + lax.broadcasted_iota(jnp.int32, v.shape, 0)
    onehot = jnp.zeros(v.shape, F32)
    for hot in hots:
        onehot = onehot + jnp.where(hot & (row < n_real), 1.0, 0.0)
    before = jnp.dot(tri_ref[...], onehot.astype(BF16), preferred_element_type=F32) + base_s[0:1, :]
    meta = jnp.zeros(v.shape, F32)
    for k in range(TOP_K):
        rank = jnp.sum(jnp.where(hots[k], before, 0.0), axis=-1, keepdims=True)
        meta = meta + jnp.where(lane == META_EXPERT + k, idxs[k].astype(F32), 0.0)
        meta = meta + jnp.where(lane == META_RANK + k, rank, 0.0)
        meta = meta + jnp.where(lane == META_GATE + k, es[k] * inv, 0.0)
    meta_ref[...] = meta
    base_s[0:1, :] = base_s[0:1, :] + jnp.sum(onehot, axis=0, keepdims=True)
    cnt_ref[...] = jnp.broadcast_to(base_s[0:1, :], cnt_ref.shape)


def _router(h, nw, wr, br, tm, n_real):
    n, d = h.shape
    c2 = lambda i: (0, 0)
    tri = jnp.asarray(np.tril(np.ones((tm, tm), np.float32), -1), BF16)
    return pl.pallas_call(
        functools.partial(_router_kernel, n_real=n_real), grid=(n // tm,),
        in_specs=[pl.BlockSpec((tm, d), lambda i: (i, 0)), pl.BlockSpec(nw.shape, c2),
                  pl.BlockSpec(wr.shape, c2), pl.BlockSpec(br.shape, c2), pl.BlockSpec(tri.shape, c2)],
        out_specs=[pl.BlockSpec((tm, d), lambda i: (i, 0)), pl.BlockSpec((tm, LANES), lambda i: (i, 0)),
                   pl.BlockSpec((SUBLANES, LANES), c2)],
        out_shape=[jax.ShapeDtypeStruct((n, d), BF16), jax.ShapeDtypeStruct((n, LANES), F32),
                   jax.ShapeDtypeStruct((SUBLANES, LANES), F32)],
        scratch_shapes=[pltpu.VMEM((SUBLANES, LANES), F32)],
        compiler_params=_cparams(("arbitrary",)), name="moe_router",
    )(h, nw, wr, br, tri)


def _moe_group_kernel(sp_ref, x_ref, w1_ref, b1_ref, w2_ref, b2_ref, y_ref):
    t = pl.program_id(0)
    n_active = sp_ref[pl.num_programs(0)]

    @pl.when(t < n_active)
    def _():
        d_ff = w2_ref.shape[1]
        y1 = jnp.dot(x_ref[...], w1_ref[0], preferred_element_type=F32) + b1_ref[0]
        gt = jnp.minimum(y1[:, :d_ff], SWIGLU_LIMIT)
        up = jnp.clip(y1[:, d_ff:], -SWIGLU_LIMIT, SWIGLU_LIMIT)
        act = (up + 1.0) * gt * _sigmoid(SWIGLU_ALPHA * gt)
        y_ref[...] = _bdot(act, w2_ref[0]) + b2_ref[0]

    @pl.when(t >= n_active)
    def _():
        y_ref[...] = jnp.zeros_like(y_ref)


def _moe_group(sp, x_sorted, w1, b1, w2, b2, tm):
    p, d = x_sorted.shape
    rows = lambda t, sp: (t, 0)
    per_e = lambda t, sp: (sp[t], 0, 0)
    grid_spec = pltpu.PrefetchScalarGridSpec(
        num_scalar_prefetch=1, grid=(p // tm,),
        in_specs=[pl.BlockSpec((tm, d), rows),
                  pl.BlockSpec((1,) + w1.shape[1:], per_e), pl.BlockSpec((1,) + b1.shape[1:], per_e),
                  pl.BlockSpec((1,) + w2.shape[1:], per_e), pl.BlockSpec((1,) + b2.shape[1:], per_e)],
        out_specs=pl.BlockSpec((tm, d), rows))
    return pl.pallas_call(
        _moe_group_kernel, grid_spec=grid_spec, out_shape=jax.ShapeDtypeStruct((p, d), F32),
        compiler_params=_cparams(("arbitrary",)), name="moe_experts",
    )(sp, x_sorted, w1, b1, w2, b2)


def _moe_combine_kernel(h_ref, yk_ref, meta_ref, o_ref):
    d = h_ref.shape[1]
    meta = meta_ref[...]
    kk = lax.broadcasted_iota(jnp.int32, (LANES, LANES), 0)
    acc = h_ref[...]
    for k in range(TOP_K):
        pick = jnp.where(kk == META_GATE + k, 1.0, 0.0).astype(BF16)
        g = _dot3(meta, pick)
        acc = acc + jnp.concatenate([g] * (d // LANES), axis=1) * yk_ref[:, k * d:(k + 1) * d]
    o_ref[...] = acc


def _moe_combine(h, yk, meta, tm):
    n, d = h.shape
    rows = lambda i: (i, 0)
    return pl.pallas_call(
        _moe_combine_kernel, grid=(n // tm,),
        in_specs=[pl.BlockSpec((tm, d), rows), pl.BlockSpec((tm, TOP_K * d), rows), pl.BlockSpec((tm, LANES), rows)],
        out_specs=pl.BlockSpec((tm, d), rows), out_shape=jax.ShapeDtypeStruct((n, d), F32),
        compiler_params=_cparams(("parallel",)), name="moe_combine",
    )(h, yk, meta)


def _mamba_prep_kernel(xbc_ref, sconv_ref, dt_ref, cw_ref, cb_ref, hp_ref, e_ref,
                       conv_ref, xs_ref, bm_ref, ct_ref, dtx_ref, dec_ref):
    cdim = xbc_ref.shape[1]
    d_inner = xs_ref.shape[1]
    xb = xbc_ref[...]
    u = cb_ref[...] + cw_ref[D_CONV - 1:D_CONV, :] * xb
    for k in range(D_CONV - 1):
        u = u + cw_ref[k:k + 1, :] * sconv_ref[:, k * cdim:(k + 1) * cdim]
    conv_ref[:, :(D_CONV - 2) * cdim] = sconv_ref[:, cdim:]
    conv_ref[:, (D_CONV - 2) * cdim:] = xb
    u = u * _sigmoid(u)
    xs = u[:, :d_inner]
    xs_ref[...] = xs
    bm_ref[...] = u[:, d_inner:d_inner + N_SSM_GROUPS * D_STATE]
    cm = u[:, d_inner + N_SSM_GROUPS * D_STATE:]
    for g in range(N_SSM_GROUPS):
        ct_ref[g] = cm[:, g * D_STATE:(g + 1) * D_STATE].T
    dtv = _softplus(dt_ref[...] + hp_ref[0:1, :])
    dec_ref[...] = jnp.exp(dtv * (-jnp.exp(hp_ref[1:2, :])))
    dtx_ref[...] = (xs * _dot3(dtv, e_ref[...])).T


def _mamba_prep(xbc, sconv, dt, conv_w, conv_b, headp, expand):
    n, cdim = xbc.shape
    d_inner = expand.shape[1]
    gn = N_SSM_GROUPS * D_STATE
    out_shape = [jax.ShapeDtypeStruct(sconv.shape, F32), jax.ShapeDtypeStruct((n, d_inner), F32),
                 jax.ShapeDtypeStruct((n, gn), F32), jax.ShapeDtypeStruct((N_SSM_GROUPS, D_STATE, n), F32),
                 jax.ShapeDtypeStruct((d_inner, n), F32), jax.ShapeDtypeStruct((n, LANES), F32)]
    return pl.pallas_call(_mamba_prep_kernel, out_shape=out_shape,
                          compiler_params=pltpu.CompilerParams(vmem_limit_bytes=VMEM_LIMIT), name="mamba_prep",
                          )(xbc, sconv, dt, conv_w, conv_b, headp, expand)


def _bf16x3(a, b):
    a_hi = a.astype(BF16)
    a_lo = (a - a_hi.astype(F32)).astype(BF16)
    b_hi = b.astype(BF16)
    b_lo = (b - b_hi.astype(F32)).astype(BF16)
    d = lambda x, y: jnp.dot(x, y, preferred_element_type=F32)
    return d(a_hi, b_hi) + (d(a_hi, b_lo) + d(a_lo, b_hi))


def _mamba_state_kernel(dec_ref, h0_ref, dtx_ref, bm_ref, ct_ref, hn_ref, yt_ref):
    s = pl.program_id(0)
    n = bm_ref.shape[0]
    d_inner = dtx_ref.shape[0]
    gw = d_inner // N_SSM_GROUPS
    hpg = gw // SSM_HEAD_DIM

    @pl.when(s == 0)
    def _():
        yt_ref[...] = jnp.zeros_like(yt_ref)

    row_is_s = lax.broadcasted_iota(jnp.int32, (n, D_STATE), 0) == s
    col_is_s = lax.broadcasted_iota(jnp.int32, (D_STATE, n), 1) == s
    for g in range(N_SSM_GROUPS):
        b_s = jnp.where(row_is_s, bm_ref[:, g * D_STATE:(g + 1) * D_STATE], 0.0)
        st = _bf16x3(dtx_ref[g * gw:(g + 1) * gw, :], b_s)
        parts = []
        for r in range(hpg):
            h = g * hpg + r
            rows = slice(h * SSM_HEAD_DIM, (h + 1) * SSM_HEAD_DIM)
            parts.append(h0_ref[0, rows, :] * dec_ref[s, h] + st[r * SSM_HEAD_DIM:(r + 1) * SSM_HEAD_DIM, :])
        hn = jnp.concatenate(parts, axis=0)
        hn_ref[0, g * gw:(g + 1) * gw, :] = hn
        c_s = jnp.where(col_is_s, ct_ref[g], 0.0)
        yt_ref[g * gw:(g + 1) * gw, :] += _bdot(hn, c_s)


def _mamba_state(dec, h0, dtx_t, bm, ct):
    n, rows, ns = h0.shape
    c2 = lambda s: (0, 0)
    return pl.pallas_call(
        _mamba_state_kernel, grid=(n,),
        in_specs=[pl.BlockSpec(memory_space=pltpu.SMEM),
                  pl.BlockSpec((1, rows, ns), lambda s: (s, 0, 0)), pl.BlockSpec(dtx_t.shape, c2),
                  pl.BlockSpec(bm.shape, c2), pl.BlockSpec(ct.shape, lambda s: (0, 0, 0))],
        out_specs=[pl.BlockSpec((1, rows, ns), lambda s: (s, 0, 0)), pl.BlockSpec(dtx_t.shape, c2)],
        out_shape=[jax.ShapeDtypeStruct(h0.shape, F32), jax.ShapeDtypeStruct(dtx_t.shape, F32)],
        compiler_params=_cparams(("arbitrary",)), name="mamba_state",
    )(dec, h0, dtx_t, bm, ct)


def _page_cmp_kernel(pt_ref, cache_ref, new_ref, w_ref, pec_ref, deint_ref, ones_ref, nw_ref, out_ref, buf, rows_s, sem):
    t = pl.program_id(0)
    nsteps = pl.num_programs(0)
    n_pages = pt_ref.shape[1]
    page = buf.shape[4]
    nstr = out_ref.shape[2]
    half = out_ref.shape[3]
    nch = half // LANES
    spp = page // CMP_STRIDE
    past_str = n_pages * spp

    def copies(step, slot):
        seq, kv = step // 2, step % 2
        return [pltpu.make_async_copy(cache_ref.at[pt_ref[seq, p], pl.ds(nch * kv, nch)], buf.at[slot, p], sem.at[slot])
                for p in range(n_pages)]

    @pl.when(t == 0)
    def _():
        rows_s[:, past_str:, :] = jnp.zeros((CMP_STRIDE, nstr - past_str, half), F32)
        for cp in copies(t, 0):
            cp.start()

    slot = t % 2

    @pl.when(t + 1 < nsteps)
    def _():
        for cp in copies(t + 1, 1 - slot):
            cp.start()

    for cp in copies(t, slot):
        cp.wait()

    kv = t % 2
    deint = deint_ref[...]

    pages_per_trip = math.gcd(n_pages, 8)

    def to_rows(trip, carry):
        for pp in range(pages_per_trip):
            p = trip * pages_per_trip + pp
            for c in range(nch):
                xr = _bdot_nt(deint, buf[slot, p, c])
                for i in range(CMP_STRIDE):
                    rows_s[i, pl.ds(pl.multiple_of(p * spp, spp), spp), c * LANES:(c + 1) * LANES] = (
                        xr[i * spp:(i + 1) * spp, :])
        return carry

    lax.fori_loop(0, n_pages // pages_per_trip, to_rows, 0)
    new = new_ref[0]
    rows_s[0, past_str:past_str + 1, :] = jnp.where(kv == 0, new[:, :half], new[:, half:])

    lo = hi = None
    for i in range(CMP_STRIDE):
        x = rows_s[i].astype(BF16)
        a = jnp.dot(x, w_ref[0, i], preferred_element_type=F32)
        b = jnp.dot(x, w_ref[0, CMP_STRIDE + i], preferred_element_type=F32)
        lo = a if lo is None else lo + a
        hi = b if hi is None else hi + b
    lo = lo + pec_ref[0, 0:1, :]
    hi = hi + pec_ref[0, 1:2, :]
    tok = lo + pltpu.roll(hi, nstr - 1, axis=0)
    out_ref[0, 0] = jnp.where(kv == 0, _head_rms(tok, ones_ref[...], nw_ref[...], NSA_HEAD_DIM), tok)


def _cmp_const_kernel(pe_ref, w_ref, out_ref):
    for kv in range(2):
        lo = hi = None
        for i in range(CMP_STRIDE):
            j = CMP_STRIDE + i
            a = _dot3(_rows8(pe_ref[kv, i:i + 1, :]), w_ref[kv, i])
            b = _dot3(_rows8(pe_ref[kv, j:j + 1, :]), w_ref[kv, j])
            lo = a if lo is None else lo + a
            hi = b if hi is None else hi + b
        out_ref[kv] = jnp.concatenate([lo[0:1, :], hi[0:1, :], jnp.zeros((SUBLANES - 2, lo.shape[1]), F32)], axis=0)


def _cmp_const(pe_kv, w_kv):
    return pl.pallas_call(_cmp_const_kernel, out_shape=jax.ShapeDtypeStruct((2, SUBLANES, pe_kv.shape[2]), F32),
                          compiler_params=pltpu.CompilerParams(vmem_limit_bytes=VMEM_LIMIT), name="cmp_const",
                          )(pe_kv, w_kv)


def _page_cmp(page_table, cache_t, new_rows, w_kv, pe_const, ones64, nw, nstr):
    nseq, n_pages = page_table.shape
    page = cache_t.shape[3]
    half = w_kv.shape[2]
    nch = half // LANES
    spp = page // CMP_STRIDE
    deint = np.zeros((page, page), np.float32)
    for i in range(CMP_STRIDE):
        for k in range(spp):
            deint[i * spp + k, CMP_STRIDE * k + i] = 1.0
    deint = jnp.asarray(deint, BF16)
    c2 = lambda t, pt: (0, 0)
    grid_spec = pltpu.PrefetchScalarGridSpec(
        num_scalar_prefetch=1, grid=(2 * nseq,),
        in_specs=[pl.BlockSpec(memory_space=pl.ANY),
                  pl.BlockSpec((1, 1, 2 * half), lambda t, pt: (t // 2, 0, 0)),
                  pl.BlockSpec((1,) + w_kv.shape[1:], lambda t, pt: (t % 2, 0, 0, 0)),
                  pl.BlockSpec((1,) + pe_const.shape[1:], lambda t, pt: (t % 2, 0, 0)),
                  pl.BlockSpec(deint.shape, c2), pl.BlockSpec(ones64.shape, c2), pl.BlockSpec(nw.shape, c2)],
        out_specs=pl.BlockSpec((1, 1, nstr, half), lambda t, pt: (t // 2, t % 2, 0, 0)),
        scratch_shapes=[pltpu.VMEM((2, n_pages, nch, LANES, page), F32),
                        pltpu.VMEM((CMP_STRIDE, nstr, half), F32),
                        pltpu.SemaphoreType.DMA((2,))])
    return pl.pallas_call(
        _page_cmp_kernel, grid_spec=grid_spec, out_shape=jax.ShapeDtypeStruct((nseq, 2, nstr, half), F32),
        compiler_params=_cparams(("arbitrary",)), name="page_cmp",
    )(page_table, cache_t, new_rows, w_kv, pe_const, deint, ones64, nw)


def _rows8(x):
    return jnp.broadcast_to(x, (SUBLANES, x.shape[1]))


def _group_q(q_row, g):
    parts = [q_row[:, (g * Q_PER_KV + r) * NSA_HEAD_DIM:(g * Q_PER_KV + r + 1) * NSA_HEAD_DIM]
             for r in range(Q_PER_KV)]
    parts.append(jnp.zeros((SUBLANES - Q_PER_KV, NSA_HEAD_DIM), F32))
    return jnp.concatenate(parts, axis=0)


def _heads_to_row(o):
    return jnp.concatenate([o[r:r + 1, :] for r in range(Q_PER_KV)], axis=1)


def _softmax_rows(s, keep):
    s = jnp.where(keep, s, NEG)
    m = jnp.max(s, axis=-1, keepdims=True)
    p = jnp.where(keep, jnp.exp2(s - m), 0.0)
    l = jnp.sum(p, axis=-1, keepdims=True)
    return p * jnp.where(l > 0.0, 1.0 / l, 0.0)


def _sample_cw_kernel(q_ref, brg_ref, kvc_ref, win_ref, wnew_ref, tbc_ref, tbw_ref, gexp_ref,
                      o_ref, sel_ref, wout_ref, wall_s, *, tq, past_w):
    nstr = kvc_ref.shape[2]
    half = kvc_ref.shape[3]
    wlen = win_ref.shape[1]
    nsel_pad = sel_ref.shape[2]
    wrows = wall_s.shape[0]
    q_row = q_ref[0]
    wall_s[0:wlen, :] = win_ref[0]
    wall_s[wlen:wlen + 1, :] = wnew_ref[0]
    wall_s[wlen + 1:, :] = jnp.zeros((wrows - wlen - 1, wall_s.shape[1]), F32)
    wout_ref[0] = wall_s[1:wlen + 1, :]

    nn = lax.broadcasted_iota(jnp.int32, (SUBLANES, nstr), 1)
    keep_c = (CMP_STRIDE * nn + (CMP_BLOCK - 1)) <= tq
    wi = lax.broadcasted_iota(jnp.int32, (SUBLANES, wrows), 1)
    dw = tq - (past_w + wi)
    keep_w = (dw >= 0) & (dw < WINDOW) & (past_w + wi >= 0) & (wi <= wlen)
    jb = lax.broadcasted_iota(jnp.int32, (nsel_pad, nstr), 0) * SEL_BLOCK
    cs = lax.broadcasted_iota(jnp.int32, (nsel_pad, nstr), 1) * CMP_STRIDE
    cover_t = jnp.where((cs < jb + SEL_BLOCK) & (cs + CMP_BLOCK > jb), 1.0, 0.0).astype(BF16)
    ji = lax.broadcasted_iota(jnp.int32, (nsel_pad, LANES), 0)
    valid = ji * SEL_BLOCK <= tq
    cur = tq // SEL_BLOCK
    forced = valid & ((ji == 0) | (ji == cur) | (ji == cur - 1))
    ii = lax.broadcasted_iota(jnp.int32, (nsel_pad, nsel_pad), 0)
    jj = lax.broadcasted_iota(jnp.int32, (nsel_pad, nsel_pad), 1)

    sel_ref[0] = jnp.zeros(sel_ref.shape[1:], F32)
    oc, ow = [], []
    for g in range(N_KV_HEADS):
        qg = _group_q(q_row, g)
        lanes = slice(g * NSA_HEAD_DIM, (g + 1) * NSA_HEAD_DIM)
        vl = slice(half + g * NSA_HEAD_DIM, half + (g + 1) * NSA_HEAD_DIM)
        p = _softmax_rows(_bdot_nt(qg, kvc_ref[0, 0, :, lanes]) + tbc_ref[g], keep_c)
        oc.append(_heads_to_row(_bdot(p, kvc_ref[0, 1, :, lanes])))
        psum = jnp.sum(p[0:Q_PER_KV, :], axis=0, keepdims=True)
        imp = _dot3_nt_l(cover_t, jnp.broadcast_to(psum, (LANES, nstr)))
        v_col = jnp.where(forced, FORCE_SCORE, jnp.where(valid, imp, NEG))
        v_row = jnp.concatenate([v_col[k * LANES:(k + 1) * LANES, :].T for k in range(nsel_pad // LANES)], axis=1)
        a = jnp.broadcast_to(v_row[0:1, :], (nsel_pad, nsel_pad))
        b = jnp.concatenate([v_col] * (nsel_pad // LANES), axis=1)
        beats = jnp.where(ii < jj, jnp.where(b >= a, 1.0, 0.0), jnp.where(b > a, 1.0, 0.0))
        cnt = jnp.sum(beats, axis=0, keepdims=True)
        sel_ref[0, g:g + 1, :] = jnp.where(cnt < N_SEL_BLOCKS, 1.0, 0.0)
        pw = _softmax_rows(_bdot_nt(qg, wall_s[:, lanes]) + tbw_ref[g], keep_w)
        ow.append(_heads_to_row(_bdot(pw, wall_s[:, vl])))
    gates = _rows8(brg_ref[0])
    o = (jnp.concatenate(oc, axis=1) * _dot3(gates, gexp_ref[0])[0:1, :]
         + jnp.concatenate(ow, axis=1) * _dot3(gates, gexp_ref[2])[0:1, :])
    o_ref[0] = o


def _sample_cw(q, brg, kvc, win, wnew, tbc, tbw, gexp, tq, past_w, nsel_pad):
    nseq = q.shape[0]
    qw = q.shape[2]
    wlen, ww = win.shape[1], win.shape[2]
    wrows = -(-(wlen + 1) // SUBLANES) * SUBLANES
    per3 = lambda s: (s, 0, 0)
    c3 = lambda s: (0, 0, 0)
    return pl.pallas_call(
        functools.partial(_sample_cw_kernel, tq=tq, past_w=past_w), grid=(nseq,),
        in_specs=[pl.BlockSpec((1, 1, qw), per3), pl.BlockSpec((1, 1, LANES), per3),
                  pl.BlockSpec((1,) + kvc.shape[1:], lambda s: (s, 0, 0, 0)),
                  pl.BlockSpec((1, wlen, ww), per3), pl.BlockSpec((1, 1, ww), per3),
                  pl.BlockSpec(tbc.shape, c3), pl.BlockSpec(tbw.shape, c3), pl.BlockSpec(gexp.shape, c3)],
        out_specs=[pl.BlockSpec((1, 1, qw), per3), pl.BlockSpec((1, SUBLANES, nsel_pad), per3),
                   pl.BlockSpec((1, wlen, ww), per3)],
        out_shape=[jax.ShapeDtypeStruct((nseq, 1, qw), F32), jax.ShapeDtypeStruct((nseq, SUBLANES, nsel_pad), F32),
                   jax.ShapeDtypeStruct((nseq, wlen, ww), F32)],
        scratch_shapes=[pltpu.VMEM((wrows, ww), F32)],
        compiler_params=_cparams(("parallel",)), name="sample_cmp_win",
    )(q, brg, kvc, win, wnew, tbc, tbw, gexp)


def _sample_sel_kernel(pg_ref, c31_ref, f0_ref, cache_ref, q_ref, brg_ref, snew_ref, code_ref, ocw_ref, tbl_ref, gexp_ref,
                       o_ref, buf, sem):
    s = pl.program_id(0)
    nseq = pl.num_programs(0)
    nblk = pg_ref.shape[1] // N_KV_HEADS
    page = cache_ref.shape[4]
    half = snew_ref.shape[2] // 2

    def copies(seq, slot):
        out = []
        for g in range(N_KV_HEADS):
            for k in range(nblk):
                src = cache_ref.at[pg_ref[seq, g * nblk + k]]
                for kv in range(2):
                    out.append(pltpu.make_async_copy(src.at[kv, g], buf.at[slot, kv, g, :, pl.ds(k * page, page)],
                                                     sem.at[slot]))
        return out

    @pl.when(s == 0)
    def _():
        for cp in copies(s, 0):
            cp.start()

    slot = s % 2

    @pl.when(s + 1 < nseq)
    def _():
        for cp in copies(s + 1, 1 - slot):
            cp.start()

    for cp in copies(s, slot):
        cp.wait()

    q_row = q_ref[0]
    new = snew_ref[0]
    outs = []
    for g in range(N_KV_HEADS):
        qg = _group_q(q_row, g)
        heads = [g * Q_PER_KV + r for r in range(Q_PER_KV)]
        code = _rows8(code_ref[0, g:g + 1, :])
        far = jnp.concatenate([jnp.full((1, 1), c31_ref[h], F32) for h in heads]
                              + [jnp.zeros((SUBLANES - Q_PER_KV, 1), F32)], axis=0)
        near = jnp.concatenate([tbl_ref[g]] * nblk, axis=1)
        sc = jnp.dot(qg.astype(BF16), buf[slot, 0, g].astype(BF16), preferred_element_type=F32)
        sc = jnp.where(code > 0.5, sc + jnp.where(code > 1.5, near, far), NEG)
        k_new = new[:, g * NSA_HEAD_DIM:(g + 1) * NSA_HEAD_DIM]
        v_new = new[:, half + g * NSA_HEAD_DIM:half + (g + 1) * NSA_HEAD_DIM]
        f0 = jnp.concatenate([jnp.full((1, 1), f0_ref[h], F32) for h in heads]
                             + [jnp.zeros((SUBLANES - Q_PER_KV, 1), F32)], axis=0)
        s_new = jnp.sum(qg * _rows8(k_new), axis=-1, keepdims=True) + f0
        new_on = _rows8(code_ref[0, N_KV_HEADS + g:N_KV_HEADS + g + 1, 0:1]) > 0.5
        s_new = jnp.where(new_on, s_new, NEG)
        m = jnp.maximum(jnp.max(sc, axis=-1, keepdims=True), s_new)
        p = jnp.where(code > 0.5, jnp.exp2(sc - m), 0.0)
        p_new = jnp.where(new_on, jnp.exp2(s_new - m), 0.0)
        l = jnp.sum(p, axis=-1, keepdims=True) + p_new
        inv = jnp.where(l > 0.0, 1.0 / l, 0.0)
        o = _bdot_nt(p, buf[slot, 1, g]) + p_new * _rows8(v_new)
        outs.append(_heads_to_row(o * inv))
    gates = _dot3(_rows8(brg_ref[0]), gexp_ref[1])[0:1, :]
    o_ref[0] = ocw_ref[0] + jnp.concatenate(outs, axis=1) * gates


def _sample_sel(pages, c31, f0, cache_t, q, brg, snew, code, ocw, tbl, gexp):
    nseq, qw = q.shape[0], q.shape[2]
    nblk = pages.shape[1] // N_KV_HEADS
    page = cache_t.shape[4]
    per3 = lambda s, *_: (s, 0, 0)
    c3 = lambda s, *_: (0, 0, 0)
    grid_spec = pltpu.PrefetchScalarGridSpec(
        num_scalar_prefetch=1, grid=(nseq,),
        in_specs=[pl.BlockSpec(memory_space=pltpu.SMEM), pl.BlockSpec(memory_space=pltpu.SMEM),
                  pl.BlockSpec(memory_space=pl.ANY),
                  pl.BlockSpec((1, 1, qw), per3), pl.BlockSpec((1, 1, LANES), per3),
                  pl.BlockSpec((1, 1, snew.shape[2]), per3), pl.BlockSpec((1,) + code.shape[1:], per3),
                  pl.BlockSpec((1, 1, qw), per3), pl.BlockSpec(tbl.shape, c3), pl.BlockSpec(gexp.shape, c3)],
        out_specs=pl.BlockSpec((1, 1, qw), per3),
        scratch_shapes=[pltpu.VMEM((2, 2, N_KV_HEADS, NSA_HEAD_DIM, nblk * page), F32),
                        pltpu.SemaphoreType.DMA((2,))])
    return pl.pallas_call(
        _sample_sel_kernel, grid_spec=grid_spec, out_shape=jax.ShapeDtypeStruct((nseq, 1, qw), F32),
        compiler_params=_cparams(("arbitrary",)), name="sample_sel",
    )(pages, c31, f0, cache_t, q, brg, snew, code, ocw, tbl, gexp)


def _sample_merge_kernel(yt_ref, xs_ref, z_ref, sgs_ref, dsk_ref, nw_ref, onesg_ref, ws_ref, ynsa_ref, sgn_ref, x_ref,
                         wn_ref, wo_ref, h_ref):
    gw = onesg_ref.shape[0]
    y = yt_ref[...].T + dsk_ref[...] * xs_ref[...]
    zz = z_ref[...]
    y = y * (zz * _sigmoid(zz))
    ms = _seg_sum(y * y, onesg_ref[...]) * (1.0 / gw)
    y = (y * lax.rsqrt(ms + RMS_EPS)) * nw_ref[...]
    u = sgs_ref[...] * _bdot(y, ws_ref[...]) + sgn_ref[...] * _bdot(ynsa_ref[...], wn_ref[...])
    h_ref[...] = x_ref[...] + _bdot(u, wo_ref[...])


def _sample_merge(y_t, xs, z, sgs, dsk, nw, onesg, ws, ynsa, sgn, x, wn, wo):
    return pl.pallas_call(_sample_merge_kernel, out_shape=jax.ShapeDtypeStruct(x.shape, F32),
                          compiler_params=pltpu.CompilerParams(vmem_limit_bytes=VMEM_LIMIT), name="sample_merge",
                          )(y_t, xs, z, sgs, dsk, nw, onesg, ws, ynsa, sgn, x, wn, wo)


def _bucket_lut():
    n = np.arange(MAX_DISTANCE + 1)
    max_exact = N_BUCKETS // 2
    nf = np.maximum(n, 1).astype(np.float32)
    large = max_exact + (np.log(nf / max_exact) / math.log(MAX_DISTANCE / max_exact)
                         * (N_BUCKETS - max_exact)).astype(np.int32)
    return np.where(n < max_exact, n, np.minimum(large, N_BUCKETS - 1))


def _bias_of_dist(rel_bias, dist):
    lut = _bucket_lut()
    idx = lut[np.clip(dist, 0, MAX_DISTANCE)]
    b = jnp.moveaxis(rel_bias.astype(F32)[idx], -1, 0) * LOG2E
    return jnp.where(jnp.asarray(dist >= 0), b, 0.0)


def _block_diag(w, reps):
    n, d, e = w.shape
    eye = jnp.eye(reps, dtype=w.dtype)
    return jnp.einsum("ab,nde->nadbe", eye, w).reshape(n, reps * d, reps * e)


def _ones_blocks(size, seg):
    return jnp.asarray(np.kron(np.eye(size // seg), np.ones((seg, seg))), BF16)


def _pad_cols(w, width):
    return jnp.pad(w, ((0, 0), (0, width - w.shape[1])))


def _prep(p):
    d_model = p["w_in"].shape[1]
    d_inner = p["w_ssm_out"].shape[1]
    n_heads = d_inner // SSM_HEAD_DIM
    conv_dim = p["conv_w"].shape[2]
    qw = N_Q_HEADS * NSA_HEAD_DIM
    kvw = 2 * N_KV_HEADS * NSA_HEAD_DIM
    splits = (d_inner, conv_dim, n_heads, qw, kvw, kvw, kvw, 3 * N_Q_HEADS, d_model, d_model)
    offs = np.concatenate([[0], np.cumsum(splits)])
    w_in = p["w_in"][0]
    seg = lambda k: w_in[:, offs[k]:offs[k + 1]]
    bf = lambda a: a.astype(BF16)
    o = {}
    o["w_ssm_in"] = [bf(seg(0)), bf(seg(1)), bf(_pad_cols(seg(2), LANES))]
    o["w_nsa_in"] = [bf(seg(3)), bf(seg(4)), bf(seg(5)), bf(seg(6)), bf(_pad_cols(seg(7), LANES)), bf(seg(8)),
                     bf(seg(9))]
    o["norm_mix"] = p["norm_mix_w"][0][None, :]
    kv_half = kvw // 2
    head_w = jnp.zeros((SUBLANES, qw), F32)
    head_w = head_w.at[0].set(jnp.tile(p["q_norm_w"][0], N_Q_HEADS))
    head_w = head_w.at[1, :kv_half].set(jnp.tile(p["k_sel_norm_w"][0], N_KV_HEADS))
    head_w = head_w.at[2, :kv_half].set(jnp.tile(p["k_win_norm_w"][0], N_KV_HEADS))
    o["head_w"] = head_w
    o["ones64"] = _ones_blocks(kv_half, NSA_HEAD_DIM)
    o["conv_w"] = p["conv_w"][0]
    o["conv_b"] = p["conv_b"][0][None, :]
    headp = jnp.zeros((SUBLANES, LANES), F32)
    headp = headp.at[0, :n_heads].set(p["dt_bias"][0]).at[1, :n_heads].set(p["a_log"][0])
    o["headp"] = headp
    o["dsk"] = jnp.repeat(p["d_skip"][0], SSM_HEAD_DIM)[None, :]
    o["ssm_nw"] = p["ssm_norm_w"][0][None, :]
    expand = np.zeros((LANES, d_inner), np.float32)
    for h in range(n_heads):
        expand[h, h * SSM_HEAD_DIM:(h + 1) * SSM_HEAD_DIM] = 1.0
    o["expand"] = jnp.asarray(expand, BF16)
    o["tri"] = jnp.asarray(np.tril(np.ones((SSD_CHUNK, SSD_CHUNK), np.float32)), BF16)
    o["onesg"] = jnp.ones((d_inner // N_SSM_GROUPS,) * 2, BF16)
    o["w_ssm_out"] = bf(p["w_ssm_out"][0])
    o["cmp_wk"] = bf(_block_diag(p["cmp_w_k"][0], N_KV_HEADS))
    o["cmp_wv"] = bf(_block_diag(p["cmp_w_v"][0], N_KV_HEADS))
    o["cmp_pek"] = jnp.tile(p["cmp_pe_k"][0], (1, N_KV_HEADS))
    o["cmp_pev"] = jnp.tile(p["cmp_pe_v"][0], (1, N_KV_HEADS))
    o["kc_nw"] = jnp.tile(p["k_cmp_norm_w"][0], N_KV_HEADS)[None, :]
    rel = p["rel_bias"]
    band = 2 * CMP_STRIDE
    i = np.arange(QT)[:, None]
    c = np.arange(band)[None, :]
    d_band = i + CMP_STRIDE * (band // 2) - CMP_STRIDE * c - (CMP_BLOCK - 1)
    d_band = np.concatenate([d_band, np.full((QT, 1), MAX_DISTANCE)], axis=1)
    tb = _bias_of_dist(rel, d_band)
    o["cmp_tb"] = jnp.pad(tb, ((0, 0), (0, 0), (0, 2 * band - tb.shape[2])))
    jk = np.arange(QT)[:, None]
    iq = np.arange(QT)[None, :]
    o["b0"] = _bias_of_dist(rel, iq - jk)
    o["b1"] = _bias_of_dist(rel, QT + iq - jk)
    o["c31"] = rel[N_BUCKETS - 1].astype(F32) * LOG2E
    gexp = np.zeros((3, LANES, qw), np.float32)
    for h in range(N_Q_HEADS):
        for k in range(3):
            gexp[k, 3 * h + k, h * NSA_HEAD_DIM:(h + 1) * NSA_HEAD_DIM] = 1.0
    o["gexp"] = jnp.asarray(gexp, BF16)
    o["w_nsa_out"] = bf(p["w_nsa_out"][0])
    o["w_out"] = bf(p["w_out"][0])
    o["norm_ffn"] = p["norm_ffn_w"][0][None, :]
    ne = p["w_router"].shape[2]
    o["w_router"] = _pad_cols(p["w_router"][0], LANES)
    o["b_router"] = jnp.full((1, LANES), NEG, F32).at[0, :ne].set(p["b_router"][0])
    o["w1"] = bf(p["w_gate_up"][0])
    o["b1e"] = p["b_gate_up"][0][:, None, :]
    o["w2"] = bf(p["w_down"][0])
    o["b2e"] = p["b_down"][0][:, None, :]
    return o


def _kv_layouts(kv, batch):
    n = kv.shape[0]
    s = n // batch
    half = kv.shape[1] // 2
    k = kv[:, :half].astype(BF16).reshape(batch, s, N_KV_HEADS, NSA_HEAD_DIM).transpose(0, 2, 1, 3)
    v = kv[:, half:].astype(BF16).reshape(batch, s // QT, QT, N_KV_HEADS, NSA_HEAD_DIM).transpose(0, 3, 1, 4, 2)
    extra = jnp.zeros(v.shape[:3] + (SUBLANES, QT), BF16).at[:, :, :, 0, :].set(1.0)
    return k, jnp.concatenate([v, extra], axis=3)


def _prompt_mixer(x, o, batch):
    z, xbc, dt = _proj(x, o["norm_mix"], o["ones64"], o["head_w"], o["w_ssm_in"], ["raw", "raw", "raw"], 256)
    q, kvc, kvs, kvw, brg, sg_ssm, sg_nsa = _proj(
        x, o["norm_mix"], o["ones64"], o["head_w"], o["w_nsa_in"], ["q", "raw", "ks", "kw", "sig", "sig", "sig"], 256)
    m_ssm, h_t = _ssd_prompt(xbc, z, dt, sg_ssm, o["conv_w"], o["conv_b"], o["headp"], o["dsk"], o["ssm_nw"],
                             o["expand"], o["tri"], o["onesg"], o["w_ssm_out"], batch)
    kc, vc = _cmp_build(kvc, o["cmp_wk"], o["cmp_wv"], o["cmp_pek"], o["cmp_pev"], o["ones64"], o["kc_nw"], batch)
    ocmp, sel = _cmp_attn(q, brg, kc, vc, o["cmp_tb"], o["gexp"], batch)
    ks, vs_t = _kv_layouts(kvs, batch)
    kw, vw_t = _kv_layouts(kvw, batch)
    h = _nsa_main(o["c31"], q, brg, sel, ocmp, ks, vs_t, kw, vw_t, o["b0"], o["b1"], m_ssm, sg_nsa, x,
                  o["w_nsa_out"], o["w_out"], batch)
    return h, (kvc, kvs, kvw, h_t, xbc)


MOE_TOKEN_TILE = 384
MOE_GROUP_TILE = 512


def _moe(h_all, o):
    n, d = h_all.shape
    tm, tg = MOE_TOKEN_TILE, MOE_GROUP_TILE
    ne = o["w1"].shape[0]
    n_pad = -(-n // tm) * tm
    hp = jnp.pad(h_all, ((0, n_pad - n), (0, 0))) if n_pad != n else h_all
    hn, meta, cnt = _router(hp, o["norm_ffn"], o["w_router"], o["b_router"], tm, n)
    eid = meta[:n, META_EXPERT:META_EXPERT + TOP_K].astype(jnp.int32)
    rank = meta[:n, META_RANK:META_RANK + TOP_K].astype(jnp.int32)
    count = cnt[0, :ne].astype(jnp.int32)
    tiles = (count + tg - 1) // tg
    tile_end = jnp.cumsum(tiles)
    pos = ((tile_end - tiles)[eid] * tg + rank).reshape(-1)
    n_tiles = -(-(n * TOP_K) // tg) + ne
    tile_expert = jnp.minimum(jnp.searchsorted(tile_end, jnp.arange(n_tiles), side="right"), ne - 1)
    sp = jnp.concatenate([tile_expert, tile_end[-1:]]).astype(jnp.int32)
    src = jnp.zeros((n_tiles * tg,), jnp.int32).at[pos].set(jnp.arange(n * TOP_K, dtype=jnp.int32) // TOP_K)
    y_sorted = _moe_group(sp, hn[src], o["w1"], o["b1e"], o["w2"], o["b2e"], tg)
    yk = y_sorted[pos].reshape(n, TOP_K * d)
    if n_pad != n:
        yk = jnp.pad(yk, ((0, n_pad - n), (0, 0)))
    return _moe_combine(hp, yk, meta, tm)[:n]


def kernel(x_prompt, x_sample, cache_cmp, cache_sel, cache_win, state_ssm, state_conv, page_table, norm_mix_w, w_in,
           conv_w, conv_b, dt_bias, a_log, d_skip, ssm_norm_w, w_ssm_out, q_norm_w, k_cmp_norm_w, k_sel_norm_w,
           k_win_norm_w, cmp_pe_k, cmp_w_k, cmp_pe_v, cmp_w_v, rel_bias, w_nsa_out, w_out, norm_ffn_w, w_router,
           b_router, w_gate_up, b_gate_up, w_down, b_down):
    params = dict(norm_mix_w=norm_mix_w, w_in=w_in, conv_w=conv_w, conv_b=conv_b, dt_bias=dt_bias, a_log=a_log,
                  d_skip=d_skip, ssm_norm_w=ssm_norm_w, w_ssm_out=w_ssm_out, q_norm_w=q_norm_w,
                  k_cmp_norm_w=k_cmp_norm_w, k_sel_norm_w=k_sel_norm_w, k_win_norm_w=k_win_norm_w,
                  cmp_pe_k=cmp_pe_k, cmp_w_k=cmp_w_k, cmp_pe_v=cmp_pe_v, cmp_w_v=cmp_w_v, rel_bias=rel_bias,
                  w_nsa_out=w_nsa_out, w_out=w_out, norm_ffn_w=norm_ffn_w, w_router=w_router, b_router=b_router,
                  w_gate_up=w_gate_up, b_gate_up=b_gate_up, w_down=w_down, b_down=b_down)
    o = _prep(params)
    bsz, s, d = x_prompt.shape
    db, t, _ = x_sample.shape
    kvshape = (2, N_KV_HEADS, NSA_HEAD_DIM)
    hp, (kvc, kvs, kvw, h_t, xbc) = _prompt_mixer(x_prompt.reshape(bsz * s, d), o, bsz)
    assert t == 1, "the sample group decodes one token per sequence"
    hs, (kvc_s, kvs_s, win_s, h_s, conv_s) = _sample_mixer(
        x_sample.reshape(db, d), o, cache_cmp[0], cache_sel[0], cache_win[0], state_ssm[0], state_conv[0],
        page_table, rel_bias)
    y_all = _moe(jnp.concatenate([hp, hs], axis=0), o)
    wlen = min(WINDOW, s)
    n_heads = h_t.shape[2] // SSM_HEAD_DIM
    outs_p = (kvc.reshape((1, bsz, s) + kvshape), kvs.reshape((1, bsz, s) + kvshape),
              kvw.reshape((bsz, s) + kvshape)[None, :, s - wlen:],
              h_t.reshape(bsz, D_STATE, n_heads, SSM_HEAD_DIM).transpose(0, 2, 3, 1)[None],
              xbc.reshape(bsz, s, -1)[None, :, s - (D_CONV - 1):])
    outs_s = (kvc_s.reshape((1, db, t) + kvshape), kvs_s.reshape((1, db, t) + kvshape),
              win_s.reshape((1, db, win_s.shape[1]) + kvshape), h_s.reshape((1, db, n_heads, SSM_HEAD_DIM, D_STATE)),
              conv_s.reshape(1, db, D_CONV - 1, -1))
    return (y_all[:bsz * s].reshape(bsz, s, d), y_all[bsz * s:].reshape(db, t, d)) + outs_p + outs_s


def _sample_mixer(x, o, cache_cmp, cache_sel, cache_win, state_ssm, state_conv, page_table, rel):
    n = x.shape[0]
    z, xbc, dt = _proj(x, o["norm_mix"], o["ones64"], o["head_w"], o["w_ssm_in"], ["raw", "raw", "raw"], n)
    q, kvc, kvs, kvw, brg, sg_ssm, sg_nsa = _proj(
        x, o["norm_mix"], o["ones64"], o["head_w"], o["w_nsa_in"], ["q", "raw", "ks", "kw", "sig", "sig", "sig"], n)
    d_inner = z.shape[1]
    n_heads = d_inner // SSM_HEAD_DIM
    conv_new, xs, bm, ct, dtx_t, dec = _mamba_prep(xbc, state_conv.reshape(n, -1), dt, o["conv_w"], o["conv_b"],
                                                   o["headp"], o["expand"])
    h_new, y_t = _mamba_state(dec[:, :n_heads], state_ssm.reshape(n, d_inner, D_STATE), dtx_t, bm, ct)
    pool, page = cache_cmp.shape[0], cache_cmp.shape[1]
    n_pages = page_table.shape[1]
    past = n_pages * page
    tq = past
    assert page >= MAX_DISTANCE and page % SEL_BLOCK == 0 and past % CMP_STRIDE == 0
    half = kvc.shape[1] // 2
    nch = half // LANES
    cmp_t = jnp.transpose(cache_cmp, (0, 2, 3, 4, 1)).reshape(pool, 2 * nch, LANES, page)
    sel_t = jnp.transpose(cache_sel, (0, 2, 3, 4, 1))
    nstr = -(-(past // CMP_STRIDE + 1) // SUBLANES) * SUBLANES
    w_kv = jnp.stack([o["cmp_wk"], o["cmp_wv"]])
    pe_kv = jnp.stack([o["cmp_pek"], o["cmp_pev"]])
    tok = _page_cmp(page_table, cmp_t, kvc[:, None, :], w_kv, _cmp_const(pe_kv, w_kv), o["ones64"], o["kc_nw"], nstr)
    pad_heads = lambda b: jnp.pad(b.reshape(N_KV_HEADS, Q_PER_KV, -1), ((0, 0), (0, SUBLANES - Q_PER_KV), (0, 0)))
    tbc = pad_heads(_bias_of_dist(rel, tq - (CMP_STRIDE * np.arange(nstr) + CMP_BLOCK - 1)))
    wlen = cache_win.shape[1]
    wrows = -(-(wlen + 1) // SUBLANES) * SUBLANES
    past_w = past - wlen
    tbw = pad_heads(_bias_of_dist(rel, tq - (past_w + np.arange(wrows))))
    n_past_sel = past // SEL_BLOCK
    nsel = n_past_sel + 1
    nsel_pad = -(-nsel // LANES) * LANES
    o_cw, selmask, win_new = _sample_cw(q[:, None, :], brg[:, None, :], tok, cache_win.reshape(n, wlen, -1),
                                        kvw[:, None, :], tbc, tbw, o["gexp"], tq, past_w, nsel_pad)
    nblk = min(N_SEL_BLOCKS, nsel)
    picked = selmask[:, :N_KV_HEADS, :nsel] > 0.5
    order = jnp.cumsum(picked, axis=-1) - 1
    hit = picked[..., None] & (order[..., None] == jnp.arange(nblk))
    idx = jnp.sum(jnp.where(hit, jnp.arange(nsel)[:, None], 0), axis=2)
    is_past = idx < n_past_sel
    jp = jnp.minimum(idx, n_past_sel - 1)
    per_page = page // SEL_BLOCK
    pg = jp // per_page
    phys = jnp.take_along_axis(page_table, pg.reshape(n, -1), axis=1).astype(jnp.int32)
    lane_blk = (np.arange(page) // SEL_BLOCK)[None, None, None, :]
    attended = is_past[..., None] & (lane_blk == (jp % per_page)[..., None])
    code = jnp.where(attended, jnp.where((pg == n_pages - 1)[..., None], 2.0, 1.0), 0.0).reshape(n, N_KV_HEADS, -1)
    new_on = jnp.any(idx >= n_past_sel, axis=-1).astype(F32)
    code = jnp.concatenate([code, jnp.broadcast_to(new_on[..., None], code.shape)], axis=1)
    tbl = pad_heads(_bias_of_dist(rel, tq - ((n_pages - 1) * page + np.arange(page))))
    f0 = rel[_bucket_lut()[0]].astype(F32) * LOG2E
    y_nsa = _sample_sel(phys, o["c31"], f0, sel_t, q[:, None, :], brg[:, None, :], kvs[:, None, :], code, o_cw, tbl,
                        o["gexp"])
    h = _sample_merge(y_t, xs, z, sg_ssm, o["dsk"], o["ssm_nw"], o["onesg"], o["w_ssm_out"], y_nsa[:, 0], sg_nsa, x,
                      o["w_nsa_out"], o["w_out"])
    return h, (kvc, kvs, win_new, h_new, conv_new)
```

```python
import functools
import math

import jax
import jax.numpy as jnp
import numpy as np
from jax import lax
from jax.experimental import pallas as pl
from jax.experimental.pallas import tpu as pltpu

F32 = jnp.float32
BF16 = jnp.bfloat16

SSM_HEAD_DIM = 64
N_SSM_GROUPS = 4
D_STATE = 128
D_CONV = 4
SSD_CHUNK = 128
NSA_HEAD_DIM = 64
N_Q_HEADS = 16
N_KV_HEADS = 4
Q_PER_KV = N_Q_HEADS // N_KV_HEADS
CMP_BLOCK = 32
CMP_STRIDE = 16
SEL_BLOCK = 64
N_SEL_BLOCKS = 16
WINDOW = 512
N_BUCKETS = 32
MAX_DISTANCE = 128
TOP_K = 4
SWIGLU_LIMIT = 7.0
SWIGLU_ALPHA = 1.702
RMS_EPS = 1e-6
LOG2E = math.log2(math.e)
NEG = -1e30
FORCE_SCORE = 1e9

LANES = 128
SUBLANES = 8
QT = 128
VMEM_LIMIT = 56 * 1024 * 1024


def _cparams(sem):
    return pltpu.CompilerParams(dimension_semantics=sem, vmem_limit_bytes=VMEM_LIMIT)


def _bdot(a, b):
    return jnp.dot(a.astype(BF16), b.astype(BF16), preferred_element_type=F32)


def _bdot_nt(a, b):
    return lax.dot_general(a.astype(BF16), b.astype(BF16), (((1,), (1,)), ((), ())),
                           preferred_element_type=F32)


def _split3(a):
    hi = a.astype(BF16)
    r = a - hi.astype(F32)
    mid = r.astype(BF16)
    lo = (r - mid.astype(F32)).astype(BF16)
    return hi, mid, lo


def _dot3(a, b):
    hi, mid, lo = _split3(a)
    d = lambda p: jnp.dot(p, b, preferred_element_type=F32)
    return (d(hi) + d(mid)) + d(lo)


def _dot3_l(a, b):
    hi, mid, lo = _split3(b)
    d = lambda p: jnp.dot(a, p, preferred_element_type=F32)
    return (d(hi) + d(mid)) + d(lo)


def _dot3_nt_l(a, b):
    hi, mid, lo = _split3(b)
    d = lambda p: lax.dot_general(a, p, (((1,), (1,)), ((), ())), preferred_element_type=F32)
    return (d(hi) + d(mid)) + d(lo)


def _seg_sum(y, ones_blk):
    c = ones_blk.shape[0]
    outs = []
    for k in range(y.shape[1] // c):
        outs.append(_dot3(y[:, k * c:(k + 1) * c], ones_blk))
    return outs[0] if len(outs) == 1 else jnp.concatenate(outs, axis=1)


def _sigmoid(x):
    return 1.0 / (1.0 + jnp.exp(-x))


def _rms_rows(x, w):
    ms = jnp.mean(x * x, axis=-1, keepdims=True)
    return (x * lax.rsqrt(ms + RMS_EPS)) * w


def _head_rms(y, ones64, w, seg):
    ms = _seg_sum(y * y, ones64) * (1.0 / seg)
    return (y * lax.rsqrt(ms + RMS_EPS)) * w


def _proj_kernel(kinds, x_ref, nw_ref, ones_ref, hw_ref, *refs):
    n = len(kinds)
    w_refs, o_refs = refs[:n], refs[n:]
    xn = _rms_rows(x_ref[...], nw_ref[...]).astype(BF16)
    ones64 = ones_ref[...]
    for kind, w_ref, o_ref in zip(kinds, w_refs, o_refs):
        y = jnp.dot(xn, w_ref[...], preferred_element_type=F32)
        if kind == "sig":
            y = _sigmoid(y)
        elif kind == "q":
            y = _head_rms(y, ones64, hw_ref[0:1, :], NSA_HEAD_DIM) * (NSA_HEAD_DIM ** -0.5 * LOG2E)
        elif kind in ("ks", "kw"):
            row = 1 if kind == "ks" else 2
            half = y.shape[1] // 2
            k = _head_rms(y[:, :half], ones64, hw_ref[row:row + 1, :half], NSA_HEAD_DIM)
            y = jnp.concatenate([k, y[:, half:]], axis=1)
        o_ref[...] = y


def _proj(x, norm_w, ones64, head_w, weights, kinds, tm):
    n, d = x.shape
    assert n % tm == 0
    const = lambda i: (0, 0)
    in_specs = [pl.BlockSpec((tm, d), lambda i: (i, 0)),
                pl.BlockSpec((1, d), const),
                pl.BlockSpec(ones64.shape, const),
                pl.BlockSpec(head_w.shape, const)]
    in_specs += [pl.BlockSpec(w.shape, const) for w in weights]
    out_specs = [pl.BlockSpec((tm, w.shape[1]), lambda i: (i, 0)) for w in weights]
    out_shape = [jax.ShapeDtypeStruct((n, w.shape[1]), F32) for w in weights]
    return pl.pallas_call(
        functools.partial(_proj_kernel, tuple(kinds)),
        grid=(n // tm,), in_specs=in_specs, out_specs=out_specs, out_shape=out_shape,
        compiler_params=_cparams(("parallel",)), name="in_proj",
    )(x, norm_w, ones64, head_w, *weights)


def _softplus(x):
    return jnp.maximum(x, 0.0) + jnp.log1p(jnp.exp(-jnp.abs(x)))


def _ssd_kernel(xbc_ref, z_ref, dt_ref, sg_ref, cw_ref, cb_ref, hp_ref, dsk_ref, nw_ref, e_ref, tri_ref,
                onesg_ref, wout_ref, o_ref, ht_ref, ht_s, ext_s):
    c = pl.program_id(1)
    q = SSD_CHUNK
    d_inner = z_ref.shape[1]
    gw = d_inner // N_SSM_GROUPS
    hpg = gw // SSM_HEAD_DIM
    tail = SUBLANES

    @pl.when(c == 0)
    def _():
        ht_s[...] = jnp.zeros_like(ht_s)
        ext_s[0:tail, :] = jnp.zeros((tail, ext_s.shape[1]), F32)

    xb = xbc_ref[...]
    ext_s[tail:tail + q, :] = xb
    u = cb_ref[...] + cw_ref[D_CONV - 1:D_CONV, :] * xb
    for k in range(1, D_CONV):
        u = u + cw_ref[D_CONV - 1 - k:D_CONV - k, :] * ext_s[tail - k:tail - k + q, :]
    ext_s[0:tail, :] = xb[q - tail:q, :]
    u = u * _sigmoid(u)
    xs = u[:, :d_inner]
    bm = u[:, d_inner:d_inner + N_SSM_GROUPS * D_STATE]
    cm = u[:, d_inner + N_SSM_GROUPS * D_STATE:]

    dtv = _softplus(dt_ref[...] + hp_ref[0:1, :])
    a = dtv * (-jnp.exp(hp_ref[1:2, :]))
    acum = _dot3_l(tri_ref[...], a)
    acum_t = acum.T
    eacum = jnp.exp(acum)
    w_end = jnp.exp(acum[q - 1:q, :] - acum) * dtv
    e = e_ref[...]
    dt_x = _dot3(dtv, e)
    we_x = _dot3(w_end, e)
    ea_x = _dot3(eacum, e)
    xdt = (xs * dt_x).astype(BF16)
    xdtw = (xs * we_x).astype(BF16)
    ii = lax.broadcasted_iota(jnp.int32, (q, q), 0)
    jj = lax.broadcasted_iota(jnp.int32, (q, q), 1)
    causal = jj <= ii

    groups = range(N_SSM_GROUPS)
    gsl = [slice(g * gw, (g + 1) * gw) for g in groups]
    cgs = [cm[:, g * D_STATE:(g + 1) * D_STATE].astype(BF16) for g in groups]
    bgs = [bm[:, g * D_STATE:(g + 1) * D_STATE] for g in groups]
    cbms = [_bdot_nt(cgs[g], bgs[g]) for g in groups]
    hgs = [ht_s[:, gsl[g]] for g in groups]
    y_inter = [jnp.dot(cgs[g], hgs[g].astype(BF16), preferred_element_type=F32) for g in groups]
    sts = [jnp.dot(bgs[g].T.astype(BF16), xdtw[:, gsl[g]], preferred_element_type=F32) for g in groups]
    ys = []
    for g in groups:
        yh = []
        for r in range(hpg):
            h = g * hpg + r
            seg = acum[:, h:h + 1] - acum_t[h:h + 1, :]
            dec = jnp.exp(jnp.where(causal, seg, NEG))
            m = (cbms[g] * dec).astype(BF16)
            yh.append(jnp.dot(m, xdt[:, h * SSM_HEAD_DIM:(h + 1) * SSM_HEAD_DIM], preferred_element_type=F32))
        ys.append(y_inter[g] * ea_x[:, gsl[g]] + jnp.concatenate(yh, axis=1))
        ht_s[:, gsl[g]] = hgs[g] * ea_x[q - 1:q, gsl[g]] + sts[g]
    y = jnp.concatenate(ys, axis=1) + dsk_ref[...] * xs
    zz = z_ref[...]
    y = y * (zz * _sigmoid(zz))
    ms = _seg_sum(y * y, onesg_ref[...]) * (1.0 / gw)
    y = (y * lax.rsqrt(ms + RMS_EPS)) * nw_ref[...]
    o_ref[...] = sg_ref[...] * _bdot(y, wout_ref[...])

    @pl.when(c == pl.num_programs(1) - 1)
    def _():
        ht_ref[0] = ht_s[...]


def _ssd_prompt(xbc, z, dt, sg, conv_w, conv_b, headp, dsk, nw, expand, tri, onesg, wout, batch):
    n, conv_dim = xbc.shape
    d_inner = z.shape[1]
    d_model = wout.shape[1]
    q = SSD_CHUNK
    nc = n // batch // q
    const = lambda b, c: (0, 0)
    rows = lambda b, c: (b * nc + c, 0)
    in_specs = [pl.BlockSpec((q, conv_dim), rows), pl.BlockSpec((q, d_inner), rows),
                pl.BlockSpec((q, LANES), rows), pl.BlockSpec((q, d_model), rows),
                pl.BlockSpec(conv_w.shape, const), pl.BlockSpec(conv_b.shape, const),
                pl.BlockSpec(headp.shape, const), pl.BlockSpec(dsk.shape, const),
                pl.BlockSpec(nw.shape, const), pl.BlockSpec(expand.shape, const),
                pl.BlockSpec(tri.shape, const), pl.BlockSpec(onesg.shape, const),
                pl.BlockSpec(wout.shape, const)]
    out_specs = [pl.BlockSpec((q, d_model), rows),
                 pl.BlockSpec((1, D_STATE, d_inner), lambda b, c: (b, 0, 0))]
    out_shape = [jax.ShapeDtypeStruct((n, d_model), F32),
                 jax.ShapeDtypeStruct((batch, D_STATE, d_inner), F32)]
    return pl.pallas_call(
        _ssd_kernel, grid=(batch, nc), in_specs=in_specs, out_specs=out_specs, out_shape=out_shape,
        scratch_shapes=[pltpu.VMEM((D_STATE, d_inner), F32), pltpu.VMEM((SUBLANES + q, conv_dim), F32)],
        compiler_params=_cparams(("parallel", "arbitrary")), name="ssd_prompt",
    )(xbc, z, dt, sg, conv_w, conv_b, headp, dsk, nw, expand, tri, onesg, wout)


def _cmp_build_kernel(kvc_ref, wk_ref, wv_ref, pek_ref, pev_ref, ones_ref, nw_ref, kc_ref, vc_ref):
    ns = kc_ref.shape[0]
    kw = kc_ref.shape[1]
    lo_k = hi_k = lo_v = hi_v = None
    add = lambda acc, v: v if acc is None else acc + v
    nch = 2 * kw // LANES
    for i in range(CMP_STRIDE):
        x = jnp.concatenate([kvc_ref[pl.ds(nch * i + c, ns, stride=nch * CMP_STRIDE), :] for c in range(nch)], axis=1)
        xk, xv = x[:, :kw], x[:, kw:]
        j = CMP_STRIDE + i
        lo_k = add(lo_k, _bdot(xk + pek_ref[i:i + 1, :], wk_ref[i]))
        hi_k = add(hi_k, _bdot(xk + pek_ref[j:j + 1, :], wk_ref[j]))
        lo_v = add(lo_v, _bdot(xv + pev_ref[i:i + 1, :], wv_ref[i]))
        hi_v = add(hi_v, _bdot(xv + pev_ref[j:j + 1, :], wv_ref[j]))
    kc = lo_k + pltpu.roll(hi_k, ns - 1, axis=0)
    vc_ref[...] = lo_v + pltpu.roll(hi_v, ns - 1, axis=0)
    kc_ref[...] = _head_rms(kc, ones_ref[...], nw_ref[...], NSA_HEAD_DIM)


def _cmp_build(kvc, wk, wv, pek, pev, ones64, nw, batch):
    n, w = kvc.shape
    s = n // batch
    ns = s // CMP_STRIDE
    kw = w // 2
    c2 = lambda b: (0, 0)
    c3 = lambda b: (0, 0, 0)
    nch = w // LANES
    kvc = kvc.reshape(n * nch, LANES)
    return pl.pallas_call(
        _cmp_build_kernel, grid=(batch,),
        in_specs=[pl.BlockSpec((s * nch, LANES), lambda b: (b, 0)), pl.BlockSpec(wk.shape, c3),
                  pl.BlockSpec(wv.shape, c3),
                  pl.BlockSpec(pek.shape, c2), pl.BlockSpec(pev.shape, c2), pl.BlockSpec(ones64.shape, c2),
                  pl.BlockSpec(nw.shape, c2)],
        out_specs=[pl.BlockSpec((ns, kw), lambda b: (b, 0)), pl.BlockSpec((ns, kw), lambda b: (b, 0))],
        out_shape=[jax.ShapeDtypeStruct((batch * ns, kw), F32)] * 2,
        compiler_params=_cparams(("parallel",)), name="cmp_build",
    )(kvc, wk, wv, pek, pev, ones64, nw)


def _rank_select(v, n_keep):
    n = v.shape[0]
    rows = SUBLANES
    sub = lax.broadcasted_iota(jnp.int32, (rows, v.shape[1]), 0)
    chunks = [v[c * rows:(c + 1) * rows, :] for c in range(n // rows)]
    cnts = [jnp.zeros(ch.shape, F32) for ch in chunks]
    for i in range(n):
        ri = v[i:i + 1, :]
        for c, ch in enumerate(chunks):
            if c * rows > i:
                beats = jnp.where(ri >= ch, 1.0, 0.0)
            elif (c + 1) * rows - 1 <= i:
                beats = jnp.where(ri > ch, 1.0, 0.0)
            else:
                beats = jnp.where(sub > i - c * rows, jnp.where(ri >= ch, 1.0, 0.0), jnp.where(ri > ch, 1.0, 0.0))
            cnts[c] = cnts[c] + beats
    return jnp.where(jnp.concatenate(cnts, axis=0) < n_keep, 1.0, 0.0)


def _cmp_attn_kernel(q_ref, brg_ref, kc_ref, vc_ref, tb_ref, gexp_ref, ocmp_ref, sel_ref):
    qi = pl.program_id(1)
    t0 = qi * QT
    ns = kc_ref.shape[0]
    nsel = sel_ref.shape[2]
    nb = tb_ref.shape[2]
    band = 2 * CMP_STRIDE
    tt = t0 + lax.broadcasted_iota(jnp.int32, (QT, ns), 0)
    nn = lax.broadcasted_iota(jnp.int32, (QT, ns), 1)
    mask = (CMP_STRIDE * nn + (CMP_BLOCK - 1)) <= tt
    first = (QT // CMP_STRIDE) * qi - band // 2
    cc = lax.broadcasted_iota(jnp.int32, (nb, ns), 0)
    n2 = lax.broadcasted_iota(jnp.int32, (nb, ns), 1)
    shift = jnp.where(((cc < band) & (n2 == first + cc)) | ((cc == band) & (n2 < first)), 1.0, 0.0).astype(BF16)
    jb = lax.broadcasted_iota(jnp.int32, (nsel, ns), 0) * SEL_BLOCK
    cs = lax.broadcasted_iota(jnp.int32, (nsel, ns), 1) * CMP_STRIDE
    cover_t = jnp.where((cs < jb + SEL_BLOCK) & (cs + CMP_BLOCK > jb), 1.0, 0.0).astype(BF16)
    jidx = lax.broadcasted_iota(jnp.int32, (nsel, QT), 0)
    tq = t0 + lax.broadcasted_iota(jnp.int32, (nsel, QT), 1)
    valid = jidx * SEL_BLOCK <= tq
    cur = tq // SEL_BLOCK
    forced = valid & ((jidx == 0) | (jidx == cur) | (jidx == cur - 1))

    q = q_ref[...].astype(BF16)
    head_cols = lambda a, g: a[:, g * NSA_HEAD_DIM:(g + 1) * NSA_HEAD_DIM]
    kgs = [head_cols(kc_ref, g).astype(BF16) for g in range(N_KV_HEADS)]
    vgs = [head_cols(vc_ref, g).astype(BF16) for g in range(N_KV_HEADS)]
    raw = [_bdot_nt(head_cols(q, h), kgs[h // Q_PER_KV]) + _dot3(tb_ref[h], shift) for h in range(N_Q_HEADS)]
    ps = []
    for s in raw:
        s = jnp.where(mask, s, NEG)
        m = jnp.max(s, axis=-1, keepdims=True)
        p = jnp.where(mask, jnp.exp2(s - m), 0.0)
        l = jnp.sum(p, axis=-1, keepdims=True)
        ps.append(p * jnp.where(l > 0.0, 1.0 / l, 0.0))
    outs = [_bdot(ps[h], vgs[h // Q_PER_KV]) for h in range(N_Q_HEADS)]
    imps = []
    for g in range(N_KV_HEADS):
        psum = functools.reduce(lambda a, b: a + b, ps[g * Q_PER_KV:(g + 1) * Q_PER_KV])
        imps.append(_dot3_nt_l(cover_t, psum))
    for g in range(N_KV_HEADS):
        v = jnp.where(forced, FORCE_SCORE, jnp.where(valid, imps[g], NEG))
        sel_ref[0, g] = _rank_select(v, N_SEL_BLOCKS)
    ocmp_ref[...] = jnp.concatenate(outs, axis=1) * _dot3(brg_ref[...], gexp_ref[0])


def _cmp_attn(q, brg, kc, vc, tb, gexp, batch):
    n, qw = q.shape
    s = n // batch
    nq = s // QT
    ns, kw = kc.shape[0] // batch, kc.shape[1]
    nsel = s // SEL_BLOCK
    rows = lambda b, i: (b * nq + i, 0)
    per_b = lambda b, i: (b, 0)
    return pl.pallas_call(
        _cmp_attn_kernel, grid=(batch, nq),
        in_specs=[pl.BlockSpec((QT, qw), rows), pl.BlockSpec((QT, LANES), rows),
                  pl.BlockSpec((ns, kw), per_b), pl.BlockSpec((ns, kw), per_b),
                  pl.BlockSpec(tb.shape, lambda b, i: (0, 0, 0)),
                  pl.BlockSpec((1,) + gexp.shape[1:], lambda b, i: (0, 0, 0))],
        out_specs=[pl.BlockSpec((QT, qw), rows), pl.BlockSpec((1, N_KV_HEADS, nsel, QT), lambda b, i: (b, 0, 0, i))],
        out_shape=[jax.ShapeDtypeStruct((n, qw), F32), jax.ShapeDtypeStruct((batch, N_KV_HEADS, nsel, s), F32)],
        compiler_params=_cparams(("parallel", "parallel")), name="cmp_attn",
    )(q, brg, kc, vc, tb, gexp)


def _attn_round(states, raw, keeps, biases, vts, shifts):
    stats = []
    for st, s, keep, bias, shift in zip(states, raw, keeps, biases, shifts):
        s = _masked_scores(s, keep, bias)
        tile_max = jnp.max(s, axis=0, keepdims=True)
        if shift is not None:
            tile_max = tile_max + shift
        m_new = tile_max if st is None else jnp.maximum(st[0], tile_max)
        alpha = None if st is None else jnp.exp2(st[0] - m_new)
        p = jnp.exp2(s - (m_new if shift is None else m_new - shift)).astype(BF16)
        stats.append((m_new, alpha, p))
    out = []
    for st, (m_new, alpha, p), vt in zip(states, stats, vts):
        pv = jnp.dot(vt, p, preferred_element_type=F32)
        out.append((m_new, pv if st is None else alpha * st[1] + pv))
    return out


def _masked_scores(s, keep, bias=None):
    parts = []
    for r in range(s.shape[1] // QT):
        v = s[:, r * QT:(r + 1) * QT]
        parts.append(jnp.where(keep, v if bias is None else v + bias[r], NEG))
    return jnp.concatenate(parts, axis=1)


def _nsa_main_kernel(c31_ref, q_ref, brg_ref, sel_ref, ocmp_ref, ks_ref, vs_ref, kw_ref, vw_ref, b0_ref, b1_ref,
                     mssm_ref, sgn_ref, x_ref, wn_ref, wo_ref, h_ref):
    qi = pl.program_id(1)
    nwt = WINDOW // QT
    blk_per_tile = QT // SEL_BLOCK
    jk = lax.broadcasted_iota(jnp.int32, (QT, QT), 0)
    iq = lax.broadcasted_iota(jnp.int32, (QT, QT), 1)
    causal_t = jk <= iq
    q_t = q_ref[...].T.astype(BF16)
    gates_t = brg_ref[...].T

    def sel_keep(g, kt, tiles=1):
        rows = [jnp.broadcast_to(sel_ref[0, g, pl.ds(blk_per_tile * kt + b, 1), :], (SEL_BLOCK, QT))
                for b in range(blk_per_tile * tiles)]
        return jnp.concatenate(rows, axis=0) > 0.5

    def key_tile(ref, g, kt):
        return ref[0, g, pl.ds(pl.multiple_of(kt * QT, QT), QT), :]

    groups = range(N_KV_HEADS)
    heads = [[g * Q_PER_KV + r for r in range(Q_PER_KV)] for g in groups]
    qg = [jnp.concatenate([q_t[h * NSA_HEAD_DIM:(h + 1) * NSA_HEAD_DIM, :] for h in heads[g]], axis=1) for g in groups]
    far = [jnp.concatenate([jnp.full((1, QT), c31_ref[h], F32) for h in heads[g]], axis=1) for g in groups]
    near0 = [[b0_ref[h] for h in heads[g]] for g in groups]
    near1 = [[b1_ref[h] for h in heads[g]] for g in groups]
    scores = lambda ref, g, kt: jnp.dot(key_tile(ref, g, kt), qg[g], preferred_element_type=F32)
    prev = jnp.maximum(qi - 1, 0)
    has_prev = jnp.broadcast_to(qi >= 1, (QT, QT))

    none4 = [None] * N_KV_HEADS
    st = _attn_round(
        none4 + none4,
        [scores(ks_ref, g, qi) for g in groups] + [scores(kw_ref, g, qi) for g in groups],
        [causal_t & sel_keep(g, qi) for g in groups] + [causal_t] * N_KV_HEADS, near0 + near0,
        [vs_ref[0, g, qi] for g in groups] + [vw_ref[0, g, qi] for g in groups], none4 + none4)
    st = _attn_round(
        st, [scores(ks_ref, g, prev) for g in groups] + [scores(kw_ref, g, prev) for g in groups],
        [sel_keep(g, prev) & has_prev for g in groups] + [has_prev] * N_KV_HEADS, near1 + near1,
        [vs_ref[0, g, prev] for g in groups] + [vw_ref[0, g, prev] for g in groups], none4 + none4)
    sel, win = st[:N_KV_HEADS], st[N_KV_HEADS:]
    for back in range(2, nwt + 1):
        kt = jnp.maximum(qi - back, 0)
        keep = jnp.broadcast_to(qi >= back, (QT, QT))
        if back == nwt:
            keep = keep & (jk > iq)
        win = _attn_round(win, [scores(kw_ref, g, kt) for g in groups], [keep] * N_KV_HEADS, none4,
                          [vw_ref[0, g, kt] for g in groups], far)

    def sel_body(kt, state):
        return tuple(_attn_round(list(state), [scores(ks_ref, g, kt) for g in groups],
                                 [sel_keep(g, kt) for g in groups], none4, [vs_ref[0, g, kt] for g in groups], far))

    def sel_body2(j, state):
        kt = 2 * j
        raw = [jnp.dot(ks_ref[0, g, pl.ds(pl.multiple_of(kt * QT, QT), 2 * QT), :], qg[g],
                       preferred_element_type=F32) for g in groups]
        keeps = [sel_keep(g, kt, 2) for g in groups]
        vts = [jnp.concatenate([vs_ref[0, g, kt], vs_ref[0, g, kt + 1]], axis=1) for g in groups]
        return tuple(_attn_round(list(state), raw, keeps, none4, vts, far))

    n_far = jnp.maximum(qi - 1, 0)
    sel = lax.fori_loop(0, n_far // 2, sel_body2, tuple(sel))
    sel = lax.fori_loop(2 * (n_far // 2), n_far, sel_body, sel)

    o_t = []
    dh = NSA_HEAD_DIM
    for g in groups:
        o_s = sel[g][1][0:dh, :] * (1.0 / sel[g][1][dh:dh + 1, :])
        o_w = win[g][1][0:dh, :] * (1.0 / win[g][1][dh:dh + 1, :])
        for r, h in enumerate(heads[g]):
            sl = slice(r * QT, (r + 1) * QT)
            o_t.append(gates_t[3 * h + 1:3 * h + 2, :] * o_s[:, sl] + gates_t[3 * h + 2:3 * h + 3, :] * o_w[:, sl])

    pairs = [jnp.concatenate(o_t[2 * k:2 * k + 2], axis=0).T for k in range(len(o_t) // 2)]
    y_nsa = jnp.concatenate(pairs, axis=1) + ocmp_ref[...]
    u = mssm_ref[...] + sgn_ref[...] * _bdot(y_nsa, wn_ref[...])
    h_ref[...] = x_ref[...] + _bdot(u, wo_ref[...])


def _nsa_main(c31, q, brg, sel, ocmp, ks, vs_t, kw, vw_t, b0, b1, mssm, sgn, x, wn, wo, batch):
    n, qw = q.shape
    d = x.shape[1]
    s = n // batch
    nq = s // QT
    rows = lambda b, i: (b * nq + i, 0)
    kspec = pl.BlockSpec((1,) + ks.shape[1:], lambda b, i: (b, 0, 0, 0))
    vspec = pl.BlockSpec((1,) + vs_t.shape[1:], lambda b, i: (b, 0, 0, 0, 0))
    c2 = lambda b, i: (0, 0)
    c3 = lambda b, i: (0, 0, 0)
    return pl.pallas_call(
        _nsa_main_kernel, grid=(batch, nq),
        in_specs=[pl.BlockSpec(memory_space=pltpu.SMEM),
                  pl.BlockSpec((QT, qw), rows), pl.BlockSpec((QT, LANES), rows),
                  pl.BlockSpec((1,) + sel.shape[1:3] + (QT,), lambda b, i: (b, 0, 0, i)),
                  pl.BlockSpec((QT, qw), rows), kspec, vspec, kspec, vspec,
                  pl.BlockSpec(b0.shape, c3), pl.BlockSpec(b1.shape, c3),
                  pl.BlockSpec((QT, d), rows), pl.BlockSpec((QT, d), rows), pl.BlockSpec((QT, d), rows),
                  pl.BlockSpec(wn.shape, c2), pl.BlockSpec(wo.shape, c2)],
        out_specs=pl.BlockSpec((QT, d), rows), out_shape=jax.ShapeDtypeStruct((n, d), F32),
        compiler_params=_cparams(("parallel", "parallel")), name="nsa_main",
    )(c31, q, brg, sel, ocmp, ks, vs_t, kw, vw_t, b0, b1, mssm, sgn, x, wn, wo)


META_EXPERT, META_RANK, META_GATE = 0, TOP_K, 2 * TOP_K


def _pack_bf16_pair(x):
    c = x.shape[1] // 2
    hi = pltpu.bitcast(x[:, :c].astype(BF16).astype(F32), jnp.int32)
    lo = pltpu.bitcast(x[:, c:].astype(BF16).astype(F32), jnp.int32)
    return hi | lax.shift_right_logical(lo, 16)


def _unpack_bf16_pair(w):
    hi = pltpu.bitcast(w & jnp.int32(-65536), F32).astype(BF16)
    lo = pltpu.bitcast(lax.shift_left(w, 16), F32).astype(BF16)
    return hi, lo


def _router_kernel(h_ref, nw_ref, wr_ref, br_ref, tri_ref, hn_ref, meta_ref, cnt_ref, base_s, *, n_real):
    i = pl.program_id(0)
    tm = h_ref.shape[0]

    @pl.when(i == 0)
    def _():
        base_s[...] = jnp.zeros_like(base_s)

    hn = _rms_rows(h_ref[...], nw_ref[...])
    hn_ref[...] = _pack_bf16_pair(hn)
    a_hi = hn.astype(BF16)
    a_lo = (hn - a_hi.astype(F32)).astype(BF16)
    w = wr_ref[...]
    w_hi = w.astype(BF16)
    w_lo = (w - w_hi.astype(F32)).astype(BF16)
    d = lambda a, b: jnp.dot(a, b, preferred_element_type=F32)
    v = (d(a_hi, w_hi) + (d(a_hi, w_lo) + d(a_lo, w_hi))) + br_ref[...]
    lane = lax.broadcasted_iota(jnp.int32, v.shape, 1)
    tops, idxs, hots = [], [], []
    for _ in range(TOP_K):
        m = jnp.max(v, axis=-1, keepdims=True)
        idx = jnp.min(jnp.where(v == m, lane, LANES), axis=-1, keepdims=True)
        hot = lane == idx
        tops.append(m)
        idxs.append(idx)
        hots.append(hot)
        v = jnp.where(hot, NEG, v)
    es = [jnp.exp(t - tops[0]) for t in tops]
    inv = 1.0 / functools.reduce(lambda a, b: a + b, es)
    row = i * tm + lax.broadcasted_iota(jnp.int32, v.shape, 0)
    onehot = jnp.zeros(v.shape, F32)
    for hot in hots:
        onehot = onehot + jnp.where(hot & (row < n_real), 1.0, 0.0)
    before = jnp.dot(tri_ref[...], onehot.astype(BF16), preferred_element_type=F32) + base_s[0:1, :]
    meta = jnp.zeros(v.shape, F32)
    for k in range(TOP_K):
        rank = jnp.sum(jnp.where(hots[k], before, 0.0), axis=-1, keepdims=True)
        meta = meta + jnp.where(lane == META_EXPERT + k, idxs[k].astype(F32), 0.0)
        meta = meta + jnp.where(lane == META_RANK + k, rank, 0.0)
        meta = meta + jnp.where(lane == META_GATE + k, es[k] * inv, 0.0)
    meta_ref[...] = meta
    base_s[0:1, :] = base_s[0:1, :] + jnp.sum(onehot, axis=0, keepdims=True)
    cnt_ref[...] = jnp.broadcast_to(base_s[0:1, :], cnt_ref.shape)


def _router(h, nw, wr, br, tm, n_real):
    n, d = h.shape
    c2 = lambda i: (0, 0)
    tri = jnp.asarray(np.tril(np.ones((tm, tm), np.float32), -1), BF16)
    return pl.pallas_call(
        functools.partial(_router_kernel, n_real=n_real), grid=(n // tm,),
        in_specs=[pl.BlockSpec((tm, d), lambda i: (i, 0)), pl.BlockSpec(nw.shape, c2),
                  pl.BlockSpec(wr.shape, c2), pl.BlockSpec(br.shape, c2), pl.BlockSpec(tri.shape, c2)],
        out_specs=[pl.BlockSpec((tm, d // 2), lambda i: (i, 0)), pl.BlockSpec((tm, LANES), lambda i: (i, 0)),
                   pl.BlockSpec((SUBLANES, LANES), c2)],
        out_shape=[jax.ShapeDtypeStruct((n, d // 2), jnp.int32), jax.ShapeDtypeStruct((n, LANES), F32),
                   jax.ShapeDtypeStruct((SUBLANES, LANES), F32)],
        scratch_shapes=[pltpu.VMEM((SUBLANES, LANES), F32)],
        compiler_params=_cparams(("arbitrary",)), name="moe_router",
    )(h, nw, wr, br, tri)


def _moe_group_kernel(sp_ref, x_ref, w1_ref, b1_ref, w2_ref, b2_ref, y_ref):
    t = pl.program_id(0)
    n_active = sp_ref[pl.num_programs(0)]

    @pl.when(t < n_active)
    def _():
        d_ff = w2_ref.shape[1]
        x_a, x_b = _unpack_bf16_pair(x_ref[...])
        half = x_a.shape[1]
        y1 = (jnp.dot(x_a, w1_ref[0, :half, :], preferred_element_type=F32)
              + jnp.dot(x_b, w1_ref[0, half:, :], preferred_element_type=F32)) + b1_ref[0]
        gt = jnp.minimum(y1[:, :d_ff], SWIGLU_LIMIT)
        up = jnp.clip(y1[:, d_ff:], -SWIGLU_LIMIT, SWIGLU_LIMIT)
        act = (up + 1.0) * gt * _sigmoid(SWIGLU_ALPHA * gt)
        y_ref[...] = _bdot(act, w2_ref[0]) + b2_ref[0]

    @pl.when(t >= n_active)
    def _():
        y_ref[...] = jnp.zeros_like(y_ref)


def _moe_group(sp, x_sorted, w1, b1, w2, b2, tm):
    p, dx = x_sorted.shape
    d = w2.shape[2]
    rows = lambda t, sp: (t, 0)
    per_e = lambda t, sp: (sp[t], 0, 0)
    grid_spec = pltpu.PrefetchScalarGridSpec(
        num_scalar_prefetch=1, grid=(p // tm,),
        in_specs=[pl.BlockSpec((tm, dx), rows),
                  pl.BlockSpec((1,) + w1.shape[1:], per_e), pl.BlockSpec((1,) + b1.shape[1:], per_e),
                  pl.BlockSpec((1,) + w2.shape[1:], per_e), pl.BlockSpec((1,) + b2.shape[1:], per_e)],
        out_specs=pl.BlockSpec((tm, d), rows))
    return pl.pallas_call(
        _moe_group_kernel, grid_spec=grid_spec, out_shape=jax.ShapeDtypeStruct((p, d), F32),
        compiler_params=_cparams(("arbitrary",)), name="moe_experts",
    )(sp, x_sorted, w1, b1, w2, b2)


def _moe_combine_kernel(h_ref, yk_ref, meta_ref, o_ref):
    d = h_ref.shape[1]
    meta = meta_ref[...]
    kk = lax.broadcasted_iota(jnp.int32, (LANES, LANES), 0)
    acc = h_ref[...]
    for k in range(TOP_K):
        pick = jnp.where(kk == META_GATE + k, 1.0, 0.0).astype(BF16)
        g = _dot3(meta, pick)
        acc = acc + jnp.concatenate([g] * (d // LANES), axis=1) * yk_ref[k]
    o_ref[...] = acc


def _moe_combine(h, yk, meta, tm):
    n, d = h.shape
    rows = lambda i: (i, 0)
    return pl.pallas_call(
        _moe_combine_kernel, grid=(n // tm,),
        in_specs=[pl.BlockSpec((tm, d), rows), pl.BlockSpec((TOP_K, tm, d), lambda i: (0, i, 0)),
                  pl.BlockSpec((tm, LANES), rows)],
        out_specs=pl.BlockSpec((tm, d), rows), out_shape=jax.ShapeDtypeStruct((n, d), F32),
        compiler_params=_cparams(("parallel",)), name="moe_combine",
    )(h, yk, meta)


def _mamba_prep_kernel(xbc_ref, sconv_ref, dt_ref, cw_ref, cb_ref, hp_ref, e_ref,
                       conv_ref, xs_ref, bm_ref, ct_ref, dtx_ref, dec_ref):
    cdim = xbc_ref.shape[1]
    d_inner = xs_ref.shape[1]
    xb = xbc_ref[...]
    u = cb_ref[...] + cw_ref[D_CONV - 1:D_CONV, :] * xb
    for k in range(D_CONV - 1):
        u = u + cw_ref[k:k + 1, :] * sconv_ref[:, k * cdim:(k + 1) * cdim]
    conv_ref[:, :(D_CONV - 2) * cdim] = sconv_ref[:, cdim:]
    conv_ref[:, (D_CONV - 2) * cdim:] = xb
    u = u * _sigmoid(u)
    xs = u[:, :d_inner]
    xs_ref[...] = xs
    bm_ref[...] = u[:, d_inner:d_inner + N_SSM_GROUPS * D_STATE]
    cm = u[:, d_inner + N_SSM_GROUPS * D_STATE:]
    for g in range(N_SSM_GROUPS):
        ct_ref[g] = cm[:, g * D_STATE:(g + 1) * D_STATE].T
    dtv = _softplus(dt_ref[...] + hp_ref[0:1, :])
    dec_ref[...] = jnp.exp(dtv * (-jnp.exp(hp_ref[1:2, :])))
    dtx_ref[...] = (xs * _dot3(dtv, e_ref[...])).T


def _mamba_prep(xbc, sconv, dt, conv_w, conv_b, headp, expand):
    n, cdim = xbc.shape
    d_inner = expand.shape[1]
    gn = N_SSM_GROUPS * D_STATE
    out_shape = [jax.ShapeDtypeStruct(sconv.shape, F32), jax.ShapeDtypeStruct((n, d_inner), F32),
                 jax.ShapeDtypeStruct((n, gn), F32), jax.ShapeDtypeStruct((N_SSM_GROUPS, D_STATE, n), F32),
                 jax.ShapeDtypeStruct((d_inner, n), F32), jax.ShapeDtypeStruct((n, LANES), F32)]
    return pl.pallas_call(_mamba_prep_kernel, out_shape=out_shape,
                          compiler_params=pltpu.CompilerParams(vmem_limit_bytes=VMEM_LIMIT), name="mamba_prep",
                          )(xbc, sconv, dt, conv_w, conv_b, headp, expand)


def _bf16x3(a, b):
    a_hi = a.astype(BF16)
    a_lo = (a - a_hi.astype(F32)).astype(BF16)
    b_hi = b.astype(BF16)
    b_lo = (b - b_hi.astype(F32)).astype(BF16)
    d = lambda x, y: jnp.dot(x, y, preferred_element_type=F32)
    return d(a_hi, b_hi) + (d(a_hi, b_lo) + d(a_lo, b_hi))


def _mamba_state_kernel(dec_ref, h0_ref, dtx_ref, bm_ref, ct_ref, hn_ref, yt_ref):
    s = pl.program_id(0)
    n = bm_ref.shape[0]
    d_inner = dtx_ref.shape[0]
    gw = d_inner // N_SSM_GROUPS
    hpg = gw // SSM_HEAD_DIM

    @pl.when(s == 0)
    def _():
        yt_ref[...] = jnp.zeros_like(yt_ref)

    row_is_s = lax.broadcasted_iota(jnp.int32, (n, D_STATE), 0) == s
    col_is_s = lax.broadcasted_iota(jnp.int32, (D_STATE, n), 1) == s
    for g in range(N_SSM_GROUPS):
        b_s = jnp.where(row_is_s, bm_ref[:, g * D_STATE:(g + 1) * D_STATE], 0.0)
        st = _bf16x3(dtx_ref[g * gw:(g + 1) * gw, :], b_s)
        parts = []
        for r in range(hpg):
            h = g * hpg + r
            rows = slice(h * SSM_HEAD_DIM, (h + 1) * SSM_HEAD_DIM)
            parts.append(h0_ref[0, rows, :] * dec_ref[s, h] + st[r * SSM_HEAD_DIM:(r + 1) * SSM_HEAD_DIM, :])
        hn = jnp.concatenate(parts, axis=0)
        hn_ref[0, g * gw:(g + 1) * gw, :] = hn
        c_s = jnp.where(col_is_s, ct_ref[g], 0.0)
        yt_ref[g * gw:(g + 1) * gw, :] += _bdot(hn, c_s)


def _mamba_state(dec, h0, dtx_t, bm, ct):
    n, rows, ns = h0.shape
    c2 = lambda s: (0, 0)
    return pl.pallas_call(
        _mamba_state_kernel, grid=(n,),
        in_specs=[pl.BlockSpec(memory_space=pltpu.SMEM),
                  pl.BlockSpec((1, rows, ns), lambda s: (s, 0, 0)), pl.BlockSpec(dtx_t.shape, c2),
                  pl.BlockSpec(bm.shape, c2), pl.BlockSpec(ct.shape, lambda s: (0, 0, 0))],
        out_specs=[pl.BlockSpec((1, rows, ns), lambda s: (s, 0, 0)), pl.BlockSpec(dtx_t.shape, c2)],
        out_shape=[jax.ShapeDtypeStruct(h0.shape, F32), jax.ShapeDtypeStruct(dtx_t.shape, F32)],
        compiler_params=_cparams(("arbitrary",)), name="mamba_state",
    )(dec, h0, dtx_t, bm, ct)


def _page_cmp_kernel(pt_ref, cache_ref, new_ref, w_ref, pec_ref, deint_ref, ones_ref, nw_ref, out_ref, buf, rows_s, sem):
    t = pl.program_id(0)
    nsteps = pl.num_programs(0)
    n_pages = pt_ref.shape[1]
    page = buf.shape[4]
    nstr = out_ref.shape[2]
    half = out_ref.shape[3]
    nch = half // LANES
    spp = page // CMP_STRIDE
    past_str = n_pages * spp

    def copies(step, slot):
        seq, kv = step // 2, step % 2
        return [pltpu.make_async_copy(cache_ref.at[pt_ref[seq, p], pl.ds(nch * kv, nch)], buf.at[slot, p], sem.at[slot])
                for p in range(n_pages)]

    @pl.when(t == 0)
    def _():
        rows_s[:, past_str:, :] = jnp.zeros((CMP_STRIDE, nstr - past_str, half), F32)
        for cp in copies(t, 0):
            cp.start()

    slot = t % 2

    @pl.when(t + 1 < nsteps)
    def _():
        for cp in copies(t + 1, 1 - slot):
            cp.start()

    for cp in copies(t, slot):
        cp.wait()

    kv = t % 2
    deint = deint_ref[...]

    pages_per_trip = math.gcd(n_pages, 8)

    def to_rows(trip, carry):
        for pp in range(pages_per_trip):
            p = trip * pages_per_trip + pp
            xr = _bdot_nt(deint, buf[slot, p].reshape(nch * LANES, page))
            for i in range(CMP_STRIDE):
                rows_s[i, pl.ds(pl.multiple_of(p * spp, spp), spp), :] = xr[i * spp:(i + 1) * spp, :]
        return carry

    lax.fori_loop(0, n_pages // pages_per_trip, to_rows, 0)
    new = new_ref[0]
    rows_s[0, past_str:past_str + 1, :] = jnp.where(kv == 0, new[:, :half], new[:, half:])

    lo = hi = None
    for i in range(CMP_STRIDE):
        x = rows_s[i].astype(BF16)
        a = jnp.dot(x, w_ref[0, i], preferred_element_type=F32)
        b = jnp.dot(x, w_ref[0, CMP_STRIDE + i], preferred_element_type=F32)
        lo = a if lo is None else lo + a
        hi = b if hi is None else hi + b
    lo = lo + pec_ref[0, 0:1, :]
    hi = hi + pec_ref[0, 1:2, :]
    tok = lo + pltpu.roll(hi, nstr - 1, axis=0)
    out_ref[0, 0] = jnp.where(kv == 0, _head_rms(tok, ones_ref[...], nw_ref[...], NSA_HEAD_DIM), tok)


def _cmp_const_kernel(pe_ref, w_ref, out_ref):
    for kv in range(2):
        lo = hi = None
        for i in range(CMP_STRIDE):
            j = CMP_STRIDE + i
            a = _dot3(_rows8(pe_ref[kv, i:i + 1, :]), w_ref[kv, i])
            b = _dot3(_rows8(pe_ref[kv, j:j + 1, :]), w_ref[kv, j])
            lo = a if lo is None else lo + a
            hi = b if hi is None else hi + b
        out_ref[kv] = jnp.concatenate([lo[0:1, :], hi[0:1, :], jnp.zeros((SUBLANES - 2, lo.shape[1]), F32)], axis=0)


def _cmp_const(pe_kv, w_kv):
    return pl.pallas_call(_cmp_const_kernel, out_shape=jax.ShapeDtypeStruct((2, SUBLANES, pe_kv.shape[2]), F32),
                          compiler_params=pltpu.CompilerParams(vmem_limit_bytes=VMEM_LIMIT), name="cmp_const",
                          )(pe_kv, w_kv)


def _page_cmp(page_table, cache_t, new_rows, w_kv, pe_const, ones64, nw, nstr):
    nseq, n_pages = page_table.shape
    page = cache_t.shape[3]
    half = w_kv.shape[2]
    nch = half // LANES
    spp = page // CMP_STRIDE
    deint = np.zeros((page, page), np.float32)
    for i in range(CMP_STRIDE):
        for k in range(spp):
            deint[i * spp + k, CMP_STRIDE * k + i] = 1.0
    deint = jnp.asarray(deint, BF16)
    c2 = lambda t, pt: (0, 0)
    grid_spec = pltpu.PrefetchScalarGridSpec(
        num_scalar_prefetch=1, grid=(2 * nseq,),
        in_specs=[pl.BlockSpec(memory_space=pl.ANY),
                  pl.BlockSpec((1, 1, 2 * half), lambda t, pt: (t // 2, 0, 0)),
                  pl.BlockSpec((1,) + w_kv.shape[1:], lambda t, pt: (t % 2, 0, 0, 0)),
                  pl.BlockSpec((1,) + pe_const.shape[1:], lambda t, pt: (t % 2, 0, 0)),
                  pl.BlockSpec(deint.shape, c2), pl.BlockSpec(ones64.shape, c2), pl.BlockSpec(nw.shape, c2)],
        out_specs=pl.BlockSpec((1, 1, nstr, half), lambda t, pt: (t // 2, t % 2, 0, 0)),
        scratch_shapes=[pltpu.VMEM((2, n_pages, nch, LANES, page), F32),
                        pltpu.VMEM((CMP_STRIDE, nstr, half), F32),
                        pltpu.SemaphoreType.DMA((2,))])
    return pl.pallas_call(
        _page_cmp_kernel, grid_spec=grid_spec, out_shape=jax.ShapeDtypeStruct((nseq, 2, nstr, half), F32),
        compiler_params=_cparams(("arbitrary",)), name="page_cmp",
    )(page_table, cache_t, new_rows, w_kv, pe_const, deint, ones64, nw)


def _rows8(x):
    return jnp.broadcast_to(x, (SUBLANES, x.shape[1]))


def _group_q(q_row, g):
    parts = [q_row[:, (g * Q_PER_KV + r) * NSA_HEAD_DIM:(g * Q_PER_KV + r + 1) * NSA_HEAD_DIM]
             for r in range(Q_PER_KV)]
    parts.append(jnp.zeros((SUBLANES - Q_PER_KV, NSA_HEAD_DIM), F32))
    return jnp.concatenate(parts, axis=0)


def _heads_to_row(o):
    return jnp.concatenate([o[r:r + 1, :] for r in range(Q_PER_KV)], axis=1)


def _softmax_rows(s, keep):
    s = jnp.where(keep, s, NEG)
    m = jnp.max(s, axis=-1, keepdims=True)
    p = jnp.where(keep, jnp.exp2(s - m), 0.0)
    l = jnp.sum(p, axis=-1, keepdims=True)
    return p * jnp.where(l > 0.0, 1.0 / l, 0.0)


def _sample_cw_kernel(q_ref, brg_ref, kvc_ref, win_ref, wnew_ref, tbc_ref, tbw_ref, gexp_ref,
                      o_ref, sel_ref, wout_ref, wall_s, *, tq, past_w):
    nstr = kvc_ref.shape[2]
    half = kvc_ref.shape[3]
    wlen = win_ref.shape[1]
    nsel_pad = sel_ref.shape[2]
    wrows = wall_s.shape[0]
    q_row = q_ref[0]
    wall_s[0:wlen, :] = win_ref[0]
    wall_s[wlen:wlen + 1, :] = wnew_ref[0]
    wall_s[wlen + 1:, :] = jnp.zeros((wrows - wlen - 1, wall_s.shape[1]), F32)
    wout_ref[0] = wall_s[1:wlen + 1, :]

    nn = lax.broadcasted_iota(jnp.int32, (SUBLANES, nstr), 1)
    keep_c = (CMP_STRIDE * nn + (CMP_BLOCK - 1)) <= tq
    wi = lax.broadcasted_iota(jnp.int32, (SUBLANES, wrows), 1)
    dw = tq - (past_w + wi)
    keep_w = (dw >= 0) & (dw < WINDOW) & (past_w + wi >= 0) & (wi <= wlen)
    jb = lax.broadcasted_iota(jnp.int32, (nsel_pad, nstr), 0) * SEL_BLOCK
    cs = lax.broadcasted_iota(jnp.int32, (nsel_pad, nstr), 1) * CMP_STRIDE
    cover_t = jnp.where((cs < jb + SEL_BLOCK) & (cs + CMP_BLOCK > jb), 1.0, 0.0).astype(BF16)
    ji = lax.broadcasted_iota(jnp.int32, (nsel_pad, LANES), 0)
    valid = ji * SEL_BLOCK <= tq
    cur = tq // SEL_BLOCK
    forced = valid & ((ji == 0) | (ji == cur) | (ji == cur - 1))
    ii = lax.broadcasted_iota(jnp.int32, (nsel_pad, nsel_pad), 0)
    jj = lax.broadcasted_iota(jnp.int32, (nsel_pad, nsel_pad), 1)

    sel_ref[0] = jnp.zeros(sel_ref.shape[1:], F32)
    groups = range(N_KV_HEADS)
    k_lanes = [slice(g * NSA_HEAD_DIM, (g + 1) * NSA_HEAD_DIM) for g in groups]
    v_lanes = [slice(half + g * NSA_HEAD_DIM, half + (g + 1) * NSA_HEAD_DIM) for g in groups]
    qgs = [_group_q(q_row, g) for g in groups]
    sc = [_bdot_nt(qgs[g], kvc_ref[0, 0, :, k_lanes[g]]) + tbc_ref[g] for g in groups]
    sw = [_bdot_nt(qgs[g], wall_s[:, k_lanes[g]]) + tbw_ref[g] for g in groups]
    pc = [_softmax_rows(s, keep_c) for s in sc]
    pw = [_softmax_rows(s, keep_w) for s in sw]
    oc = [_heads_to_row(_bdot(pc[g], kvc_ref[0, 1, :, k_lanes[g]])) for g in groups]
    ow = [_heads_to_row(_bdot(pw[g], wall_s[:, v_lanes[g]])) for g in groups]
    imps = [_dot3_nt_l(cover_t, jnp.broadcast_to(jnp.sum(pc[g][0:Q_PER_KV, :], axis=0, keepdims=True), (LANES, nstr)))
            for g in groups]
    for g in groups:
        v_col = jnp.where(forced, FORCE_SCORE, jnp.where(valid, imps[g], NEG))
        v_row = jnp.concatenate([v_col[k * LANES:(k + 1) * LANES, :].T for k in range(nsel_pad // LANES)], axis=1)
        a = jnp.broadcast_to(v_row[0:1, :], (nsel_pad, nsel_pad))
        b = jnp.concatenate([v_col] * (nsel_pad // LANES), axis=1)
        beats = jnp.where(ii < jj, jnp.where(b >= a, 1.0, 0.0), jnp.where(b > a, 1.0, 0.0))
        cnt = jnp.sum(beats, axis=0, keepdims=True)
        sel_ref[0, g:g + 1, :] = jnp.where(cnt < N_SEL_BLOCKS, 1.0, 0.0)
    gates = _rows8(brg_ref[0])
    o = (jnp.concatenate(oc, axis=1) * _dot3(gates, gexp_ref[0])[0:1, :]
         + jnp.concatenate(ow, axis=1) * _dot3(gates, gexp_ref[2])[0:1, :])
    o_ref[0] = o


def _sample_cw(q, brg, kvc, win, wnew, tbc, tbw, gexp, tq, past_w, nsel_pad):
    nseq = q.shape[0]
    qw = q.shape[2]
    wlen, ww = win.shape[1], win.shape[2]
    wrows = -(-(wlen + 1) // SUBLANES) * SUBLANES
    per3 = lambda s: (s, 0, 0)
    c3 = lambda s: (0, 0, 0)
    return pl.pallas_call(
        functools.partial(_sample_cw_kernel, tq=tq, past_w=past_w), grid=(nseq,),
        in_specs=[pl.BlockSpec((1, 1, qw), per3), pl.BlockSpec((1, 1, LANES), per3),
                  pl.BlockSpec((1,) + kvc.shape[1:], lambda s: (s, 0, 0, 0)),
                  pl.BlockSpec((1, wlen, ww), per3), pl.BlockSpec((1, 1, ww), per3),
                  pl.BlockSpec(tbc.shape, c3), pl.BlockSpec(tbw.shape, c3), pl.BlockSpec(gexp.shape, c3)],
        out_specs=[pl.BlockSpec((1, 1, qw), per3), pl.BlockSpec((1, SUBLANES, nsel_pad), per3),
                   pl.BlockSpec((1, wlen, ww), per3)],
        out_shape=[jax.ShapeDtypeStruct((nseq, 1, qw), F32), jax.ShapeDtypeStruct((nseq, SUBLANES, nsel_pad), F32),
                   jax.ShapeDtypeStruct((nseq, wlen, ww), F32)],
        scratch_shapes=[pltpu.VMEM((wrows, ww), F32)],
        compiler_params=_cparams(("parallel",)), name="sample_cmp_win",
    )(q, brg, kvc, win, wnew, tbc, tbw, gexp)


def _sample_sel_kernel(pg_ref, c31_ref, f0_ref, cache_ref, q_ref, brg_ref, snew_ref, code_ref, ocw_ref, tbl_ref, gexp_ref,
                       o_ref, buf, sem):
    s = pl.program_id(0)
    nseq = pl.num_programs(0)
    nblk = pg_ref.shape[1] // N_KV_HEADS
    page = cache_ref.shape[4]
    half = snew_ref.shape[2] // 2

    def copies(seq, slot):
        out = []
        for g in range(N_KV_HEADS):
            for k in range(nblk):
                src = cache_ref.at[pg_ref[seq, g * nblk + k]]
                for kv in range(2):
                    out.append(pltpu.make_async_copy(src.at[kv, g], buf.at[slot, kv, g, :, pl.ds(k * page, page)],
                                                     sem.at[slot]))
        return out

    @pl.when(s == 0)
    def _():
        for cp in copies(s, 0):
            cp.start()

    slot = s % 2

    @pl.when(s + 1 < nseq)
    def _():
        for cp in copies(s + 1, 1 - slot):
            cp.start()

    for cp in copies(s, slot):
        cp.wait()

    q_row = q_ref[0]
    new = snew_ref[0]
    outs = []
    for g in range(N_KV_HEADS):
        qg = _group_q(q_row, g)
        heads = [g * Q_PER_KV + r for r in range(Q_PER_KV)]
        code = _rows8(code_ref[0, g:g + 1, :])
        far = jnp.concatenate([jnp.full((1, 1), c31_ref[h], F32) for h in heads]
                              + [jnp.zeros((SUBLANES - Q_PER_KV, 1), F32)], axis=0)
        near = jnp.concatenate([tbl_ref[g]] * nblk, axis=1)
        sc = jnp.dot(qg.astype(BF16), buf[slot, 0, g].astype(BF16), preferred_element_type=F32)
        sc = jnp.where(code > 0.5, sc + jnp.where(code > 1.5, near, far), NEG)
        k_new = new[:, g * NSA_HEAD_DIM:(g + 1) * NSA_HEAD_DIM]
        v_new = new[:, half + g * NSA_HEAD_DIM:half + (g + 1) * NSA_HEAD_DIM]
        f0 = jnp.concatenate([jnp.full((1, 1), f0_ref[h], F32) for h in heads]
                             + [jnp.zeros((SUBLANES - Q_PER_KV, 1), F32)], axis=0)
        s_new = jnp.sum(qg * _rows8(k_new), axis=-1, keepdims=True) + f0
        new_on = _rows8(code_ref[0, N_KV_HEADS + g:N_KV_HEADS + g + 1, 0:1]) > 0.5
        s_new = jnp.where(new_on, s_new, NEG)
        m = jnp.maximum(jnp.max(sc, axis=-1, keepdims=True), s_new)
        p = jnp.where(code > 0.5, jnp.exp2(sc - m), 0.0)
        p_new = jnp.where(new_on, jnp.exp2(s_new - m), 0.0)
        l = jnp.sum(p, axis=-1, keepdims=True) + p_new
        inv = jnp.where(l > 0.0, 1.0 / l, 0.0)
        o = _bdot_nt(p, buf[slot, 1, g]) + p_new * _rows8(v_new)
        outs.append(_heads_to_row(o * inv))
    gates = _dot3(_rows8(brg_ref[0]), gexp_ref[1])[0:1, :]
    o_ref[0] = ocw_ref[0] + jnp.concatenate(outs, axis=1) * gates


def _sample_sel(pages, c31, f0, cache_t, q, brg, snew, code, ocw, tbl, gexp):
    nseq, qw = q.shape[0], q.shape[2]
    nblk = pages.shape[1] // N_KV_HEADS
    page = cache_t.shape[4]
    per3 = lambda s, *_: (s, 0, 0)
    c3 = lambda s, *_: (0, 0, 0)
    grid_spec = pltpu.PrefetchScalarGridSpec(
        num_scalar_prefetch=1, grid=(nseq,),
        in_specs=[pl.BlockSpec(memory_space=pltpu.SMEM), pl.BlockSpec(memory_space=pltpu.SMEM),
                  pl.BlockSpec(memory_space=pl.ANY),
                  pl.BlockSpec((1, 1, qw), per3), pl.BlockSpec((1, 1, LANES), per3),
                  pl.BlockSpec((1, 1, snew.shape[2]), per3), pl.BlockSpec((1,) + code.shape[1:], per3),
                  pl.BlockSpec((1, 1, qw), per3), pl.BlockSpec(tbl.shape, c3), pl.BlockSpec(gexp.shape, c3)],
        out_specs=pl.BlockSpec((1, 1, qw), per3),
        scratch_shapes=[pltpu.VMEM((2, 2, N_KV_HEADS, NSA_HEAD_DIM, nblk * page), F32),
                        pltpu.SemaphoreType.DMA((2,))])
    return pl.pallas_call(
        _sample_sel_kernel, grid_spec=grid_spec, out_shape=jax.ShapeDtypeStruct((nseq, 1, qw), F32),
        compiler_params=_cparams(("arbitrary",)), name="sample_sel",
    )(pages, c31, f0, cache_t, q, brg, snew, code, ocw, tbl, gexp)


def _sample_merge_kernel(yt_ref, xs_ref, z_ref, sgs_ref, dsk_ref, nw_ref, onesg_ref, ws_ref, ynsa_ref, sgn_ref, x_ref,
                         wn_ref, wo_ref, h_ref):
    gw = onesg_ref.shape[0]
    y = yt_ref[...].T + dsk_ref[...] * xs_ref[...]
    zz = z_ref[...]
    y = y * (zz * _sigmoid(zz))
    ms = _seg_sum(y * y, onesg_ref[...]) * (1.0 / gw)
    y = (y * lax.rsqrt(ms + RMS_EPS)) * nw_ref[...]
    u = sgs_ref[...] * _bdot(y, ws_ref[...]) + sgn_ref[...] * _bdot(ynsa_ref[...], wn_ref[...])
    h_ref[...] = x_ref[...] + _bdot(u, wo_ref[...])


def _sample_merge(y_t, xs, z, sgs, dsk, nw, onesg, ws, ynsa, sgn, x, wn, wo):
    return pl.pallas_call(_sample_merge_kernel, out_shape=jax.ShapeDtypeStruct(x.shape, F32),
                          compiler_params=pltpu.CompilerParams(vmem_limit_bytes=VMEM_LIMIT), name="sample_merge",
                          )(y_t, xs, z, sgs, dsk, nw, onesg, ws, ynsa, sgn, x, wn, wo)


def _bucket_lut():
    n = np.arange(MAX_DISTANCE + 1)
    max_exact = N_BUCKETS // 2
    nf = np.maximum(n, 1).astype(np.float32)
    large = max_exact + (np.log(nf / max_exact) / math.log(MAX_DISTANCE / max_exact)
                         * (N_BUCKETS - max_exact)).astype(np.int32)
    return np.where(n < max_exact, n, np.minimum(large, N_BUCKETS - 1))


def _bias_of_dist(rel_bias, dist):
    lut = _bucket_lut()
    idx = lut[np.clip(dist, 0, MAX_DISTANCE)]
    b = jnp.moveaxis(rel_bias.astype(F32)[idx], -1, 0) * LOG2E
    return jnp.where(jnp.asarray(dist >= 0), b, 0.0)


def _block_diag(w, reps):
    n, d, e = w.shape
    eye = jnp.eye(reps, dtype=w.dtype)
    return jnp.einsum("ab,nde->nadbe", eye, w).reshape(n, reps * d, reps * e)


def _ones_blocks(size, seg):
    return jnp.asarray(np.kron(np.eye(size // seg), np.ones((seg, seg))), BF16)


def _pad_cols(w, width):
    return jnp.pad(w, ((0, 0), (0, width - w.shape[1])))


def _prep(p):
    d_model = p["w_in"].shape[1]
    d_inner = p["w_ssm_out"].shape[1]
    n_heads = d_inner // SSM_HEAD_DIM
    conv_dim = p["conv_w"].shape[2]
    qw = N_Q_HEADS * NSA_HEAD_DIM
    kvw = 2 * N_KV_HEADS * NSA_HEAD_DIM
    splits = (d_inner, conv_dim, n_heads, qw, kvw, kvw, kvw, 3 * N_Q_HEADS, d_model, d_model)
    offs = np.concatenate([[0], np.cumsum(splits)])
    w_in = p["w_in"][0]
    seg = lambda k: w_in[:, offs[k]:offs[k + 1]]
    bf = lambda a: a.astype(BF16)
    o = {}
    o["w_ssm_in"] = [bf(seg(0)), bf(seg(1)), bf(_pad_cols(seg(2), LANES))]
    o["w_nsa_in"] = [bf(seg(3)), bf(seg(4)), bf(seg(5)), bf(seg(6)), bf(_pad_cols(seg(7), LANES)), bf(seg(8)),
                     bf(seg(9))]
    o["norm_mix"] = p["norm_mix_w"][0][None, :]
    kv_half = kvw // 2
    head_w = jnp.zeros((SUBLANES, qw), F32)
    head_w = head_w.at[0].set(jnp.tile(p["q_norm_w"][0], N_Q_HEADS))
    head_w = head_w.at[1, :kv_half].set(jnp.tile(p["k_sel_norm_w"][0], N_KV_HEADS))
    head_w = head_w.at[2, :kv_half].set(jnp.tile(p["k_win_norm_w"][0], N_KV_HEADS))
    o["head_w"] = head_w
    o["ones64"] = _ones_blocks(kv_half, NSA_HEAD_DIM)
    o["conv_w"] = p["conv_w"][0]
    o["conv_b"] = p["conv_b"][0][None, :]
    headp = jnp.zeros((SUBLANES, LANES), F32)
    headp = headp.at[0, :n_heads].set(p["dt_bias"][0]).at[1, :n_heads].set(p["a_log"][0])
    o["headp"] = headp
    o["dsk"] = jnp.repeat(p["d_skip"][0], SSM_HEAD_DIM)[None, :]
    o["ssm_nw"] = p["ssm_norm_w"][0][None, :]
    expand = np.zeros((LANES, d_inner), np.float32)
    for h in range(n_heads):
        expand[h, h * SSM_HEAD_DIM:(h + 1) * SSM_HEAD_DIM] = 1.0
    o["expand"] = jnp.asarray(expand, BF16)
    o["tri"] = jnp.asarray(np.tril(np.ones((SSD_CHUNK, SSD_CHUNK), np.float32)), BF16)
    o["onesg"] = jnp.ones((d_inner // N_SSM_GROUPS,) * 2, BF16)
    o["w_ssm_out"] = bf(p["w_ssm_out"][0])
    o["cmp_wk"] = bf(_block_diag(p["cmp_w_k"][0], N_KV_HEADS))
    o["cmp_wv"] = bf(_block_diag(p["cmp_w_v"][0], N_KV_HEADS))
    o["cmp_pek"] = jnp.tile(p["cmp_pe_k"][0], (1, N_KV_HEADS))
    o["cmp_pev"] = jnp.tile(p["cmp_pe_v"][0], (1, N_KV_HEADS))
    o["kc_nw"] = jnp.tile(p["k_cmp_norm_w"][0], N_KV_HEADS)[None, :]
    rel = p["rel_bias"]
    band = 2 * CMP_STRIDE
    i = np.arange(QT)[:, None]
    c = np.arange(band)[None, :]
    d_band = i + CMP_STRIDE * (band // 2) - CMP_STRIDE * c - (CMP_BLOCK - 1)
    d_band = np.concatenate([d_band, np.full((QT, 1), MAX_DISTANCE)], axis=1)
    tb = _bias_of_dist(rel, d_band)
    o["cmp_tb"] = jnp.pad(tb, ((0, 0), (0, 0), (0, 2 * band - tb.shape[2])))
    jk = np.arange(QT)[:, None]
    iq = np.arange(QT)[None, :]
    o["b0"] = _bias_of_dist(rel, iq - jk)
    o["b1"] = _bias_of_dist(rel, QT + iq - jk)
    o["c31"] = rel[N_BUCKETS - 1].astype(F32) * LOG2E
    gexp = np.zeros((3, LANES, qw), np.float32)
    for h in range(N_Q_HEADS):
        for k in range(3):
            gexp[k, 3 * h + k, h * NSA_HEAD_DIM:(h + 1) * NSA_HEAD_DIM] = 1.0
    o["gexp"] = jnp.asarray(gexp, BF16)
    o["w_nsa_out"] = bf(p["w_nsa_out"][0])
    o["w_out"] = bf(p["w_out"][0])
    o["norm_ffn"] = p["norm_ffn_w"][0][None, :]
    ne = p["w_router"].shape[2]
    o["w_router"] = _pad_cols(p["w_router"][0], LANES)
    o["b_router"] = jnp.full((1, LANES), NEG, F32).at[0, :ne].set(p["b_router"][0])
    o["w1"] = bf(p["w_gate_up"][0])
    o["b1e"] = p["b_gate_up"][0][:, None, :]
    o["w2"] = bf(p["w_down"][0])
    o["b2e"] = p["b_down"][0][:, None, :]
    return o


def _kv_layouts(kv, batch):
    n = kv.shape[0]
    s = n // batch
    half = kv.shape[1] // 2
    k = kv[:, :half].astype(BF16).reshape(batch, s, N_KV_HEADS, NSA_HEAD_DIM).transpose(0, 2, 1, 3)
    v = kv[:, half:].astype(BF16).reshape(batch, s // QT, QT, N_KV_HEADS, NSA_HEAD_DIM).transpose(0, 3, 1, 4, 2)
    extra = jnp.zeros(v.shape[:3] + (SUBLANES, QT), BF16).at[:, :, :, 0, :].set(1.0)
    return k, jnp.concatenate([v, extra], axis=3)


def _prompt_mixer(x, o, batch):
    z, xbc, dt = _proj(x, o["norm_mix"], o["ones64"], o["head_w"], o["w_ssm_in"], ["raw", "raw", "raw"], 256)
    q, kvc, kvs, kvw, brg, sg_ssm, sg_nsa = _proj(
        x, o["norm_mix"], o["ones64"], o["head_w"], o["w_nsa_in"], ["q", "raw", "ks", "kw", "sig", "sig", "sig"], 256)
    m_ssm, h_t = _ssd_prompt(xbc, z, dt, sg_ssm, o["conv_w"], o["conv_b"], o["headp"], o["dsk"], o["ssm_nw"],
                             o["expand"], o["tri"], o["onesg"], o["w_ssm_out"], batch)
    kc, vc = _cmp_build(kvc, o["cmp_wk"], o["cmp_wv"], o["cmp_pek"], o["cmp_pev"], o["ones64"], o["kc_nw"], batch)
    ocmp, sel = _cmp_attn(q, brg, kc, vc, o["cmp_tb"], o["gexp"], batch)
    ks, vs_t = _kv_layouts(kvs, batch)
    kw, vw_t = _kv_layouts(kvw, batch)
    h = _nsa_main(o["c31"], q, brg, sel, ocmp, ks, vs_t, kw, vw_t, o["b0"], o["b1"], m_ssm, sg_nsa, x,
                  o["w_nsa_out"], o["w_out"], batch)
    return h, (kvc, kvs, kvw, h_t, xbc)


MOE_TOKEN_TILE = 384
MOE_GROUP_TILE = 512


def _moe(h_all, o):
    n, d = h_all.shape
    tm, tg = MOE_TOKEN_TILE, MOE_GROUP_TILE
    ne = o["w1"].shape[0]
    n_pad = -(-n // tm) * tm
    hp = jnp.pad(h_all, ((0, n_pad - n), (0, 0))) if n_pad != n else h_all
    hn, meta, cnt = _router(hp, o["norm_ffn"], o["w_router"], o["b_router"], tm, n)
    eid = meta[:n, META_EXPERT:META_EXPERT + TOP_K].astype(jnp.int32)
    rank = meta[:n, META_RANK:META_RANK + TOP_K].astype(jnp.int32)
    count = cnt[0, :ne].astype(jnp.int32)
    tiles = (count + tg - 1) // tg
    tile_end = jnp.cumsum(tiles)
    pos = (tile_end - tiles)[eid] * tg + rank
    n_tiles = -(-(n * TOP_K) // tg) + ne
    tile_expert = jnp.minimum(jnp.sum(tile_end[None, :] <= jnp.arange(n_tiles)[:, None], axis=1), ne - 1)
    sp = jnp.concatenate([tile_expert, tile_end[-1:]]).astype(jnp.int32)
    src = jnp.zeros((n_tiles * tg,), jnp.int32).at[pos.reshape(-1)].set(
        jnp.arange(n * TOP_K, dtype=jnp.int32) // TOP_K)
    y_sorted = _moe_group(sp, hn[src], o["w1"], o["b1e"], o["w2"], o["b2e"], tg)
    yk = y_sorted[pos.T.reshape(-1)].reshape(TOP_K, n, d)
    if n_pad != n:
        yk = jnp.pad(yk, ((0, 0), (0, n_pad - n), (0, 0)))
    return _moe_combine(hp, yk, meta, tm)[:n]


def kernel(x_prompt, x_sample, cache_cmp, cache_sel, cache_win, state_ssm, state_conv, page_table, norm_mix_w, w_in,
           conv_w, conv_b, dt_bias, a_log, d_skip, ssm_norm_w, w_ssm_out, q_norm_w, k_cmp_norm_w, k_sel_norm_w,
           k_win_norm_w, cmp_pe_k, cmp_w_k, cmp_pe_v, cmp_w_v, rel_bias, w_nsa_out, w_out, norm_ffn_w, w_router,
           b_router, w_gate_up, b_gate_up, w_down, b_down):
    params = dict(norm_mix_w=norm_mix_w, w_in=w_in, conv_w=conv_w, conv_b=conv_b, dt_bias=dt_bias, a_log=a_log,
                  d_skip=d_skip, ssm_norm_w=ssm_norm_w, w_ssm_out=w_ssm_out, q_norm_w=q_norm_w,
                  k_cmp_norm_w=k_cmp_norm_w, k_sel_norm_w=k_sel_norm_w, k_win_norm_w=k_win_norm_w,
                  cmp_pe_k=cmp_pe_k, cmp_w_k=cmp_w_k, cmp_pe_v=cmp_pe_v, cmp_w_v=cmp_w_v, rel_bias=rel_bias,
                  w_nsa_out=w_nsa_out, w_out=w_out, norm_ffn_w=norm_ffn_w, w_router=w_router, b_router=b_router,
                  w_gate_up=w_gate_up, b_gate_up=b_gate_up, w_down=w_down, b_down=b_down)
    o = _prep(params)
    bsz, s, d = x_prompt.shape
    db, t, _ = x_sample.shape
    kvshape = (2, N_KV_HEADS, NSA_HEAD_DIM)
    hp, (kvc, kvs, kvw, h_t, xbc) = _prompt_mixer(x_prompt.reshape(bsz * s, d), o, bsz)
    assert t == 1, "the sample group decodes one token per sequence"
    hs, (kvc_s, kvs_s, win_s, h_s, conv_s) = _sample_mixer(
        x_sample.reshape(db, d), o, cache_cmp[0], cache_sel[0], cache_win[0], state_ssm[0], state_conv[0],
        page_table, rel_bias)
    y_all = _moe(jnp.concatenate([hp, hs], axis=0), o)
    wlen = min(WINDOW, s)
    n_heads = h_t.shape[2] // SSM_HEAD_DIM
    outs_p = (kvc.reshape((1, bsz, s) + kvshape), kvs.reshape((1, bsz, s) + kvshape),
              kvw.reshape((bsz, s) + kvshape)[None, :, s - wlen:],
              h_t.reshape(bsz, D_STATE, n_heads, SSM_HEAD_DIM).transpose(0, 2, 3, 1)[None],
              xbc.reshape(bsz, s, -1)[None, :, s - (D_CONV - 1):])
    outs_s = (kvc_s.reshape((1, db, t) + kvshape), kvs_s.reshape((1, db, t) + kvshape),
              win_s.reshape((1, db, win_s.shape[1]) + kvshape), h_s.reshape((1, db, n_heads, SSM_HEAD_DIM, D_STATE)),
              conv_s.reshape(1, db, D_CONV - 1, -1))
    return (y_all[:bsz * s].reshape(bsz, s, d), y_all[bsz * s:].reshape(db, t, d)) + outs_p + outs_s


def _sample_mixer(x, o, cache_cmp, cache_sel, cache_win, state_ssm, state_conv, page_table, rel):
    n = x.shape[0]
    z, xbc, dt = _proj(x, o["norm_mix"], o["ones64"], o["head_w"], o["w_ssm_in"], ["raw", "raw", "raw"], n)
    q, kvc, kvs, kvw, brg, sg_ssm, sg_nsa = _proj(
        x, o["norm_mix"], o["ones64"], o["head_w"], o["w_nsa_in"], ["q", "raw", "ks", "kw", "sig", "sig", "sig"], n)
    d_inner = z.shape[1]
    n_heads = d_inner // SSM_HEAD_DIM
    conv_new, xs, bm, ct, dtx_t, dec = _mamba_prep(xbc, state_conv.reshape(n, -1), dt, o["conv_w"], o["conv_b"],
                                                   o["headp"], o["expand"])
    h_new, y_t = _mamba_state(dec[:, :n_heads], state_ssm.reshape(n, d_inner, D_STATE), dtx_t, bm, ct)
    pool, page = cache_cmp.shape[0], cache_cmp.shape[1]
    n_pages = page_table.shape[1]
    past = n_pages * page
    tq = past
    assert page >= MAX_DISTANCE and page % SEL_BLOCK == 0 and past % CMP_STRIDE == 0
    half = kvc.shape[1] // 2
    nch = half // LANES
    cmp_t = jnp.transpose(cache_cmp, (0, 2, 3, 4, 1)).reshape(pool, 2 * nch, LANES, page)
    sel_t = jnp.transpose(cache_sel, (0, 2, 3, 4, 1))
    nstr = -(-(past // CMP_STRIDE + 1) // SUBLANES) * SUBLANES
    w_kv = jnp.stack([o["cmp_wk"], o["cmp_wv"]])
    pe_kv = jnp.stack([o["cmp_pek"], o["cmp_pev"]])
    tok = _page_cmp(page_table, cmp_t, kvc[:, None, :], w_kv, _cmp_const(pe_kv, w_kv), o["ones64"], o["kc_nw"], nstr)
    pad_heads = lambda b: jnp.pad(b.reshape(N_KV_HEADS, Q_PER_KV, -1), ((0, 0), (0, SUBLANES - Q_PER_KV), (0, 0)))
    tbc = pad_heads(_bias_of_dist(rel, tq - (CMP_STRIDE * np.arange(nstr) + CMP_BLOCK - 1)))
    wlen = cache_win.shape[1]
    wrows = -(-(wlen + 1) // SUBLANES) * SUBLANES
    past_w = past - wlen
    tbw = pad_heads(_bias_of_dist(rel, tq - (past_w + np.arange(wrows))))
    n_past_sel = past // SEL_BLOCK
    nsel = n_past_sel + 1
    nsel_pad = -(-nsel // LANES) * LANES
    o_cw, selmask, win_new = _sample_cw(q[:, None, :], brg[:, None, :], tok, cache_win.reshape(n, wlen, -1),
                                        kvw[:, None, :], tbc, tbw, o["gexp"], tq, past_w, nsel_pad)
    nblk = min(N_SEL_BLOCKS, nsel)
    picked = selmask[:, :N_KV_HEADS, :nsel] > 0.5
    order = jnp.cumsum(picked, axis=-1) - 1
    hit = picked[..., None] & (order[..., None] == jnp.arange(nblk))
    idx = jnp.sum(jnp.where(hit, jnp.arange(nsel)[:, None], 0), axis=2)
    is_past = idx < n_past_sel
    jp = jnp.minimum(idx, n_past_sel - 1)
    per_page = page // SEL_BLOCK
    pg = jp // per_page
    phys = jnp.take_along_axis(page_table, pg.reshape(n, -1), axis=1).astype(jnp.int32)
    lane_blk = (np.arange(page) // SEL_BLOCK)[None, None, None, :]
    attended = is_past[..., None] & (lane_blk == (jp % per_page)[..., None])
    code = jnp.where(attended, jnp.where((pg == n_pages - 1)[..., None], 2.0, 1.0), 0.0).reshape(n, N_KV_HEADS, -1)
    new_on = jnp.any(idx >= n_past_sel, axis=-1).astype(F32)
    code = jnp.concatenate([code, jnp.broadcast_to(new_on[..., None], code.shape)], axis=1)
    tbl = pad_heads(_bias_of_dist(rel, tq - ((n_pages - 1) * page + np.arange(page))))
    f0 = rel[_bucket_lut()[0]].astype(F32) * LOG2E
    y_nsa = _sample_sel(phys, o["c31"], f0, sel_t, q[:, None, :], brg[:, None, :], kvs[:, None, :], code, o_cw, tbl,
                        o["gexp"])
    h = _sample_merge(y_t, xs, z, sg_ssm, o["dsk"], o["ssm_nw"], o["onesg"], o["w_ssm_out"], y_nsa[:, 0], sg_nsa, x,
                      o["w_nsa_out"], o["w_out"])
    return h, (kvc, kvs, win_new, h_new, conv_new)
```

```python
import functools
import math

import jax
import jax.numpy as jnp
import numpy as np
from jax import lax
from jax.experimental import pallas as pl
from jax.experimental.pallas import tpu as pltpu

F32 = jnp.float32
BF16 = jnp.bfloat16

SSM_HEAD_DIM = 64
N_SSM_GROUPS = 4
D_STATE = 128
D_CONV = 4
SSD_CHUNK = 128
NSA_HEAD_DIM = 64
N_Q_HEADS = 16
N_KV_HEADS = 4
Q_PER_KV = N_Q_HEADS // N_KV_HEADS
CMP_BLOCK = 32
CMP_STRIDE = 16
SEL_BLOCK = 64
N_SEL_BLOCKS = 16
WINDOW = 512
N_BUCKETS = 32
MAX_DISTANCE = 128
TOP_K = 4
SWIGLU_LIMIT = 7.0
SWIGLU_ALPHA = 1.702
RMS_EPS = 1e-6
LOG2E = math.log2(math.e)
NEG = -1e30
FORCE_SCORE = 1e9

LANES = 128
SUBLANES = 8
QT = 128
FAR_TILES_PER_TRIP = 4
VMEM_LIMIT = 56 * 1024 * 1024


def _cparams(sem):
    return pltpu.CompilerParams(dimension_semantics=sem, vmem_limit_bytes=VMEM_LIMIT)


def _bdot(a, b):
    return jnp.dot(a.astype(BF16), b.astype(BF16), preferred_element_type=F32)


def _bdot_nt(a, b):
    return lax.dot_general(a.astype(BF16), b.astype(BF16), (((1,), (1,)), ((), ())),
                           preferred_element_type=F32)


def _split3(a):
    hi = a.astype(BF16)
    r = a - hi.astype(F32)
    mid = r.astype(BF16)
    lo = (r - mid.astype(F32)).astype(BF16)
    return hi, mid, lo


def _dot3(a, b):
    hi, mid, lo = _split3(a)
    d = lambda p: jnp.dot(p, b, preferred_element_type=F32)
    return (d(hi) + d(mid)) + d(lo)


def _dot3_l(a, b):
    hi, mid, lo = _split3(b)
    d = lambda p: jnp.dot(a, p, preferred_element_type=F32)
    return (d(hi) + d(mid)) + d(lo)


def _dot3_nt_l(a, b):
    hi, mid, lo = _split3(b)
    d = lambda p: lax.dot_general(a, p, (((1,), (1,)), ((), ())), preferred_element_type=F32)
    return (d(hi) + d(mid)) + d(lo)


def _seg_sum(y, ones_blk):
    c = ones_blk.shape[0]
    outs = []
    for k in range(y.shape[1] // c):
        outs.append(_dot3(y[:, k * c:(k + 1) * c], ones_blk))
    return outs[0] if len(outs) == 1 else jnp.concatenate(outs, axis=1)


def _sigmoid(x):
    return 1.0 / (1.0 + jnp.exp(-x))


def _rms_rows(x, w):
    ms = jnp.mean(x * x, axis=-1, keepdims=True)
    return (x * lax.rsqrt(ms + RMS_EPS)) * w


def _head_rms(y, ones64, w, seg):
    ms = _seg_sum(y * y, ones64) * (1.0 / seg)
    return (y * lax.rsqrt(ms + RMS_EPS)) * w


def _proj_kernel(kinds, x_ref, nw_ref, ones_ref, hw_ref, *refs):
    n = len(kinds)
    w_refs, o_refs = refs[:n], refs[n:]
    xn = _rms_rows(x_ref[...], nw_ref[...]).astype(BF16)
    ones64 = ones_ref[...]
    for kind, w_ref, o_ref in zip(kinds, w_refs, o_refs):
        y = jnp.dot(xn, w_ref[...], preferred_element_type=F32)
        if kind == "sig":
            y = _sigmoid(y)
        elif kind == "q":
            y = _head_rms(y, ones64, hw_ref[0:1, :], NSA_HEAD_DIM) * (NSA_HEAD_DIM ** -0.5 * LOG2E)
        elif kind in ("ks", "kw"):
            row = 1 if kind == "ks" else 2
            half = y.shape[1] // 2
            k = _head_rms(y[:, :half], ones64, hw_ref[row:row + 1, :half], NSA_HEAD_DIM)
            y = jnp.concatenate([k, y[:, half:]], axis=1)
        o_ref[...] = y


def _proj(x, norm_w, ones64, head_w, weights, kinds, tm):
    n, d = x.shape
    assert n % tm == 0
    const = lambda i: (0, 0)
    in_specs = [pl.BlockSpec((tm, d), lambda i: (i, 0)),
                pl.BlockSpec((1, d), const),
                pl.BlockSpec(ones64.shape, const),
                pl.BlockSpec(head_w.shape, const)]
    in_specs += [pl.BlockSpec(w.shape, const) for w in weights]
    out_specs = [pl.BlockSpec((tm, w.shape[1]), lambda i: (i, 0)) for w in weights]
    out_shape = [jax.ShapeDtypeStruct((n, w.shape[1]), F32) for w in weights]
    return pl.pallas_call(
        functools.partial(_proj_kernel, tuple(kinds)),
        grid=(n // tm,), in_specs=in_specs, out_specs=out_specs, out_shape=out_shape,
        compiler_params=_cparams(("parallel",)), name="in_proj",
    )(x, norm_w, ones64, head_w, *weights)


def _softplus(x):
    return jnp.maximum(x, 0.0) + jnp.log1p(jnp.exp(-jnp.abs(x)))


def _ssd_kernel(xbc_ref, z_ref, dt_ref, sg_ref, cw_ref, cb_ref, hp_ref, dsk_ref, nw_ref, e_ref, tri_ref,
                onesg_ref, wout_ref, o_ref, ht_ref, ht_s, ext_s):
    c = pl.program_id(1)
    q = SSD_CHUNK
    d_inner = z_ref.shape[1]
    gw = d_inner // N_SSM_GROUPS
    hpg = gw // SSM_HEAD_DIM
    tail = SUBLANES

    @pl.when(c == 0)
    def _():
        ht_s[...] = jnp.zeros_like(ht_s)
        ext_s[0:tail, :] = jnp.zeros((tail, ext_s.shape[1]), F32)

    xb = xbc_ref[...]
    ext_s[tail:tail + q, :] = xb
    u = cb_ref[...] + cw_ref[D_CONV - 1:D_CONV, :] * xb
    for k in range(1, D_CONV):
        u = u + cw_ref[D_CONV - 1 - k:D_CONV - k, :] * ext_s[tail - k:tail - k + q, :]
    ext_s[0:tail, :] = xb[q - tail:q, :]
    u = u * _sigmoid(u)
    xs = u[:, :d_inner]
    bm = u[:, d_inner:d_inner + N_SSM_GROUPS * D_STATE]
    cm = u[:, d_inner + N_SSM_GROUPS * D_STATE:]

    dtv = _softplus(dt_ref[...] + hp_ref[0:1, :])
    a = dtv * (-jnp.exp(hp_ref[1:2, :]))
    acum = _dot3_l(tri_ref[...], a)
    acum_t = acum.T
    eacum = jnp.exp(acum)
    w_end = jnp.exp(acum[q - 1:q, :] - acum) * dtv
    e = e_ref[...]
    dt_x = _dot3(dtv, e)
    we_x = _dot3(w_end, e)
    ea_x = _dot3(eacum, e)
    xdt = (xs * dt_x).astype(BF16)
    xdtw = (xs * we_x).astype(BF16)
    ii = lax.broadcasted_iota(jnp.int32, (q, q), 0)
    jj = lax.broadcasted_iota(jnp.int32, (q, q), 1)
    causal = jj <= ii

    groups = range(N_SSM_GROUPS)
    gsl = [slice(g * gw, (g + 1) * gw) for g in groups]
    cgs = [cm[:, g * D_STATE:(g + 1) * D_STATE].astype(BF16) for g in groups]
    bgs = [bm[:, g * D_STATE:(g + 1) * D_STATE] for g in groups]
    cbms = [_bdot_nt(cgs[g], bgs[g]) for g in groups]
    hgs = [ht_s[:, gsl[g]] for g in groups]
    y_inter = [jnp.dot(cgs[g], hgs[g].astype(BF16), preferred_element_type=F32) for g in groups]
    sts = [jnp.dot(bgs[g].T.astype(BF16), xdtw[:, gsl[g]], preferred_element_type=F32) for g in groups]
    ys = []
    for g in groups:
        yh = []
        for r in range(hpg):
            h = g * hpg + r
            seg = acum[:, h:h + 1] - acum_t[h:h + 1, :]
            dec = jnp.exp(jnp.where(causal, seg, NEG))
            m = (cbms[g] * dec).astype(BF16)
            yh.append(jnp.dot(m, xdt[:, h * SSM_HEAD_DIM:(h + 1) * SSM_HEAD_DIM], preferred_element_type=F32))
        ys.append(y_inter[g] * ea_x[:, gsl[g]] + jnp.concatenate(yh, axis=1))
        ht_s[:, gsl[g]] = hgs[g] * ea_x[q - 1:q, gsl[g]] + sts[g]
    y = jnp.concatenate(ys, axis=1) + dsk_ref[...] * xs
    zz = z_ref[...]
    y = y * (zz * _sigmoid(zz))
    ms = _seg_sum(y * y, onesg_ref[...]) * (1.0 / gw)
    y = (y * lax.rsqrt(ms + RMS_EPS)) * nw_ref[...]
    o_ref[...] = sg_ref[...] * _bdot(y, wout_ref[...])

    @pl.when(c == pl.num_programs(1) - 1)
    def _():
        ht_ref[0] = ht_s[...]


def _ssd_prompt(xbc, z, dt, sg, conv_w, conv_b, headp, dsk, nw, expand, tri, onesg, wout, batch):
    n, conv_dim = xbc.shape
    d_inner = z.shape[1]
    d_model = wout.shape[1]
    q = SSD_CHUNK
    nc = n // batch // q
    const = lambda b, c: (0, 0)
    rows = lambda b, c: (b * nc + c, 0)
    in_specs = [pl.BlockSpec((q, conv_dim), rows), pl.BlockSpec((q, d_inner), rows),
                pl.BlockSpec((q, LANES), rows), pl.BlockSpec((q, d_model), rows),
                pl.BlockSpec(conv_w.shape, const), pl.BlockSpec(conv_b.shape, const),
                pl.BlockSpec(headp.shape, const), pl.BlockSpec(dsk.shape, const),
                pl.BlockSpec(nw.shape, const), pl.BlockSpec(expand.shape, const),
                pl.BlockSpec(tri.shape, const), pl.BlockSpec(onesg.shape, const),
                pl.BlockSpec(wout.shape, const)]
    out_specs = [pl.BlockSpec((q, d_model), rows),
                 pl.BlockSpec((1, D_STATE, d_inner), lambda b, c: (b, 0, 0))]
    out_shape = [jax.ShapeDtypeStruct((n, d_model), F32),
                 jax.ShapeDtypeStruct((batch, D_STATE, d_inner), F32)]
    return pl.pallas_call(
        _ssd_kernel, grid=(batch, nc), in_specs=in_specs, out_specs=out_specs, out_shape=out_shape,
        scratch_shapes=[pltpu.VMEM((D_STATE, d_inner), F32), pltpu.VMEM((SUBLANES + q, conv_dim), F32)],
        compiler_params=_cparams(("parallel", "arbitrary")), name="ssd_prompt",
    )(xbc, z, dt, sg, conv_w, conv_b, headp, dsk, nw, expand, tri, onesg, wout)


def _cmp_build_kernel(kvc_ref, wk_ref, wv_ref, pek_ref, pev_ref, ones_ref, nw_ref, kc_ref, vc_ref):
    ns = kc_ref.shape[0]
    kw = kc_ref.shape[1]
    lo_k = hi_k = lo_v = hi_v = None
    add = lambda acc, v: v if acc is None else acc + v
    nch = 2 * kw // LANES
    for i in range(CMP_STRIDE):
        x = jnp.concatenate([kvc_ref[pl.ds(nch * i + c, ns, stride=nch * CMP_STRIDE), :] for c in range(nch)], axis=1)
        xk, xv = x[:, :kw], x[:, kw:]
        j = CMP_STRIDE + i
        lo_k = add(lo_k, _bdot(xk + pek_ref[i:i + 1, :], wk_ref[i]))
        hi_k = add(hi_k, _bdot(xk + pek_ref[j:j + 1, :], wk_ref[j]))
        lo_v = add(lo_v, _bdot(xv + pev_ref[i:i + 1, :], wv_ref[i]))
        hi_v = add(hi_v, _bdot(xv + pev_ref[j:j + 1, :], wv_ref[j]))
    kc = lo_k + pltpu.roll(hi_k, ns - 1, axis=0)
    vc_ref[...] = lo_v + pltpu.roll(hi_v, ns - 1, axis=0)
    kc_ref[...] = _head_rms(kc, ones_ref[...], nw_ref[...], NSA_HEAD_DIM)


def _cmp_build(kvc, wk, wv, pek, pev, ones64, nw, batch):
    n, w = kvc.shape
    s = n // batch
    ns = s // CMP_STRIDE
    kw = w // 2
    c2 = lambda b: (0, 0)
    c3 = lambda b: (0, 0, 0)
    nch = w // LANES
    kvc = kvc.reshape(n * nch, LANES)
    return pl.pallas_call(
        _cmp_build_kernel, grid=(batch,),
        in_specs=[pl.BlockSpec((s * nch, LANES), lambda b: (b, 0)), pl.BlockSpec(wk.shape, c3),
                  pl.BlockSpec(wv.shape, c3),
                  pl.BlockSpec(pek.shape, c2), pl.BlockSpec(pev.shape, c2), pl.BlockSpec(ones64.shape, c2),
                  pl.BlockSpec(nw.shape, c2)],
        out_specs=[pl.BlockSpec((ns, kw), lambda b: (b, 0)), pl.BlockSpec((ns, kw), lambda b: (b, 0))],
        out_shape=[jax.ShapeDtypeStruct((batch * ns, kw), F32)] * 2,
        compiler_params=_cparams(("parallel",)), name="cmp_build",
    )(kvc, wk, wv, pek, pev, ones64, nw)


def _rank_select(v, n_keep):
    n = v.shape[0]
    rows = SUBLANES
    sub = lax.broadcasted_iota(jnp.int32, (rows, v.shape[1]), 0)
    chunks = [v[c * rows:(c + 1) * rows, :] for c in range(n // rows)]
    cnts = [jnp.zeros(ch.shape, F32) for ch in chunks]
    for i in range(n):
        ri = v[i:i + 1, :]
        for c, ch in enumerate(chunks):
            if c * rows > i:
                beats = jnp.where(ri >= ch, 1.0, 0.0)
            elif (c + 1) * rows - 1 <= i:
                beats = jnp.where(ri > ch, 1.0, 0.0)
            else:
                beats = jnp.where(sub > i - c * rows, jnp.where(ri >= ch, 1.0, 0.0), jnp.where(ri > ch, 1.0, 0.0))
            cnts[c] = cnts[c] + beats
    return jnp.where(jnp.concatenate(cnts, axis=0) < n_keep, 1.0, 0.0)


def _cmp_attn_kernel(q_ref, brg_ref, kc_ref, vc_ref, tb_ref, gexp_ref, ocmp_ref, sel_ref):
    qi = pl.program_id(1)
    t0 = qi * QT
    ns = kc_ref.shape[0]
    nsel = sel_ref.shape[2]
    nb = tb_ref.shape[2]
    band = 2 * CMP_STRIDE
    tt = t0 + lax.broadcasted_iota(jnp.int32, (QT, ns), 0)
    nn = lax.broadcasted_iota(jnp.int32, (QT, ns), 1)
    mask = (CMP_STRIDE * nn + (CMP_BLOCK - 1)) <= tt
    first = (QT // CMP_STRIDE) * qi - band // 2
    cc = lax.broadcasted_iota(jnp.int32, (nb, ns), 0)
    n2 = lax.broadcasted_iota(jnp.int32, (nb, ns), 1)
    shift = jnp.where(((cc < band) & (n2 == first + cc)) | ((cc == band) & (n2 < first)), 1.0, 0.0).astype(BF16)
    jb = lax.broadcasted_iota(jnp.int32, (nsel, ns), 0) * SEL_BLOCK
    cs = lax.broadcasted_iota(jnp.int32, (nsel, ns), 1) * CMP_STRIDE
    cover_t = jnp.where((cs < jb + SEL_BLOCK) & (cs + CMP_BLOCK > jb), 1.0, 0.0).astype(BF16)
    jidx = lax.broadcasted_iota(jnp.int32, (nsel, QT), 0)
    tq = t0 + lax.broadcasted_iota(jnp.int32, (nsel, QT), 1)
    valid = jidx * SEL_BLOCK <= tq
    cur = tq // SEL_BLOCK
    forced = valid & ((jidx == 0) | (jidx == cur) | (jidx == cur - 1))

    q = q_ref[...].astype(BF16)
    head_cols = lambda a, g: a[:, g * NSA_HEAD_DIM:(g + 1) * NSA_HEAD_DIM]
    kgs = [head_cols(kc_ref, g).astype(BF16) for g in range(N_KV_HEADS)]
    vgs = [head_cols(vc_ref, g).astype(BF16) for g in range(N_KV_HEADS)]
    raw = [_bdot_nt(head_cols(q, h), kgs[h // Q_PER_KV]) + _dot3(tb_ref[h], shift) for h in range(N_Q_HEADS)]
    ps = []
    for s in raw:
        s = jnp.where(mask, s, NEG)
        m = jnp.max(s, axis=-1, keepdims=True)
        p = jnp.where(mask, jnp.exp2(s - m), 0.0)
        l = jnp.sum(p, axis=-1, keepdims=True)
        ps.append(p * jnp.where(l > 0.0, 1.0 / l, 0.0))
    outs = [_bdot(ps[h], vgs[h // Q_PER_KV]) for h in range(N_Q_HEADS)]
    imps = []
    for g in range(N_KV_HEADS):
        psum = functools.reduce(lambda a, b: a + b, ps[g * Q_PER_KV:(g + 1) * Q_PER_KV])
        imps.append(_dot3_nt_l(cover_t, psum))
    for g in range(N_KV_HEADS):
        v = jnp.where(forced, FORCE_SCORE, jnp.where(valid, imps[g], NEG))
        sel_ref[0, g] = _rank_select(v, N_SEL_BLOCKS)
    ocmp_ref[...] = jnp.concatenate(outs, axis=1) * _dot3(brg_ref[...], gexp_ref[0])


def _cmp_attn(q, brg, kc, vc, tb, gexp, batch):
    n, qw = q.shape
    s = n // batch
    nq = s // QT
    ns, kw = kc.shape[0] // batch, kc.shape[1]
    nsel = s // SEL_BLOCK
    rows = lambda b, i: (b * nq + i, 0)
    per_b = lambda b, i: (b, 0)
    return pl.pallas_call(
        _cmp_attn_kernel, grid=(batch, nq),
        in_specs=[pl.BlockSpec((QT, qw), rows), pl.BlockSpec((QT, LANES), rows),
                  pl.BlockSpec((ns, kw), per_b), pl.BlockSpec((ns, kw), per_b),
                  pl.BlockSpec(tb.shape, lambda b, i: (0, 0, 0)),
                  pl.BlockSpec((1,) + gexp.shape[1:], lambda b, i: (0, 0, 0))],
        out_specs=[pl.BlockSpec((QT, qw), rows), pl.BlockSpec((1, N_KV_HEADS, nsel, QT), lambda b, i: (b, 0, 0, i))],
        out_shape=[jax.ShapeDtypeStruct((n, qw), F32), jax.ShapeDtypeStruct((batch, N_KV_HEADS, nsel, s), F32)],
        compiler_params=_cparams(("parallel", "parallel")), name="cmp_attn",
    )(q, brg, kc, vc, tb, gexp)


def _attn_round(states, raw, keeps, biases, vts, shifts):
    stats = []
    for st, s, keep, bias, shift in zip(states, raw, keeps, biases, shifts):
        s = _masked_scores(s, keep, bias)
        tile_max = jnp.max(s, axis=0, keepdims=True)
        if shift is not None:
            tile_max = tile_max + shift
        m_new = tile_max if st is None else jnp.maximum(st[0], tile_max)
        alpha = None if st is None else jnp.exp2(st[0] - m_new)
        p = jnp.exp2(s - (m_new if shift is None else m_new - shift)).astype(BF16)
        stats.append((m_new, alpha, p))
    out = []
    for st, (m_new, alpha, p), vt in zip(states, stats, vts):
        pv = jnp.dot(vt, p, preferred_element_type=F32)
        out.append((m_new, pv if st is None else alpha * st[1] + pv))
    return out


def _masked_scores(s, keep, bias=None):
    parts = []
    for r in range(s.shape[1] // QT):
        v = s[:, r * QT:(r + 1) * QT]
        parts.append(jnp.where(keep, v if bias is None else v + bias[r], NEG))
    return jnp.concatenate(parts, axis=1)


def _nsa_main_kernel(c31_ref, q_ref, brg_ref, sel_ref, ocmp_ref, ks_ref, vs_ref, kw_ref, vw_ref, b0_ref, b1_ref,
                     mssm_ref, sgn_ref, x_ref, wn_ref, wo_ref, h_ref):
    qi = pl.program_id(1)
    nwt = WINDOW // QT
    blk_per_tile = QT // SEL_BLOCK
    jk = lax.broadcasted_iota(jnp.int32, (QT, QT), 0)
    iq = lax.broadcasted_iota(jnp.int32, (QT, QT), 1)
    causal_t = jk <= iq
    q_t = q_ref[...].T.astype(BF16)
    gates_t = brg_ref[...].T

    def sel_keep(g, kt, tiles=1):
        rows = [jnp.broadcast_to(sel_ref[0, g, pl.ds(blk_per_tile * kt + b, 1), :], (SEL_BLOCK, QT))
                for b in range(blk_per_tile * tiles)]
        return jnp.concatenate(rows, axis=0) > 0.5

    def key_tile(ref, g, kt):
        return ref[0, g, pl.ds(pl.multiple_of(kt * QT, QT), QT), :]

    groups = range(N_KV_HEADS)
    heads = [[g * Q_PER_KV + r for r in range(Q_PER_KV)] for g in groups]
    qg = [jnp.concatenate([q_t[h * NSA_HEAD_DIM:(h + 1) * NSA_HEAD_DIM, :] for h in heads[g]], axis=1) for g in groups]
    far = [jnp.concatenate([jnp.full((1, QT), c31_ref[h], F32) for h in heads[g]], axis=1) for g in groups]
    near0 = [[b0_ref[h] for h in heads[g]] for g in groups]
    near1 = [[b1_ref[h] for h in heads[g]] for g in groups]
    scores = lambda ref, g, kt: jnp.dot(key_tile(ref, g, kt), qg[g], preferred_element_type=F32)
    prev = jnp.maximum(qi - 1, 0)
    has_prev = jnp.broadcast_to(qi >= 1, (QT, QT))

    none4 = [None] * N_KV_HEADS
    st = _attn_round(
        none4 + none4,
        [scores(ks_ref, g, qi) for g in groups] + [scores(kw_ref, g, qi) for g in groups],
        [causal_t & sel_keep(g, qi) for g in groups] + [causal_t] * N_KV_HEADS, near0 + near0,
        [vs_ref[0, g, qi] for g in groups] + [vw_ref[0, g, qi] for g in groups], none4 + none4)
    st = _attn_round(
        st, [scores(ks_ref, g, prev) for g in groups] + [scores(kw_ref, g, prev) for g in groups],
        [sel_keep(g, prev) & has_prev for g in groups] + [has_prev] * N_KV_HEADS, near1 + near1,
        [vs_ref[0, g, prev] for g in groups] + [vw_ref[0, g, prev] for g in groups], none4 + none4)
    sel, win = st[:N_KV_HEADS], st[N_KV_HEADS:]
    for back in range(2, nwt + 1):
        kt = jnp.maximum(qi - back, 0)
        keep = jnp.broadcast_to(qi >= back, (QT, QT))
        if back == nwt:
            keep = keep & (jk > iq)
        win = _attn_round(win, [scores(kw_ref, g, kt) for g in groups], [keep] * N_KV_HEADS, none4,
                          [vw_ref[0, g, kt] for g in groups], far)

    def sel_body(kt, state):
        return tuple(_attn_round(list(state), [scores(ks_ref, g, kt) for g in groups],
                                 [sel_keep(g, kt) for g in groups], none4, [vs_ref[0, g, kt] for g in groups], far))

    wide = FAR_TILES_PER_TRIP

    def sel_body_wide(j, state):
        kt = wide * j
        raw = [jnp.dot(ks_ref[0, g, pl.ds(pl.multiple_of(kt * QT, QT), wide * QT), :], qg[g],
                       preferred_element_type=F32) for g in groups]
        keeps = [sel_keep(g, kt, wide) for g in groups]
        vts = [jnp.concatenate([vs_ref[0, g, kt + i] for i in range(wide)], axis=1) for g in groups]
        return tuple(_attn_round(list(state), raw, keeps, none4, vts, far))

    n_far = jnp.maximum(qi - 1, 0)
    sel = lax.fori_loop(0, n_far // wide, sel_body_wide, tuple(sel))
    sel = lax.fori_loop(wide * (n_far // wide), n_far, sel_body, sel)

    o_t = []
    dh = NSA_HEAD_DIM
    for g in groups:
        o_s = sel[g][1][0:dh, :] * (1.0 / sel[g][1][dh:dh + 1, :])
        o_w = win[g][1][0:dh, :] * (1.0 / win[g][1][dh:dh + 1, :])
        for r, h in enumerate(heads[g]):
            sl = slice(r * QT, (r + 1) * QT)
            o_t.append(gates_t[3 * h + 1:3 * h + 2, :] * o_s[:, sl] + gates_t[3 * h + 2:3 * h + 3, :] * o_w[:, sl])

    pairs = [jnp.concatenate(o_t[2 * k:2 * k + 2], axis=0).T for k in range(len(o_t) // 2)]
    y_nsa = jnp.concatenate(pairs, axis=1) + ocmp_ref[...]
    u = mssm_ref[...] + sgn_ref[...] * _bdot(y_nsa, wn_ref[...])
    h_ref[...] = x_ref[...] + _bdot(u, wo_ref[...])


def _nsa_main(c31, q, brg, sel, ocmp, ks, vs_t, kw, vw_t, b0, b1, mssm, sgn, x, wn, wo, batch):
    n, qw = q.shape
    d = x.shape[1]
    s = n // batch
    nq = s // QT
    rows = lambda b, i: (b * nq + i, 0)
    kspec = pl.BlockSpec((1,) + ks.shape[1:], lambda b, i: (b, 0, 0, 0))
    vspec = pl.BlockSpec((1,) + vs_t.shape[1:], lambda b, i: (b, 0, 0, 0, 0))
    c2 = lambda b, i: (0, 0)
    c3 = lambda b, i: (0, 0, 0)
    return pl.pallas_call(
        _nsa_main_kernel, grid=(batch, nq),
        in_specs=[pl.BlockSpec(memory_space=pltpu.SMEM),
                  pl.BlockSpec((QT, qw), rows), pl.BlockSpec((QT, LANES), rows),
                  pl.BlockSpec((1,) + sel.shape[1:3] + (QT,), lambda b, i: (b, 0, 0, i)),
                  pl.BlockSpec((QT, qw), rows), kspec, vspec, kspec, vspec,
                  pl.BlockSpec(b0.shape, c3), pl.BlockSpec(b1.shape, c3),
                  pl.BlockSpec((QT, d), rows), pl.BlockSpec((QT, d), rows), pl.BlockSpec((QT, d), rows),
                  pl.BlockSpec(wn.shape, c2), pl.BlockSpec(wo.shape, c2)],
        out_specs=pl.BlockSpec((QT, d), rows), out_shape=jax.ShapeDtypeStruct((n, d), F32),
        compiler_params=_cparams(("parallel", "parallel")), name="nsa_main",
    )(c31, q, brg, sel, ocmp, ks, vs_t, kw, vw_t, b0, b1, mssm, sgn, x, wn, wo)


META_EXPERT, META_RANK, META_GATE = 0, TOP_K, 2 * TOP_K


def _pack_bf16_pair(x):
    c = x.shape[1] // 2
    hi = pltpu.bitcast(x[:, :c].astype(BF16).astype(F32), jnp.int32)
    lo = pltpu.bitcast(x[:, c:].astype(BF16).astype(F32), jnp.int32)
    return hi | lax.shift_right_logical(lo, 16)


def _unpack_bf16_pair(w):
    hi = pltpu.bitcast(w & jnp.int32(-65536), F32).astype(BF16)
    lo = pltpu.bitcast(lax.shift_left(w, 16), F32).astype(BF16)
    return hi, lo


def _router_kernel(h_ref, nw_ref, wr_ref, br_ref, tri_ref, hn_ref, meta_ref, cnt_ref, base_s, *, n_real):
    i = pl.program_id(0)
    tm = h_ref.shape[0]

    @pl.when(i == 0)
    def _():
        base_s[...] = jnp.zeros_like(base_s)

    hn = _rms_rows(h_ref[...], nw_ref[...])
    hn_ref[...] = _pack_bf16_pair(hn)
    a_hi = hn.astype(BF16)
    a_lo = (hn - a_hi.astype(F32)).astype(BF16)
    w = wr_ref[...]
    w_hi = w.astype(BF16)
    w_lo = (w - w_hi.astype(F32)).astype(BF16)
    d = lambda a, b: jnp.dot(a, b, preferred_element_type=F32)
    v = (d(a_hi, w_hi) + (d(a_hi, w_lo) + d(a_lo, w_hi))) + br_ref[...]
    lane = lax.broadcasted_iota(jnp.int32, v.shape, 1)
    tops, idxs, hots = [], [], []
    for _ in range(TOP_K):
        m = jnp.max(v, axis=-1, keepdims=True)
        idx = jnp.min(jnp.where(v == m, lane, LANES), axis=-1, keepdims=True)
        hot = lane == idx
        tops.append(m)
        idxs.append(idx)
        hots.append(hot)
        v = jnp.where(hot, NEG, v)
    es = [jnp.exp(t - tops[0]) for t in tops]
    inv = 1.0 / functools.reduce(lambda a, b: a + b, es)
    row = i * tm + lax.broadcasted_iota(jnp.int32, v.shape, 0)
    onehot = jnp.zeros(v.shape, F32)
    for hot in hots:
        onehot = onehot + jnp.where(hot & (row < n_real), 1.0, 0.0)
    before = jnp.dot(tri_ref[...], onehot.astype(BF16), preferred_element_type=F32) + base_s[0:1, :]
    meta = jnp.zeros(v.shape, F32)
    for k in range(TOP_K):
        rank = jnp.sum(jnp.where(hots[k], before, 0.0), axis=-1, keepdims=True)
        meta = meta + jnp.where(lane == META_EXPERT + k, idxs[k].astype(F32), 0.0)
        meta = meta + jnp.where(lane == META_RANK + k, rank, 0.0)
        meta = meta + jnp.where(lane == META_GATE + k, es[k] * inv, 0.0)
    meta_ref[...] = meta
    base_s[0:1, :] = base_s[0:1, :] + jnp.sum(onehot, axis=0, keepdims=True)
    cnt_ref[...] = jnp.broadcast_to(base_s[0:1, :], cnt_ref.shape)


def _router(h, nw, wr, br, tm, n_real):
    n, d = h.shape
    c2 = lambda i: (0, 0)
    tri = jnp.asarray(np.tril(np.ones((tm, tm), np.float32), -1), BF16)
    return pl.pallas_call(
        functools.partial(_router_kernel, n_real=n_real), grid=(n // tm,),
        in_specs=[pl.BlockSpec((tm, d), lambda i: (i, 0)), pl.BlockSpec(nw.shape, c2),
                  pl.BlockSpec(wr.shape, c2), pl.BlockSpec(br.shape, c2), pl.BlockSpec(tri.shape, c2)],
        out_specs=[pl.BlockSpec((tm, d // 2), lambda i: (i, 0)), pl.BlockSpec((tm, LANES), lambda i: (i, 0)),
                   pl.BlockSpec((SUBLANES, LANES), c2)],
        out_shape=[jax.ShapeDtypeStruct((n, d // 2), jnp.int32), jax.ShapeDtypeStruct((n, LANES), F32),
                   jax.ShapeDtypeStruct((SUBLANES, LANES), F32)],
        scratch_shapes=[pltpu.VMEM((SUBLANES, LANES), F32)],
        compiler_params=_cparams(("arbitrary",)), name="moe_router",
    )(h, nw, wr, br, tri)


def _moe_group_kernel(sp_ref, x_ref, w1_ref, b1_ref, w2_ref, b2_ref, y_ref, w1_s, w2_s):
    t = pl.program_id(0)
    n_active = sp_ref[pl.num_programs(0)]
    new_expert = (t == 0) | (sp_ref[t] != sp_ref[jnp.maximum(t - 1, 0)])

    @pl.when(new_expert & (t < n_active))
    def _():
        w1_s[...] = w1_ref[0].astype(BF16)
        w2_s[...] = w2_ref[0].astype(BF16)

    @pl.when(t < n_active)
    def _():
        d_ff = w2_s.shape[0]
        x_a, x_b = _unpack_bf16_pair(x_ref[...])
        half = x_a.shape[1]
        y1 = (jnp.dot(x_a, w1_s[:half, :], preferred_element_type=F32)
              + jnp.dot(x_b, w1_s[half:, :], preferred_element_type=F32)) + b1_ref[0]
        gt = jnp.minimum(y1[:, :d_ff], SWIGLU_LIMIT)
        up = jnp.clip(y1[:, d_ff:], -SWIGLU_LIMIT, SWIGLU_LIMIT)
        act = (up + 1.0) * gt * _sigmoid(SWIGLU_ALPHA * gt)
        y_ref[...] = _bdot(act, w2_s[...]) + b2_ref[0]

    @pl.when(t >= n_active)
    def _():
        y_ref[...] = jnp.zeros_like(y_ref)


def _moe_group(sp, x_sorted, w1, b1, w2, b2, tm):
    p, dx = x_sorted.shape
    d = w2.shape[2]
    rows = lambda t, sp: (t, 0)
    per_e = lambda t, sp: (sp[t], 0, 0)
    grid_spec = pltpu.PrefetchScalarGridSpec(
        num_scalar_prefetch=1, grid=(p // tm,),
        in_specs=[pl.BlockSpec((tm, dx), rows),
                  pl.BlockSpec((1,) + w1.shape[1:], per_e), pl.BlockSpec((1,) + b1.shape[1:], per_e),
                  pl.BlockSpec((1,) + w2.shape[1:], per_e), pl.BlockSpec((1,) + b2.shape[1:], per_e)],
        out_specs=pl.BlockSpec((tm, d), rows),
        scratch_shapes=[pltpu.VMEM(w1.shape[1:], BF16), pltpu.VMEM(w2.shape[1:], BF16)])
    return pl.pallas_call(
        _moe_group_kernel, grid_spec=grid_spec, out_shape=jax.ShapeDtypeStruct((p, d), F32),
        compiler_params=_cparams(("arbitrary",)), name="moe_experts",
    )(sp, x_sorted, w1, b1, w2, b2)


def _moe_combine_kernel(h_ref, yk_ref, meta_ref, o_ref):
    d = h_ref.shape[1]
    meta = meta_ref[...]
    kk = lax.broadcasted_iota(jnp.int32, (LANES, LANES), 0)
    acc = h_ref[...]
    for k in range(TOP_K):
        pick = jnp.where(kk == META_GATE + k, 1.0, 0.0).astype(BF16)
        g = _dot3(meta, pick)
        acc = acc + jnp.concatenate([g] * (d // LANES), axis=1) * yk_ref[k]
    o_ref[...] = acc


def _moe_combine(h, yk, meta, tm):
    n, d = h.shape
    rows = lambda i: (i, 0)
    return pl.pallas_call(
        _moe_combine_kernel, grid=(n // tm,),
        in_specs=[pl.BlockSpec((tm, d), rows), pl.BlockSpec((TOP_K, tm, d), lambda i: (0, i, 0)),
                  pl.BlockSpec((tm, LANES), rows)],
        out_specs=pl.BlockSpec((tm, d), rows), out_shape=jax.ShapeDtypeStruct((n, d), F32),
        compiler_params=_cparams(("parallel",)), name="moe_combine",
    )(h, yk, meta)


def _mamba_prep_kernel(xbc_ref, sconv_ref, dt_ref, cw_ref, cb_ref, hp_ref, e_ref,
                       conv_ref, xs_ref, bm_ref, ct_ref, dtx_ref, dec_ref):
    cdim = xbc_ref.shape[1]
    d_inner = xs_ref.shape[1]
    xb = xbc_ref[...]
    u = cb_ref[...] + cw_ref[D_CONV - 1:D_CONV, :] * xb
    for k in range(D_CONV - 1):
        u = u + cw_ref[k:k + 1, :] * sconv_ref[:, k * cdim:(k + 1) * cdim]
    conv_ref[:, :(D_CONV - 2) * cdim] = sconv_ref[:, cdim:]
    conv_ref[:, (D_CONV - 2) * cdim:] = xb
    u = u * _sigmoid(u)
    xs = u[:, :d_inner]
    xs_ref[...] = xs
    bm_ref[...] = u[:, d_inner:d_inner + N_SSM_GROUPS * D_STATE]
    cm = u[:, d_inner + N_SSM_GROUPS * D_STATE:]
    for g in range(N_SSM_GROUPS):
        ct_ref[g] = cm[:, g * D_STATE:(g + 1) * D_STATE].T
    dtv = _softplus(dt_ref[...] + hp_ref[0:1, :])
    dec_ref[...] = jnp.exp(dtv * (-jnp.exp(hp_ref[1:2, :])))
    dtx_ref[...] = (xs * _dot3(dtv, e_ref[...])).T


def _mamba_prep(xbc, sconv, dt, conv_w, conv_b, headp, expand):
    n, cdim = xbc.shape
    d_inner = expand.shape[1]
    gn = N_SSM_GROUPS * D_STATE
    out_shape = [jax.ShapeDtypeStruct(sconv.shape, F32), jax.ShapeDtypeStruct((n, d_inner), F32),
                 jax.ShapeDtypeStruct((n, gn), F32), jax.ShapeDtypeStruct((N_SSM_GROUPS, D_STATE, n), F32),
                 jax.ShapeDtypeStruct((d_inner, n), F32), jax.ShapeDtypeStruct((n, LANES), F32)]
    return pl.pallas_call(_mamba_prep_kernel, out_shape=out_shape,
                          compiler_params=pltpu.CompilerParams(vmem_limit_bytes=VMEM_LIMIT), name="mamba_prep",
                          )(xbc, sconv, dt, conv_w, conv_b, headp, expand)


def _bf16x3(a, b):
    a_hi = a.astype(BF16)
    a_lo = (a - a_hi.astype(F32)).astype(BF16)
    b_hi = b.astype(BF16)
    b_lo = (b - b_hi.astype(F32)).astype(BF16)
    d = lambda x, y: jnp.dot(x, y, preferred_element_type=F32)
    return d(a_hi, b_hi) + (d(a_hi, b_lo) + d(a_lo, b_hi))


def _mamba_state_kernel(dec_ref, h0_ref, dtx_ref, bm_ref, ct_ref, hn_ref, yt_ref):
    s = pl.program_id(0)
    n = bm_ref.shape[0]
    d_inner = dtx_ref.shape[0]
    gw = d_inner // N_SSM_GROUPS
    hpg = gw // SSM_HEAD_DIM

    @pl.when(s == 0)
    def _():
        yt_ref[...] = jnp.zeros_like(yt_ref)

    row_is_s = lax.broadcasted_iota(jnp.int32, (n, D_STATE), 0) == s
    col_is_s = lax.broadcasted_iota(jnp.int32, (D_STATE, n), 1) == s
    groups = range(N_SSM_GROUPS)
    sts = []
    for g in groups:
        b_s = jnp.where(row_is_s, bm_ref[:, g * D_STATE:(g + 1) * D_STATE], 0.0)
        sts.append(_bf16x3(dtx_ref[g * gw:(g + 1) * gw, :], b_s))
    hns = []
    for g in groups:
        parts = []
        for r in range(hpg):
            h = g * hpg + r
            rows = slice(h * SSM_HEAD_DIM, (h + 1) * SSM_HEAD_DIM)
            parts.append(h0_ref[0, rows, :] * dec_ref[s, h] + sts[g][r * SSM_HEAD_DIM:(r + 1) * SSM_HEAD_DIM, :])
        hns.append(jnp.concatenate(parts, axis=0))
        hn_ref[0, g * gw:(g + 1) * gw, :] = hns[g]
    for g in groups:
        c_s = jnp.where(col_is_s, ct_ref[g], 0.0)
        yt_ref[g * gw:(g + 1) * gw, :] += _bdot(hns[g], c_s)


def _mamba_state(dec, h0, dtx_t, bm, ct):
    n, rows, ns = h0.shape
    c2 = lambda s: (0, 0)
    return pl.pallas_call(
        _mamba_state_kernel, grid=(n,),
        in_specs=[pl.BlockSpec(memory_space=pltpu.SMEM),
                  pl.BlockSpec((1, rows, ns), lambda s: (s, 0, 0)), pl.BlockSpec(dtx_t.shape, c2),
                  pl.BlockSpec(bm.shape, c2), pl.BlockSpec(ct.shape, lambda s: (0, 0, 0))],
        out_specs=[pl.BlockSpec((1, rows, ns), lambda s: (s, 0, 0)), pl.BlockSpec(dtx_t.shape, c2)],
        out_shape=[jax.ShapeDtypeStruct(h0.shape, F32), jax.ShapeDtypeStruct(dtx_t.shape, F32)],
        compiler_params=_cparams(("arbitrary",)), name="mamba_state",
    )(dec, h0, dtx_t, bm, ct)


def _page_cmp_kernel(pt_ref, cache_ref, new_ref, w_ref, pec_ref, deint_ref, ones_ref, nw_ref, out_ref, buf, rows_s, sem):
    t = pl.program_id(0)
    nsteps = pl.num_programs(0)
    n_pages = pt_ref.shape[1]
    page = buf.shape[4]
    nstr = out_ref.shape[2]
    half = out_ref.shape[3]
    nch = half // LANES
    spp = page // CMP_STRIDE
    past_str = n_pages * spp

    def copies(step, slot):
        seq, kv = step // 2, step % 2
        return [pltpu.make_async_copy(cache_ref.at[pt_ref[seq, p], pl.ds(nch * kv, nch)], buf.at[slot, p], sem.at[slot])
                for p in range(n_pages)]

    @pl.when(t == 0)
    def _():
        rows_s[:, past_str:, :] = jnp.zeros((CMP_STRIDE, nstr - past_str, half), F32)
        for cp in copies(t, 0):
            cp.start()

    slot = t % 2

    @pl.when(t + 1 < nsteps)
    def _():
        for cp in copies(t + 1, 1 - slot):
            cp.start()

    for cp in copies(t, slot):
        cp.wait()

    kv = t % 2
    deint = deint_ref[...]

    pages_per_trip = math.gcd(n_pages, 8)

    def to_rows(trip, carry):
        for pp in range(pages_per_trip):
            p = trip * pages_per_trip + pp
            xr = _bdot_nt(deint, buf[slot, p].reshape(nch * LANES, page))
            for i in range(CMP_STRIDE):
                rows_s[i, pl.ds(pl.multiple_of(p * spp, spp), spp), :] = xr[i * spp:(i + 1) * spp, :]
        return carry

    lax.fori_loop(0, n_pages // pages_per_trip, to_rows, 0)
    new = new_ref[0]
    rows_s[0, past_str:past_str + 1, :] = jnp.where(kv == 0, new[:, :half], new[:, half:])

    xcat = jnp.concatenate([rows_s[i].astype(BF16) for i in range(CMP_STRIDE)], axis=1)
    w_lo = w_ref[0, :CMP_STRIDE].reshape(CMP_STRIDE * half, half)
    w_hi = w_ref[0, CMP_STRIDE:].reshape(CMP_STRIDE * half, half)
    lo = jnp.dot(xcat, w_lo, preferred_element_type=F32) + pec_ref[0, 0:1, :]
    hi = jnp.dot(xcat, w_hi, preferred_element_type=F32) + pec_ref[0, 1:2, :]
    tok = lo + pltpu.roll(hi, nstr - 1, axis=0)
    out_ref[0, 0] = jnp.where(kv == 0, _head_rms(tok, ones_ref[...], nw_ref[...], NSA_HEAD_DIM), tok)


def _cmp_const_kernel(pe_ref, w_ref, out_ref):
    for kv in range(2):
        lo = hi = None
        for i in range(CMP_STRIDE):
            j = CMP_STRIDE + i
            a = _dot3(_rows8(pe_ref[kv, i:i + 1, :]), w_ref[kv, i])
            b = _dot3(_rows8(pe_ref[kv, j:j + 1, :]), w_ref[kv, j])
            lo = a if lo is None else lo + a
            hi = b if hi is None else hi + b
        out_ref[kv] = jnp.concatenate([lo[0:1, :], hi[0:1, :], jnp.zeros((SUBLANES - 2, lo.shape[1]), F32)], axis=0)


def _cmp_const(pe_kv, w_kv):
    return pl.pallas_call(_cmp_const_kernel, out_shape=jax.ShapeDtypeStruct((2, SUBLANES, pe_kv.shape[2]), F32),
                          compiler_params=pltpu.CompilerParams(vmem_limit_bytes=VMEM_LIMIT), name="cmp_const",
                          )(pe_kv, w_kv)


def _page_cmp(page_table, cache_t, new_rows, w_kv, pe_const, ones64, nw, nstr):
    nseq, n_pages = page_table.shape
    page = cache_t.shape[3]
    half = w_kv.shape[2]
    nch = half // LANES
    spp = page // CMP_STRIDE
    deint = np.zeros((page, page), np.float32)
    for i in range(CMP_STRIDE):
        for k in range(spp):
            deint[i * spp + k, CMP_STRIDE * k + i] = 1.0
    deint = jnp.asarray(deint, BF16)
    c2 = lambda t, pt: (0, 0)
    grid_spec = pltpu.PrefetchScalarGridSpec(
        num_scalar_prefetch=1, grid=(2 * nseq,),
        in_specs=[pl.BlockSpec(memory_space=pl.ANY),
                  pl.BlockSpec((1, 1, 2 * half), lambda t, pt: (t // 2, 0, 0)),
                  pl.BlockSpec((1,) + w_kv.shape[1:], lambda t, pt: (t % 2, 0, 0, 0)),
                  pl.BlockSpec((1,) + pe_const.shape[1:], lambda t, pt: (t % 2, 0, 0)),
                  pl.BlockSpec(deint.shape, c2), pl.BlockSpec(ones64.shape, c2), pl.BlockSpec(nw.shape, c2)],
        out_specs=pl.BlockSpec((1, 1, nstr, half), lambda t, pt: (t // 2, t % 2, 0, 0)),
        scratch_shapes=[pltpu.VMEM((2, n_pages, nch, LANES, page), F32),
                        pltpu.VMEM((CMP_STRIDE, nstr, half), F32),
                        pltpu.SemaphoreType.DMA((2,))])
    return pl.pallas_call(
        _page_cmp_kernel, grid_spec=grid_spec, out_shape=jax.ShapeDtypeStruct((nseq, 2, nstr, half), F32),
        compiler_params=_cparams(("arbitrary",)), name="page_cmp",
    )(page_table, cache_t, new_rows, w_kv, pe_const, deint, ones64, nw)


def _rows8(x):
    return jnp.broadcast_to(x, (SUBLANES, x.shape[1]))


def _group_q(q_row, g):
    parts = [q_row[:, (g * Q_PER_KV + r) * NSA_HEAD_DIM:(g * Q_PER_KV + r + 1) * NSA_HEAD_DIM]
             for r in range(Q_PER_KV)]
    parts.append(jnp.zeros((SUBLANES - Q_PER_KV, NSA_HEAD_DIM), F32))
    return jnp.concatenate(parts, axis=0)


def _heads_to_row(o):
    return jnp.concatenate([o[r:r + 1, :] for r in range(Q_PER_KV)], axis=1)


def _softmax_rows(s, keep):
    s = jnp.where(keep, s, NEG)
    m = jnp.max(s, axis=-1, keepdims=True)
    p = jnp.where(keep, jnp.exp2(s - m), 0.0)
    l = jnp.sum(p, axis=-1, keepdims=True)
    return p * jnp.where(l > 0.0, 1.0 / l, 0.0)


def _sample_cw_kernel(q_ref, brg_ref, kvc_ref, win_ref, wnew_ref, tbc_ref, tbw_ref, gexp_ref,
                      o_ref, sel_ref, wout_ref, wall_s, *, tq, past_w):
    nstr = kvc_ref.shape[2]
    half = kvc_ref.shape[3]
    wlen = win_ref.shape[1]
    nsel_pad = sel_ref.shape[2]
    wrows = wall_s.shape[0]
    q_row = q_ref[0]
    wall_s[0:wlen, :] = win_ref[0]
    wall_s[wlen:wlen + 1, :] = wnew_ref[0]
    wall_s[wlen + 1:, :] = jnp.zeros((wrows - wlen - 1, wall_s.shape[1]), F32)
    wout_ref[0] = wall_s[1:wlen + 1, :]

    nn = lax.broadcasted_iota(jnp.int32, (SUBLANES, nstr), 1)
    keep_c = (CMP_STRIDE * nn + (CMP_BLOCK - 1)) <= tq
    wi = lax.broadcasted_iota(jnp.int32, (SUBLANES, wrows), 1)
    dw = tq - (past_w + wi)
    keep_w = (dw >= 0) & (dw < WINDOW) & (past_w + wi >= 0) & (wi <= wlen)
    jb = lax.broadcasted_iota(jnp.int32, (nsel_pad, nstr), 0) * SEL_BLOCK
    cs = lax.broadcasted_iota(jnp.int32, (nsel_pad, nstr), 1) * CMP_STRIDE
    cover_t = jnp.where((cs < jb + SEL_BLOCK) & (cs + CMP_BLOCK > jb), 1.0, 0.0).astype(BF16)
    ji = lax.broadcasted_iota(jnp.int32, (nsel_pad, LANES), 0)
    valid = ji * SEL_BLOCK <= tq
    cur = tq // SEL_BLOCK
    forced = valid & ((ji == 0) | (ji == cur) | (ji == cur - 1))
    ii = lax.broadcasted_iota(jnp.int32, (nsel_pad, nsel_pad), 0)
    jj = lax.broadcasted_iota(jnp.int32, (nsel_pad, nsel_pad), 1)

    sel_ref[0] = jnp.zeros(sel_ref.shape[1:], F32)
    groups = range(N_KV_HEADS)
    k_lanes = [slice(g * NSA_HEAD_DIM, (g + 1) * NSA_HEAD_DIM) for g in groups]
    v_lanes = [slice(half + g * NSA_HEAD_DIM, half + (g + 1) * NSA_HEAD_DIM) for g in groups]
    qgs = [_group_q(q_row, g) for g in groups]
    sc = [_bdot_nt(qgs[g], kvc_ref[0, 0, :, k_lanes[g]]) + tbc_ref[g] for g in groups]
    sw = [_bdot_nt(qgs[g], wall_s[:, k_lanes[g]]) + tbw_ref[g] for g in groups]
    pc = [_softmax_rows(s, keep_c) for s in sc]
    pw = [_softmax_rows(s, keep_w) for s in sw]
    oc = [_heads_to_row(_bdot(pc[g], kvc_ref[0, 1, :, k_lanes[g]])) for g in groups]
    ow = [_heads_to_row(_bdot(pw[g], wall_s[:, v_lanes[g]])) for g in groups]
    imps = [_dot3_nt_l(cover_t, jnp.broadcast_to(jnp.sum(pc[g][0:Q_PER_KV, :], axis=0, keepdims=True), (LANES, nstr)))
            for g in groups]
    for g in groups:
        v_col = jnp.where(forced, FORCE_SCORE, jnp.where(valid, imps[g], NEG))
        v_row = jnp.concatenate([v_col[k * LANES:(k + 1) * LANES, :].T for k in range(nsel_pad // LANES)], axis=1)
        a = jnp.broadcast_to(v_row[0:1, :], (nsel_pad, nsel_pad))
        b = jnp.concatenate([v_col] * (nsel_pad // LANES), axis=1)
        beats = jnp.where(ii < jj, jnp.where(b >= a, 1.0, 0.0), jnp.where(b > a, 1.0, 0.0))
        cnt = jnp.sum(beats, axis=0, keepdims=True)
        sel_ref[0, g:g + 1, :] = jnp.where(cnt < N_SEL_BLOCKS, 1.0, 0.0)
    gates = _rows8(brg_ref[0])
    o = (jnp.concatenate(oc, axis=1) * _dot3(gates, gexp_ref[0])[0:1, :]
         + jnp.concatenate(ow, axis=1) * _dot3(gates, gexp_ref[2])[0:1, :])
    o_ref[0] = o


def _sample_cw(q, brg, kvc, win, wnew, tbc, tbw, gexp, tq, past_w, nsel_pad):
    nseq = q.shape[0]
    qw = q.shape[2]
    wlen, ww = win.shape[1], win.shape[2]
    wrows = -(-(wlen + 1) // SUBLANES) * SUBLANES
    per3 = lambda s: (s, 0, 0)
    c3 = lambda s: (0, 0, 0)
    return pl.pallas_call(
        functools.partial(_sample_cw_kernel, tq=tq, past_w=past_w), grid=(nseq,),
        in_specs=[pl.BlockSpec((1, 1, qw), per3), pl.BlockSpec((1, 1, LANES), per3),
                  pl.BlockSpec((1,) + kvc.shape[1:], lambda s: (s, 0, 0, 0)),
                  pl.BlockSpec((1, wlen, ww), per3), pl.BlockSpec((1, 1, ww), per3),
                  pl.BlockSpec(tbc.shape, c3), pl.BlockSpec(tbw.shape, c3), pl.BlockSpec(gexp.shape, c3)],
        out_specs=[pl.BlockSpec((1, 1, qw), per3), pl.BlockSpec((1, SUBLANES, nsel_pad), per3),
                   pl.BlockSpec((1, wlen, ww), per3)],
        out_shape=[jax.ShapeDtypeStruct((nseq, 1, qw), F32), jax.ShapeDtypeStruct((nseq, SUBLANES, nsel_pad), F32),
                   jax.ShapeDtypeStruct((nseq, wlen, ww), F32)],
        scratch_shapes=[pltpu.VMEM((wrows, ww), F32)],
        compiler_params=_cparams(("parallel",)), name="sample_cmp_win",
    )(q, brg, kvc, win, wnew, tbc, tbw, gexp)


def _sample_sel_kernel(pg_ref, c31_ref, f0_ref, cache_ref, q_ref, brg_ref, snew_ref, code_ref, ocw_ref, tbl_ref, gexp_ref,
                       o_ref, buf, sem):
    s = pl.program_id(0)
    nseq = pl.num_programs(0)
    nblk = pg_ref.shape[1] // N_KV_HEADS
    page = cache_ref.shape[4]
    half = snew_ref.shape[2] // 2

    def copies(seq, slot):
        out = []
        for g in range(N_KV_HEADS):
            for k in range(nblk):
                out.append(pltpu.make_async_copy(cache_ref.at[pg_ref[seq, g * nblk + k], :, g],
                                                 buf.at[slot, :, g, :, pl.ds(k * page, page)], sem.at[slot]))
        return out

    @pl.when(s == 0)
    def _():
        for cp in copies(s, 0):
            cp.start()

    slot = s % 2

    @pl.when(s + 1 < nseq)
    def _():
        for cp in copies(s + 1, 1 - slot):
            cp.start()

    for cp in copies(s, slot):
        cp.wait()

    q_row = q_ref[0]
    new = snew_ref[0]
    groups = range(N_KV_HEADS)
    qgs = [_group_q(q_row, g) for g in groups]
    raw = [jnp.dot(qgs[g].astype(BF16), buf[slot, 0, g].astype(BF16), preferred_element_type=F32) for g in groups]
    stats = []
    for g in groups:
        heads = [g * Q_PER_KV + r for r in range(Q_PER_KV)]
        per_head = lambda ref: jnp.concatenate([jnp.full((1, 1), ref[h], F32) for h in heads]
                                               + [jnp.zeros((SUBLANES - Q_PER_KV, 1), F32)], axis=0)
        code = _rows8(code_ref[0, g:g + 1, :])
        near = jnp.concatenate([tbl_ref[g]] * nblk, axis=1)
        sc = jnp.where(code > 0.5, raw[g] + jnp.where(code > 1.5, near, per_head(c31_ref)), NEG)
        k_new = new[:, g * NSA_HEAD_DIM:(g + 1) * NSA_HEAD_DIM]
        s_new = jnp.sum(qgs[g] * _rows8(k_new), axis=-1, keepdims=True) + per_head(f0_ref)
        new_on = _rows8(code_ref[0, N_KV_HEADS + g:N_KV_HEADS + g + 1, 0:1]) > 0.5
        s_new = jnp.where(new_on, s_new, NEG)
        m = jnp.maximum(jnp.max(sc, axis=-1, keepdims=True), s_new)
        p = jnp.where(code > 0.5, jnp.exp2(sc - m), 0.0)
        p_new = jnp.where(new_on, jnp.exp2(s_new - m), 0.0)
        l = jnp.sum(p, axis=-1, keepdims=True) + p_new
        stats.append((p, p_new, jnp.where(l > 0.0, 1.0 / l, 0.0)))
    outs = []
    for g in groups:
        p, p_new, inv = stats[g]
        v_new = new[:, half + g * NSA_HEAD_DIM:half + (g + 1) * NSA_HEAD_DIM]
        o = _bdot_nt(p, buf[slot, 1, g]) + p_new * _rows8(v_new)
        outs.append(_heads_to_row(o * inv))
    gates = _dot3(_rows8(brg_ref[0]), gexp_ref[1])[0:1, :]
    o_ref[0] = ocw_ref[0] + jnp.concatenate(outs, axis=1) * gates


def _sample_sel(pages, c31, f0, cache_t, q, brg, snew, code, ocw, tbl, gexp):
    nseq, qw = q.shape[0], q.shape[2]
    nblk = pages.shape[1] // N_KV_HEADS
    page = cache_t.shape[4]
    per3 = lambda s, *_: (s, 0, 0)
    c3 = lambda s, *_: (0, 0, 0)
    grid_spec = pltpu.PrefetchScalarGridSpec(
        num_scalar_prefetch=1, grid=(nseq,),
        in_specs=[pl.BlockSpec(memory_space=pltpu.SMEM), pl.BlockSpec(memory_space=pltpu.SMEM),
                  pl.BlockSpec(memory_space=pl.ANY),
                  pl.BlockSpec((1, 1, qw), per3), pl.BlockSpec((1, 1, LANES), per3),
                  pl.BlockSpec((1, 1, snew.shape[2]), per3), pl.BlockSpec((1,) + code.shape[1:], per3),
                  pl.BlockSpec((1, 1, qw), per3), pl.BlockSpec(tbl.shape, c3), pl.BlockSpec(gexp.shape, c3)],
        out_specs=pl.BlockSpec((1, 1, qw), per3),
        scratch_shapes=[pltpu.VMEM((2, 2, N_KV_HEADS, NSA_HEAD_DIM, nblk * page), F32),
                        pltpu.SemaphoreType.DMA((2,))])
    return pl.pallas_call(
        _sample_sel_kernel, grid_spec=grid_spec, out_shape=jax.ShapeDtypeStruct((nseq, 1, qw), F32),
        compiler_params=_cparams(("arbitrary",)), name="sample_sel",
    )(pages, c31, f0, cache_t, q, brg, snew, code, ocw, tbl, gexp)


def _sample_merge_kernel(yt_ref, xs_ref, z_ref, sgs_ref, dsk_ref, nw_ref, onesg_ref, ws_ref, ynsa_ref, sgn_ref, x_ref,
                         wn_ref, wo_ref, h_ref):
    gw = onesg_ref.shape[0]
    y = yt_ref[...].T + dsk_ref[...] * xs_ref[...]
    zz = z_ref[...]
    y = y * (zz * _sigmoid(zz))
    ms = _seg_sum(y * y, onesg_ref[...]) * (1.0 / gw)
    y = (y * lax.rsqrt(ms + RMS_EPS)) * nw_ref[...]
    u = sgs_ref[...] * _bdot(y, ws_ref[...]) + sgn_ref[...] * _bdot(ynsa_ref[...], wn_ref[...])
    h_ref[...] = x_ref[...] + _bdot(u, wo_ref[...])


def _sample_merge(y_t, xs, z, sgs, dsk, nw, onesg, ws, ynsa, sgn, x, wn, wo):
    return pl.pallas_call(_sample_merge_kernel, out_shape=jax.ShapeDtypeStruct(x.shape, F32),
                          compiler_params=pltpu.CompilerParams(vmem_limit_bytes=VMEM_LIMIT), name="sample_merge",
                          )(y_t, xs, z, sgs, dsk, nw, onesg, ws, ynsa, sgn, x, wn, wo)


def _bucket_lut():
    n = np.arange(MAX_DISTANCE + 1)
    max_exact = N_BUCKETS // 2
    nf = np.maximum(n, 1).astype(np.float32)
    large = max_exact + (np.log(nf / max_exact) / math.log(MAX_DISTANCE / max_exact)
                         * (N_BUCKETS - max_exact)).astype(np.int32)
    return np.where(n < max_exact, n, np.minimum(large, N_BUCKETS - 1))


def _bias_of_dist(rel_bias, dist):
    lut = _bucket_lut()
    idx = lut[np.clip(dist, 0, MAX_DISTANCE)]
    b = jnp.moveaxis(rel_bias.astype(F32)[idx], -1, 0) * LOG2E
    return jnp.where(jnp.asarray(dist >= 0), b, 0.0)


def _block_diag(w, reps):
    n, d, e = w.shape
    eye = jnp.eye(reps, dtype=w.dtype)
    return jnp.einsum("ab,nde->nadbe", eye, w).reshape(n, reps * d, reps * e)


def _ones_blocks(size, seg):
    return jnp.asarray(np.kron(np.eye(size // seg), np.ones((seg, seg))), BF16)


def _pad_cols(w, width):
    return jnp.pad(w, ((0, 0), (0, width - w.shape[1])))


def _prep(p):
    d_model = p["w_in"].shape[1]
    d_inner = p["w_ssm_out"].shape[1]
    n_heads = d_inner // SSM_HEAD_DIM
    conv_dim = p["conv_w"].shape[2]
    qw = N_Q_HEADS * NSA_HEAD_DIM
    kvw = 2 * N_KV_HEADS * NSA_HEAD_DIM
    splits = (d_inner, conv_dim, n_heads, qw, kvw, kvw, kvw, 3 * N_Q_HEADS, d_model, d_model)
    offs = np.concatenate([[0], np.cumsum(splits)])
    w_in = p["w_in"][0]
    seg = lambda k: w_in[:, offs[k]:offs[k + 1]]
    bf = lambda a: a.astype(BF16)
    o = {}
    o["w_ssm_in"] = [bf(seg(0)), bf(seg(1)), bf(_pad_cols(seg(2), LANES))]
    o["w_nsa_in"] = [bf(seg(3)), bf(seg(4)), bf(seg(5)), bf(seg(6)), bf(_pad_cols(seg(7), LANES)), bf(seg(8)),
                     bf(seg(9))]
    o["norm_mix"] = p["norm_mix_w"][0][None, :]
    kv_half = kvw // 2
    head_w = jnp.zeros((SUBLANES, qw), F32)
    head_w = head_w.at[0].set(jnp.tile(p["q_norm_w"][0], N_Q_HEADS))
    head_w = head_w.at[1, :kv_half].set(jnp.tile(p["k_sel_norm_w"][0], N_KV_HEADS))
    head_w = head_w.at[2, :kv_half].set(jnp.tile(p["k_win_norm_w"][0], N_KV_HEADS))
    o["head_w"] = head_w
    o["ones64"] = _ones_blocks(kv_half, NSA_HEAD_DIM)
    o["conv_w"] = p["conv_w"][0]
    o["conv_b"] = p["conv_b"][0][None, :]
    headp = jnp.zeros((SUBLANES, LANES), F32)
    headp = headp.at[0, :n_heads].set(p["dt_bias"][0]).at[1, :n_heads].set(p["a_log"][0])
    o["headp"] = headp
    o["dsk"] = jnp.repeat(p["d_skip"][0], SSM_HEAD_DIM)[None, :]
    o["ssm_nw"] = p["ssm_norm_w"][0][None, :]
    expand = np.zeros((LANES, d_inner), np.float32)
    for h in range(n_heads):
        expand[h, h * SSM_HEAD_DIM:(h + 1) * SSM_HEAD_DIM] = 1.0
    o["expand"] = jnp.asarray(expand, BF16)
    o["tri"] = jnp.asarray(np.tril(np.ones((SSD_CHUNK, SSD_CHUNK), np.float32)), BF16)
    o["onesg"] = jnp.ones((d_inner // N_SSM_GROUPS,) * 2, BF16)
    o["w_ssm_out"] = bf(p["w_ssm_out"][0])
    o["cmp_wk"] = bf(_block_diag(p["cmp_w_k"][0], N_KV_HEADS))
    o["cmp_wv"] = bf(_block_diag(p["cmp_w_v"][0], N_KV_HEADS))
    o["cmp_pek"] = jnp.tile(p["cmp_pe_k"][0], (1, N_KV_HEADS))
    o["cmp_pev"] = jnp.tile(p["cmp_pe_v"][0], (1, N_KV_HEADS))
    o["kc_nw"] = jnp.tile(p["k_cmp_norm_w"][0], N_KV_HEADS)[None, :]
    rel = p["rel_bias"]
    band = 2 * CMP_STRIDE
    i = np.arange(QT)[:, None]
    c = np.arange(band)[None, :]
    d_band = i + CMP_STRIDE * (band // 2) - CMP_STRIDE * c - (CMP_BLOCK - 1)
    d_band = np.concatenate([d_band, np.full((QT, 1), MAX_DISTANCE)], axis=1)
    tb = _bias_of_dist(rel, d_band)
    o["cmp_tb"] = jnp.pad(tb, ((0, 0), (0, 0), (0, 2 * band - tb.shape[2])))
    jk = np.arange(QT)[:, None]
    iq = np.arange(QT)[None, :]
    o["b0"] = _bias_of_dist(rel, iq - jk)
    o["b1"] = _bias_of_dist(rel, QT + iq - jk)
    o["c31"] = rel[N_BUCKETS - 1].astype(F32) * LOG2E
    gexp = np.zeros((3, LANES, qw), np.float32)
    for h in range(N_Q_HEADS):
        for k in range(3):
            gexp[k, 3 * h + k, h * NSA_HEAD_DIM:(h + 1) * NSA_HEAD_DIM] = 1.0
    o["gexp"] = jnp.asarray(gexp, BF16)
    o["w_nsa_out"] = bf(p["w_nsa_out"][0])
    o["w_out"] = bf(p["w_out"][0])
    o["norm_ffn"] = p["norm_ffn_w"][0][None, :]
    ne = p["w_router"].shape[2]
    o["w_router"] = _pad_cols(p["w_router"][0], LANES)
    o["b_router"] = jnp.full((1, LANES), NEG, F32).at[0, :ne].set(p["b_router"][0])
    o["w1"] = p["w_gate_up"][0]
    o["b1e"] = p["b_gate_up"][0][:, None, :]
    o["w2"] = p["w_down"][0]
    o["b2e"] = p["b_down"][0][:, None, :]
    return o


def _kv_layouts(kv, batch):
    n = kv.shape[0]
    s = n // batch
    half = kv.shape[1] // 2
    k = kv[:, :half].astype(BF16).reshape(batch, s, N_KV_HEADS, NSA_HEAD_DIM).transpose(0, 2, 1, 3)
    v = kv[:, half:].astype(BF16).reshape(batch, s // QT, QT, N_KV_HEADS, NSA_HEAD_DIM).transpose(0, 3, 1, 4, 2)
    extra = jnp.zeros(v.shape[:3] + (SUBLANES, QT), BF16).at[:, :, :, 0, :].set(1.0)
    return k, jnp.concatenate([v, extra], axis=3)


def _prompt_mixer(x, o, batch):
    z, xbc, dt = _proj(x, o["norm_mix"], o["ones64"], o["head_w"], o["w_ssm_in"], ["raw", "raw", "raw"], 256)
    q, kvc, kvs, kvw, brg, sg_ssm, sg_nsa = _proj(
        x, o["norm_mix"], o["ones64"], o["head_w"], o["w_nsa_in"], ["q", "raw", "ks", "kw", "sig", "sig", "sig"], 256)
    m_ssm, h_t = _ssd_prompt(xbc, z, dt, sg_ssm, o["conv_w"], o["conv_b"], o["headp"], o["dsk"], o["ssm_nw"],
                             o["expand"], o["tri"], o["onesg"], o["w_ssm_out"], batch)
    kc, vc = _cmp_build(kvc, o["cmp_wk"], o["cmp_wv"], o["cmp_pek"], o["cmp_pev"], o["ones64"], o["kc_nw"], batch)
    ocmp, sel = _cmp_attn(q, brg, kc, vc, o["cmp_tb"], o["gexp"], batch)
    ks, vs_t = _kv_layouts(kvs, batch)
    kw, vw_t = _kv_layouts(kvw, batch)
    h = _nsa_main(o["c31"], q, brg, sel, ocmp, ks, vs_t, kw, vw_t, o["b0"], o["b1"], m_ssm, sg_nsa, x,
                  o["w_nsa_out"], o["w_out"], batch)
    return h, (kvc, kvs, kvw, h_t, xbc)


MOE_TOKEN_TILE = 384
MOE_GROUP_TILE = 512


def _moe(h_all, o):
    n, d = h_all.shape
    tm, tg = MOE_TOKEN_TILE, MOE_GROUP_TILE
    ne = o["w1"].shape[0]
    n_pad = -(-n // tm) * tm
    hp = jnp.pad(h_all, ((0, n_pad - n), (0, 0))) if n_pad != n else h_all
    hn, meta, cnt = _router(hp, o["norm_ffn"], o["w_router"], o["b_router"], tm, n)
    eid = meta[:n, META_EXPERT:META_EXPERT + TOP_K].astype(jnp.int32)
    rank = meta[:n, META_RANK:META_RANK + TOP_K].astype(jnp.int32)
    count = cnt[0, :ne].astype(jnp.int32)
    tiles = (count + tg - 1) // tg
    tile_end = jnp.cumsum(tiles)
    pos = (tile_end - tiles)[eid] * tg + rank
    n_tiles = -(-(n * TOP_K) // tg) + ne
    tile_expert = jnp.minimum(jnp.sum(tile_end[None, :] <= jnp.arange(n_tiles)[:, None], axis=1), ne - 1)
    sp = jnp.concatenate([tile_expert, tile_end[-1:]]).astype(jnp.int32)
    src = jnp.zeros((n_tiles * tg,), jnp.int32).at[pos.reshape(-1)].set(
        jnp.arange(n * TOP_K, dtype=jnp.int32) // TOP_K)
    y_sorted = _moe_group(sp, hn[src], o["w1"], o["b1e"], o["w2"], o["b2e"], tg)
    yk = y_sorted[pos.T.reshape(-1)].reshape(TOP_K, n, d)
    if n_pad != n:
        yk = jnp.pad(yk, ((0, 0), (0, n_pad - n), (0, 0)))
    return _moe_combine(hp, yk, meta, tm)[:n]


def kernel(x_prompt, x_sample, cache_cmp, cache_sel, cache_win, state_ssm, state_conv, page_table, norm_mix_w, w_in,
           conv_w, conv_b, dt_bias, a_log, d_skip, ssm_norm_w, w_ssm_out, q_norm_w, k_cmp_norm_w, k_sel_norm_w,
           k_win_norm_w, cmp_pe_k, cmp_w_k, cmp_pe_v, cmp_w_v, rel_bias, w_nsa_out, w_out, norm_ffn_w, w_router,
           b_router, w_gate_up, b_gate_up, w_down, b_down):
    params = dict(norm_mix_w=norm_mix_w, w_in=w_in, conv_w=conv_w, conv_b=conv_b, dt_bias=dt_bias, a_log=a_log,
                  d_skip=d_skip, ssm_norm_w=ssm_norm_w, w_ssm_out=w_ssm_out, q_norm_w=q_norm_w,
                  k_cmp_norm_w=k_cmp_norm_w, k_sel_norm_w=k_sel_norm_w, k_win_norm_w=k_win_norm_w,
                  cmp_pe_k=cmp_pe_k, cmp_w_k=cmp_w_k, cmp_pe_v=cmp_pe_v, cmp_w_v=cmp_w_v, rel_bias=rel_bias,
                  w_nsa_out=w_nsa_out, w_out=w_out, norm_ffn_w=norm_ffn_w, w_router=w_router, b_router=b_router,
                  w_gate_up=w_gate_up, b_gate_up=b_gate_up, w_down=w_down, b_down=b_down)
    o = _prep(params)
    bsz, s, d = x_prompt.shape
    db, t, _ = x_sample.shape
    kvshape = (2, N_KV_HEADS, NSA_HEAD_DIM)
    hp, (kvc, kvs, kvw, h_t, xbc) = _prompt_mixer(x_prompt.reshape(bsz * s, d), o, bsz)
    assert t == 1, "the sample group decodes one token per sequence"
    hs, (kvc_s, kvs_s, win_s, h_s, conv_s) = _sample_mixer(
        x_sample.reshape(db, d), o, cache_cmp[0], cache_sel[0], cache_win[0], state_ssm[0], state_conv[0],
        page_table, rel_bias)
    y_all = _moe(jnp.concatenate([hp, hs], axis=0), o)
    wlen = min(WINDOW, s)
    n_heads = h_t.shape[2] // SSM_HEAD_DIM
    outs_p = (kvc.reshape((1, bsz, s) + kvshape), kvs.reshape((1, bsz, s) + kvshape),
              kvw.reshape((bsz, s) + kvshape)[None, :, s - wlen:],
              h_t.reshape(bsz, D_STATE, n_heads, SSM_HEAD_DIM).transpose(0, 2, 3, 1)[None],
              xbc.reshape(bsz, s, -1)[None, :, s - (D_CONV - 1):])
    outs_s = (kvc_s.reshape((1, db, t) + kvshape), kvs_s.reshape((1, db, t) + kvshape),
              win_s.reshape((1, db, win_s.shape[1]) + kvshape), h_s.reshape((1, db, n_heads, SSM_HEAD_DIM, D_STATE)),
              conv_s.reshape(1, db, D_CONV - 1, -1))
    return (y_all[:bsz * s].reshape(bsz, s, d), y_all[bsz * s:].reshape(db, t, d)) + outs_p + outs_s


def _sample_mixer(x, o, cache_cmp, cache_sel, cache_win, state_ssm, state_conv, page_table, rel):
    n = x.shape[0]
    z, xbc, dt = _proj(x, o["norm_mix"], o["ones64"], o["head_w"], o["w_ssm_in"], ["raw", "raw", "raw"], n)
    q, kvc, kvs, kvw, brg, sg_ssm, sg_nsa = _proj(
        x, o["norm_mix"], o["ones64"], o["head_w"], o["w_nsa_in"], ["q", "raw", "ks", "kw", "sig", "sig", "sig"], n)
    d_inner = z.shape[1]
    n_heads = d_inner // SSM_HEAD_DIM
    conv_new, xs, bm, ct, dtx_t, dec = _mamba_prep(xbc, state_conv.reshape(n, -1), dt, o["conv_w"], o["conv_b"],
                                                   o["headp"], o["expand"])
    h_new, y_t = _mamba_state(dec[:, :n_heads], state_ssm.reshape(n, d_inner, D_STATE), dtx_t, bm, ct)
    pool, page = cache_cmp.shape[0], cache_cmp.shape[1]
    n_pages = page_table.shape[1]
    past = n_pages * page
    tq = past
    assert page >= MAX_DISTANCE and page % SEL_BLOCK == 0 and past % CMP_STRIDE == 0
    half = kvc.shape[1] // 2
    nch = half // LANES
    cmp_t = jnp.transpose(cache_cmp, (0, 2, 3, 4, 1)).reshape(pool, 2 * nch, LANES, page)
    sel_t = jnp.transpose(cache_sel, (0, 2, 3, 4, 1))
    nstr = -(-(past // CMP_STRIDE + 1) // SUBLANES) * SUBLANES
    w_kv = jnp.stack([o["cmp_wk"], o["cmp_wv"]])
    pe_kv = jnp.stack([o["cmp_pek"], o["cmp_pev"]])
    tok = _page_cmp(page_table, cmp_t, kvc[:, None, :], w_kv, _cmp_const(pe_kv, w_kv), o["ones64"], o["kc_nw"], nstr)
    pad_heads = lambda b: jnp.pad(b.reshape(N_KV_HEADS, Q_PER_KV, -1), ((0, 0), (0, SUBLANES - Q_PER_KV), (0, 0)))
    tbc = pad_heads(_bias_of_dist(rel, tq - (CMP_STRIDE * np.arange(nstr) + CMP_BLOCK - 1)))
    wlen = cache_win.shape[1]
    wrows = -(-(wlen + 1) // SUBLANES) * SUBLANES
    past_w = past - wlen
    tbw = pad_heads(_bias_of_dist(rel, tq - (past_w + np.arange(wrows))))
    n_past_sel = past // SEL_BLOCK
    nsel = n_past_sel + 1
    nsel_pad = -(-nsel // LANES) * LANES
    o_cw, selmask, win_new = _sample_cw(q[:, None, :], brg[:, None, :], tok, cache_win.reshape(n, wlen, -1),
                                        kvw[:, None, :], tbc, tbw, o["gexp"], tq, past_w, nsel_pad)
    nblk = min(N_SEL_BLOCKS, nsel)
    picked = selmask[:, :N_KV_HEADS, :nsel] > 0.5
    order = jnp.cumsum(picked, axis=-1) - 1
    hit = picked[..., None] & (order[..., None] == jnp.arange(nblk))
    idx = jnp.sum(jnp.where(hit, jnp.arange(nsel)[:, None], 0), axis=2)
    is_past = idx < n_past_sel
    jp = jnp.minimum(idx, n_past_sel - 1)
    per_page = page // SEL_BLOCK
    pg = jp // per_page
    phys = jnp.take_along_axis(page_table, pg.reshape(n, -1), axis=1).astype(jnp.int32)
    lane_blk = (np.arange(page) // SEL_BLOCK)[None, None, None, :]
    attended = is_past[..., None] & (lane_blk == (jp % per_page)[..., None])
    code = jnp.where(attended, jnp.where((pg == n_pages - 1)[..., None], 2.0, 1.0), 0.0).reshape(n, N_KV_HEADS, -1)
    new_on = jnp.any(idx >= n_past_sel, axis=-1).astype(F32)
    code = jnp.concatenate([code, jnp.broadcast_to(new_on[..., None], code.shape)], axis=1)
    tbl = pad_heads(_bias_of_dist(rel, tq - ((n_pages - 1) * page + np.arange(page))))
    f0 = rel[_bucket_lut()[0]].astype(F32) * LOG2E
    y_nsa = _sample_sel(phys, o["c31"], f0, sel_t, q[:, None, :], brg[:, None, :], kvs[:, None, :], code, o_cw, tbl,
                        o["gexp"])
    h = _sample_merge(y_t, xs, z, sg_ssm, o["dsk"], o["ssm_nw"], o["onesg"], o["w_ssm_out"], y_nsa[:, 0], sg_nsa, x,
                      o["w_nsa_out"], o["w_out"])
    return h, (kvc, kvs, win_new, h_new, conv_new)
```

```python
import functools
import math

import jax
import jax.numpy as jnp
import numpy as np
from jax import lax
from jax.experimental import pallas as pl
from jax.experimental.pallas import tpu as pltpu

F32 = jnp.float32
BF16 = jnp.bfloat16

SSM_HEAD_DIM = 64
N_SSM_GROUPS = 4
D_STATE = 128
D_CONV = 4
SSD_CHUNK = 128
NSA_HEAD_DIM = 64
N_Q_HEADS = 16
N_KV_HEADS = 4
Q_PER_KV = N_Q_HEADS // N_KV_HEADS
CMP_BLOCK = 32
CMP_STRIDE = 16
SEL_BLOCK = 64
N_SEL_BLOCKS = 16
WINDOW = 512
N_BUCKETS = 32
MAX_DISTANCE = 128
TOP_K = 4
SWIGLU_LIMIT = 7.0
SWIGLU_ALPHA = 1.702
RMS_EPS = 1e-6
LOG2E = math.log2(math.e)
NEG = -1e30
FORCE_SCORE = 1e9

LANES = 128
SUBLANES = 8
QT = 128
FAR_TILES_PER_TRIP = 4
VMEM_LIMIT = 56 * 1024 * 1024


def _cparams(sem):
    return pltpu.CompilerParams(dimension_semantics=sem, vmem_limit_bytes=VMEM_LIMIT)


def _bdot(a, b):
    return jnp.dot(a.astype(BF16), b.astype(BF16), preferred_element_type=F32)


def _bdot_nt(a, b):
    return lax.dot_general(a.astype(BF16), b.astype(BF16), (((1,), (1,)), ((), ())),
                           preferred_element_type=F32)


def _split3(a):
    hi = a.astype(BF16)
    r = a - hi.astype(F32)
    mid = r.astype(BF16)
    lo = (r - mid.astype(F32)).astype(BF16)
    return hi, mid, lo


def _dot3(a, b):
    hi, mid, lo = _split3(a)
    d = lambda p: jnp.dot(p, b, preferred_element_type=F32)
    return (d(hi) + d(mid)) + d(lo)


def _dot2(a, b):
    hi = a.astype(BF16)
    mid = (a - hi.astype(F32)).astype(BF16)
    return jnp.dot(hi, b, preferred_element_type=F32) + jnp.dot(mid, b, preferred_element_type=F32)


def _dot3_l(a, b):
    hi, mid, lo = _split3(b)
    d = lambda p: jnp.dot(a, p, preferred_element_type=F32)
    return (d(hi) + d(mid)) + d(lo)


def _dot3_nt_l(a, b):
    hi, mid, lo = _split3(b)
    d = lambda p: lax.dot_general(a, p, (((1,), (1,)), ((), ())), preferred_element_type=F32)
    return (d(hi) + d(mid)) + d(lo)


def _seg_sum(y, ones_blk):
    c = ones_blk.shape[0]
    outs = []
    for k in range(y.shape[1] // c):
        outs.append(_dot3(y[:, k * c:(k + 1) * c], ones_blk))
    return outs[0] if len(outs) == 1 else jnp.concatenate(outs, axis=1)


def _sigmoid(x):
    return 1.0 / (1.0 + jnp.exp(-x))


def _rms_rows(x, w):
    ms = jnp.mean(x * x, axis=-1, keepdims=True)
    return (x * lax.rsqrt(ms + RMS_EPS)) * w


def _head_rms(y, ones64, w, seg):
    ms = _seg_sum(y * y, ones64) * (1.0 / seg)
    return (y * lax.rsqrt(ms + RMS_EPS)) * w


def _proj_kernel(kinds, x_ref, nw_ref, ones_ref, hw_ref, *refs):
    n = len(kinds)
    w_refs, o_refs = refs[:n], refs[n:]
    xn = _rms_rows(x_ref[...], nw_ref[...]).astype(BF16)
    ones64 = ones_ref[...]
    for kind, w_ref, o_ref in zip(kinds, w_refs, o_refs):
        y = jnp.dot(xn, w_ref[...], preferred_element_type=F32)
        if kind == "sig":
            y = _sigmoid(y)
        elif kind == "q":
            y = _head_rms(y, ones64, hw_ref[0:1, :], NSA_HEAD_DIM) * (NSA_HEAD_DIM ** -0.5 * LOG2E)
        elif kind in ("ks", "kw"):
            row = 1 if kind == "ks" else 2
            half = y.shape[1] // 2
            k = _head_rms(y[:, :half], ones64, hw_ref[row:row + 1, :half], NSA_HEAD_DIM)
            y = jnp.concatenate([k, y[:, half:]], axis=1)
        o_ref[...] = y


def _proj(x, norm_w, ones64, head_w, weights, kinds, tm):
    n, d = x.shape
    assert n % tm == 0
    const = lambda i: (0, 0)
    in_specs = [pl.BlockSpec((tm, d), lambda i: (i, 0)),
                pl.BlockSpec((1, d), const),
                pl.BlockSpec(ones64.shape, const),
                pl.BlockSpec(head_w.shape, const)]
    in_specs += [pl.BlockSpec(w.shape, const) for w in weights]
    out_specs = [pl.BlockSpec((tm, w.shape[1]), lambda i: (i, 0)) for w in weights]
    out_shape = [jax.ShapeDtypeStruct((n, w.shape[1]), F32) for w in weights]
    return pl.pallas_call(
        functools.partial(_proj_kernel, tuple(kinds)),
        grid=(n // tm,), in_specs=in_specs, out_specs=out_specs, out_shape=out_shape,
        compiler_params=_cparams(("parallel",)), name="in_proj",
    )(x, norm_w, ones64, head_w, *weights)


def _softplus(x):
    return jnp.maximum(x, 0.0) + jnp.log1p(jnp.exp(-jnp.abs(x)))


def _ssd_kernel(xbc_ref, z_ref, dt_ref, sg_ref, cw_ref, cb_ref, hp_ref, dsk_ref, nw_ref, e_ref, tri_ref,
                wout_ref, o_ref, ht_ref, ht_s, ext_s):
    c = pl.program_id(1)
    q = SSD_CHUNK
    d_inner = z_ref.shape[1]
    gw = d_inner // N_SSM_GROUPS
    hpg = gw // SSM_HEAD_DIM
    tail = SUBLANES

    @pl.when(c == 0)
    def _():
        ht_s[...] = jnp.zeros_like(ht_s)
        ext_s[0:tail, :] = jnp.zeros((tail, ext_s.shape[1]), F32)

    xb = xbc_ref[...]
    ext_s[tail:tail + q, :] = xb
    u = cb_ref[...] + cw_ref[D_CONV - 1:D_CONV, :] * xb
    for k in range(1, D_CONV):
        u = u + cw_ref[D_CONV - 1 - k:D_CONV - k, :] * ext_s[tail - k:tail - k + q, :]
    ext_s[0:tail, :] = xb[q - tail:q, :]
    u = u * _sigmoid(u)
    xs = u[:, :d_inner]
    bm = u[:, d_inner:d_inner + N_SSM_GROUPS * D_STATE]
    cm = u[:, d_inner + N_SSM_GROUPS * D_STATE:]

    dtv = _softplus(dt_ref[...] + hp_ref[0:1, :])
    a = dtv * (-jnp.exp(hp_ref[1:2, :]))
    acum = _dot3_l(tri_ref[...], a)
    acum_t = acum.T
    eacum = jnp.exp(acum)
    w_end = jnp.exp(acum[q - 1:q, :] - acum) * dtv
    e = e_ref[...]
    dt_x = _dot2(dtv, e)
    we_x = _dot2(w_end, e)
    ea_x = _dot2(eacum, e)
    xdt = (xs * dt_x).astype(BF16)
    xdtw = (xs * we_x).astype(BF16)
    ii = lax.broadcasted_iota(jnp.int32, (q, q), 0)
    jj = lax.broadcasted_iota(jnp.int32, (q, q), 1)
    causal = jj <= ii

    groups = range(N_SSM_GROUPS)
    gsl = [slice(g * gw, (g + 1) * gw) for g in groups]
    cgs = [cm[:, g * D_STATE:(g + 1) * D_STATE].astype(BF16) for g in groups]
    bgs = [bm[:, g * D_STATE:(g + 1) * D_STATE] for g in groups]
    cbms = [_bdot_nt(cgs[g], bgs[g]) for g in groups]
    hgs = [ht_s[:, gsl[g]] for g in groups]
    y_inter = [jnp.dot(cgs[g], hgs[g].astype(BF16), preferred_element_type=F32) for g in groups]
    sts = [jnp.dot(bgs[g].T.astype(BF16), xdtw[:, gsl[g]], preferred_element_type=F32) for g in groups]
    ys = []
    for g in groups:
        yh = []
        for r in range(hpg):
            h = g * hpg + r
            seg = acum[:, h:h + 1] - acum_t[h:h + 1, :]
            dec = jnp.exp(jnp.where(causal, seg, NEG))
            m = (cbms[g] * dec).astype(BF16)
            yh.append(jnp.dot(m, xdt[:, h * SSM_HEAD_DIM:(h + 1) * SSM_HEAD_DIM], preferred_element_type=F32))
        ys.append(y_inter[g] * ea_x[:, gsl[g]] + jnp.concatenate(yh, axis=1))
        ht_s[:, gsl[g]] = hgs[g] * ea_x[q - 1:q, gsl[g]] + sts[g]
    y = jnp.concatenate(ys, axis=1) + dsk_ref[...] * xs
    zz = z_ref[...]
    y = y * (zz * _sigmoid(zz))
    normed = []
    for g in groups:
        yg = y[:, gsl[g]]
        normed.append(yg * lax.rsqrt(jnp.mean(yg * yg, axis=-1, keepdims=True) + RMS_EPS))
    y = jnp.concatenate(normed, axis=1) * nw_ref[...]
    o_ref[...] = sg_ref[...] * _bdot(y, wout_ref[...])

    @pl.when(c == pl.num_programs(1) - 1)
    def _():
        ht_ref[0] = ht_s[...]


def _ssd_prompt(xbc, z, dt, sg, conv_w, conv_b, headp, dsk, nw, expand, tri, wout, batch):
    n, conv_dim = xbc.shape
    d_inner = z.shape[1]
    d_model = wout.shape[1]
    q = SSD_CHUNK
    nc = n // batch // q
    const = lambda b, c: (0, 0)
    rows = lambda b, c: (b * nc + c, 0)
    in_specs = [pl.BlockSpec((q, conv_dim), rows), pl.BlockSpec((q, d_inner), rows),
                pl.BlockSpec((q, LANES), rows), pl.BlockSpec((q, d_model), rows),
                pl.BlockSpec(conv_w.shape, const), pl.BlockSpec(conv_b.shape, const),
                pl.BlockSpec(headp.shape, const), pl.BlockSpec(dsk.shape, const),
                pl.BlockSpec(nw.shape, const), pl.BlockSpec(expand.shape, const),
                pl.BlockSpec(tri.shape, const), pl.BlockSpec(wout.shape, const)]
    out_specs = [pl.BlockSpec((q, d_model), rows),
                 pl.BlockSpec((1, D_STATE, d_inner), lambda b, c: (b, 0, 0))]
    out_shape = [jax.ShapeDtypeStruct((n, d_model), F32),
                 jax.ShapeDtypeStruct((batch, D_STATE, d_inner), F32)]
    return pl.pallas_call(
        _ssd_kernel, grid=(batch, nc), in_specs=in_specs, out_specs=out_specs, out_shape=out_shape,
        scratch_shapes=[pltpu.VMEM((D_STATE, d_inner), F32), pltpu.VMEM((SUBLANES + q, conv_dim), F32)],
        compiler_params=_cparams(("parallel", "arbitrary")), name="ssd_prompt",
    )(xbc, z, dt, sg, conv_w, conv_b, headp, dsk, nw, expand, tri, wout)


def _cmp_build_kernel(kvc_ref, wk_ref, wv_ref, pek_ref, pev_ref, ones_ref, nw_ref, kc_ref, vc_ref):
    ns = kc_ref.shape[0]
    kw = kc_ref.shape[1]
    lo_k = hi_k = lo_v = hi_v = None
    add = lambda acc, v: v if acc is None else acc + v
    nch = 2 * kw // LANES
    for i in range(CMP_STRIDE):
        x = jnp.concatenate([kvc_ref[pl.ds(nch * i + c, ns, stride=nch * CMP_STRIDE), :] for c in range(nch)], axis=1)
        xk, xv = x[:, :kw], x[:, kw:]
        j = CMP_STRIDE + i
        lo_k = add(lo_k, _bdot(xk + pek_ref[i:i + 1, :], wk_ref[i]))
        hi_k = add(hi_k, _bdot(xk + pek_ref[j:j + 1, :], wk_ref[j]))
        lo_v = add(lo_v, _bdot(xv + pev_ref[i:i + 1, :], wv_ref[i]))
        hi_v = add(hi_v, _bdot(xv + pev_ref[j:j + 1, :], wv_ref[j]))
    kc = lo_k + pltpu.roll(hi_k, ns - 1, axis=0)
    vc_ref[...] = lo_v + pltpu.roll(hi_v, ns - 1, axis=0)
    kc_ref[...] = _head_rms(kc, ones_ref[...], nw_ref[...], NSA_HEAD_DIM)


def _cmp_build(kvc, wk, wv, pek, pev, ones64, nw, batch):
    n, w = kvc.shape
    s = n // batch
    ns = s // CMP_STRIDE
    kw = w // 2
    c2 = lambda b: (0, 0)
    c3 = lambda b: (0, 0, 0)
    nch = w // LANES
    kvc = kvc.reshape(n * nch, LANES)
    return pl.pallas_call(
        _cmp_build_kernel, grid=(batch,),
        in_specs=[pl.BlockSpec((s * nch, LANES), lambda b: (b, 0)), pl.BlockSpec(wk.shape, c3),
                  pl.BlockSpec(wv.shape, c3),
                  pl.BlockSpec(pek.shape, c2), pl.BlockSpec(pev.shape, c2), pl.BlockSpec(ones64.shape, c2),
                  pl.BlockSpec(nw.shape, c2)],
        out_specs=[pl.BlockSpec((ns, kw), lambda b: (b, 0)), pl.BlockSpec((ns, kw), lambda b: (b, 0))],
        out_shape=[jax.ShapeDtypeStruct((batch * ns, kw), F32)] * 2,
        compiler_params=_cparams(("parallel",)), name="cmp_build",
    )(kvc, wk, wv, pek, pev, ones64, nw)


def _rank_select(v, n_keep):
    n = v.shape[0]
    rows = SUBLANES
    sub = lax.broadcasted_iota(jnp.int32, (rows, v.shape[1]), 0)
    chunks = [v[c * rows:(c + 1) * rows, :] for c in range(n // rows)]
    cnts = [jnp.zeros(ch.shape, F32) for ch in chunks]
    for i in range(n):
        ri = v[i:i + 1, :]
        for c, ch in enumerate(chunks):
            if c * rows > i:
                beats = jnp.where(ri >= ch, 1.0, 0.0)
            elif (c + 1) * rows - 1 <= i:
                beats = jnp.where(ri > ch, 1.0, 0.0)
            else:
                beats = jnp.where(sub > i - c * rows, jnp.where(ri >= ch, 1.0, 0.0), jnp.where(ri > ch, 1.0, 0.0))
            cnts[c] = cnts[c] + beats
    return jnp.where(jnp.concatenate(cnts, axis=0) < n_keep, 1.0, 0.0)


def _cmp_attn_kernel(q_ref, brg_ref, kc_ref, vc_ref, tb_ref, gexp_ref, ocmp_ref, sel_ref):
    qi = pl.program_id(1)
    t0 = qi * QT
    ns = kc_ref.shape[0]
    nsel = sel_ref.shape[2]
    nb = tb_ref.shape[2]
    band = 2 * CMP_STRIDE
    tt = t0 + lax.broadcasted_iota(jnp.int32, (QT, ns), 0)
    nn = lax.broadcasted_iota(jnp.int32, (QT, ns), 1)
    mask = (CMP_STRIDE * nn + (CMP_BLOCK - 1)) <= tt
    first = (QT // CMP_STRIDE) * qi - band // 2
    cc = lax.broadcasted_iota(jnp.int32, (nb, ns), 0)
    n2 = lax.broadcasted_iota(jnp.int32, (nb, ns), 1)
    shift = jnp.where(((cc < band) & (n2 == first + cc)) | ((cc == band) & (n2 < first)), 1.0, 0.0).astype(BF16)
    jb = lax.broadcasted_iota(jnp.int32, (nsel, ns), 0) * SEL_BLOCK
    cs = lax.broadcasted_iota(jnp.int32, (nsel, ns), 1) * CMP_STRIDE
    cover_t = jnp.where((cs < jb + SEL_BLOCK) & (cs + CMP_BLOCK > jb), 1.0, 0.0).astype(BF16)
    jidx = lax.broadcasted_iota(jnp.int32, (nsel, QT), 0)
    tq = t0 + lax.broadcasted_iota(jnp.int32, (nsel, QT), 1)
    valid = jidx * SEL_BLOCK <= tq
    cur = tq // SEL_BLOCK
    forced = valid & ((jidx == 0) | (jidx == cur) | (jidx == cur - 1))

    q = q_ref[...].astype(BF16)
    head_cols = lambda a, g: a[:, g * NSA_HEAD_DIM:(g + 1) * NSA_HEAD_DIM]
    kgs = [head_cols(kc_ref, g).astype(BF16) for g in range(N_KV_HEADS)]
    vgs = [head_cols(vc_ref, g).astype(BF16) for g in range(N_KV_HEADS)]
    raw = [_bdot_nt(head_cols(q, h), kgs[h // Q_PER_KV]) + _dot3(tb_ref[h], shift) for h in range(N_Q_HEADS)]
    ps = []
    for s in raw:
        s = jnp.where(mask, s, NEG)
        m = jnp.max(s, axis=-1, keepdims=True)
        p = jnp.where(mask, jnp.exp2(s - m), 0.0)
        l = jnp.sum(p, axis=-1, keepdims=True)
        ps.append(p * jnp.where(l > 0.0, 1.0 / l, 0.0))
    outs = [_bdot(ps[h], vgs[h // Q_PER_KV]) for h in range(N_Q_HEADS)]
    imps = []
    for g in range(N_KV_HEADS):
        psum = functools.reduce(lambda a, b: a + b, ps[g * Q_PER_KV:(g + 1) * Q_PER_KV])
        imps.append(_dot3_nt_l(cover_t, psum))
    for g in range(N_KV_HEADS):
        v = jnp.where(forced, FORCE_SCORE, jnp.where(valid, imps[g], NEG))
        sel_ref[0, g] = _rank_select(v, N_SEL_BLOCKS)
    ocmp_ref[...] = jnp.concatenate(outs, axis=1) * _dot3(brg_ref[...], gexp_ref[0])


def _cmp_attn(q, brg, kc, vc, tb, gexp, batch):
    n, qw = q.shape
    s = n // batch
    nq = s // QT
    ns, kw = kc.shape[0] // batch, kc.shape[1]
    nsel = s // SEL_BLOCK
    rows = lambda b, i: (b * nq + i, 0)
    per_b = lambda b, i: (b, 0)
    return pl.pallas_call(
        _cmp_attn_kernel, grid=(batch, nq),
        in_specs=[pl.BlockSpec((QT, qw), rows), pl.BlockSpec((QT, LANES), rows),
                  pl.BlockSpec((ns, kw), per_b), pl.BlockSpec((ns, kw), per_b),
                  pl.BlockSpec(tb.shape, lambda b, i: (0, 0, 0)),
                  pl.BlockSpec((1,) + gexp.shape[1:], lambda b, i: (0, 0, 0))],
        out_specs=[pl.BlockSpec((QT, qw), rows), pl.BlockSpec((1, N_KV_HEADS, nsel, QT), lambda b, i: (b, 0, 0, i))],
        out_shape=[jax.ShapeDtypeStruct((n, qw), F32), jax.ShapeDtypeStruct((batch, N_KV_HEADS, nsel, s), F32)],
        compiler_params=_cparams(("parallel", "parallel")), name="cmp_attn",
    )(q, brg, kc, vc, tb, gexp)


def _attn_round(states, raw, keeps, biases, vts, shifts):
    stats = []
    for st, s, keep, bias, shift in zip(states, raw, keeps, biases, shifts):
        s = _masked_scores(s, keep, bias)
        tile_max = jnp.max(s, axis=0, keepdims=True)
        if shift is not None:
            tile_max = tile_max + shift
        m_new = tile_max if st is None else jnp.maximum(st[0], tile_max)
        alpha = None if st is None else jnp.exp2(st[0] - m_new)
        p = jnp.exp2(s - (m_new if shift is None else m_new - shift)).astype(BF16)
        stats.append((m_new, alpha, p))
    out = []
    for st, (m_new, alpha, p), vt in zip(states, stats, vts):
        pv = jnp.dot(vt, p, preferred_element_type=F32)
        out.append((m_new, pv if st is None else alpha * st[1] + pv))
    return out


def _masked_scores(s, keep, bias=None):
    parts = []
    for r in range(s.shape[1] // QT):
        v = s[:, r * QT:(r + 1) * QT]
        parts.append(jnp.where(keep, v if bias is None else v + bias[r], NEG))
    return jnp.concatenate(parts, axis=1)


def _nsa_main_kernel(c31_ref, q_ref, brg_ref, sel_ref, ocmp_ref, ks_ref, vs_ref, kw_ref, vw_ref, b0_ref, b1_ref,
                     mssm_ref, sgn_ref, x_ref, wn_ref, wo_ref, h_ref):
    qi = pl.program_id(1)
    nwt = WINDOW // QT
    blk_per_tile = QT // SEL_BLOCK
    jk = lax.broadcasted_iota(jnp.int32, (QT, QT), 0)
    iq = lax.broadcasted_iota(jnp.int32, (QT, QT), 1)
    causal_t = jk <= iq
    q_t = q_ref[...].T.astype(BF16)
    gates_t = brg_ref[...].T

    def sel_keep(g, kt, tiles=1):
        rows = [jnp.broadcast_to(sel_ref[0, g, pl.ds(blk_per_tile * kt + b, 1), :], (SEL_BLOCK, QT))
                for b in range(blk_per_tile * tiles)]
        return jnp.concatenate(rows, axis=0) > 0.5

    def key_tile(ref, g, kt):
        return ref[0, g, pl.ds(pl.multiple_of(kt * QT, QT), QT), :]

    groups = range(N_KV_HEADS)
    heads = [[g * Q_PER_KV + r for r in range(Q_PER_KV)] for g in groups]
    qg = [jnp.concatenate([q_t[h * NSA_HEAD_DIM:(h + 1) * NSA_HEAD_DIM, :] for h in heads[g]], axis=1) for g in groups]
    far = [jnp.concatenate([jnp.full((1, QT), c31_ref[h], F32) for h in heads[g]], axis=1) for g in groups]
    near0 = [[b0_ref[h] for h in heads[g]] for g in groups]
    near1 = [[b1_ref[h] for h in heads[g]] for g in groups]
    scores = lambda ref, g, kt: jnp.dot(key_tile(ref, g, kt), qg[g], preferred_element_type=F32)
    prev = jnp.maximum(qi - 1, 0)
    has_prev = jnp.broadcast_to(qi >= 1, (QT, QT))

    none4 = [None] * N_KV_HEADS
    st = _attn_round(
        none4 + none4,
        [scores(ks_ref, g, qi) for g in groups] + [scores(kw_ref, g, qi) for g in groups],
        [causal_t & sel_keep(g, qi) for g in groups] + [causal_t] * N_KV_HEADS, near0 + near0,
        [vs_ref[0, g, qi] for g in groups] + [vw_ref[0, g, qi] for g in groups], none4 + none4)
    st = _attn_round(
        st, [scores(ks_ref, g, prev) for g in groups] + [scores(kw_ref, g, prev) for g in groups],
        [sel_keep(g, prev) & has_prev for g in groups] + [has_prev] * N_KV_HEADS, near1 + near1,
        [vs_ref[0, g, prev] for g in groups] + [vw_ref[0, g, prev] for g in groups], none4 + none4)
    sel, win = st[:N_KV_HEADS], st[N_KV_HEADS:]
    for back in range(2, nwt + 1):
        kt = jnp.maximum(qi - back, 0)
        keep = jnp.broadcast_to(qi >= back, (QT, QT))
        if back == nwt:
            keep = keep & (jk > iq)
        win = _attn_round(win, [scores(kw_ref, g, kt) for g in groups], [keep] * N_KV_HEADS, none4,
                          [vw_ref[0, g, kt] for g in groups], far)

    def sel_body(kt, state):
        return tuple(_attn_round(list(state), [scores(ks_ref, g, kt) for g in groups],
                                 [sel_keep(g, kt) for g in groups], none4, [vs_ref[0, g, kt] for g in groups], far))

    wide = FAR_TILES_PER_TRIP

    def sel_body_wide(j, state):
        kt = wide * j
        raw = [jnp.dot(ks_ref[0, g, pl.ds(pl.multiple_of(kt * QT, QT), wide * QT), :], qg[g],
                       preferred_element_type=F32) for g in groups]
        keeps = [sel_keep(g, kt, wide) for g in groups]
        vts = [jnp.concatenate([vs_ref[0, g, kt + i] for i in range(wide)], axis=1) for g in groups]
        return tuple(_attn_round(list(state), raw, keeps, none4, vts, far))

    n_far = jnp.maximum(qi - 1, 0)
    sel = lax.fori_loop(0, n_far // wide, sel_body_wide, tuple(sel))
    sel = lax.fori_loop(wide * (n_far // wide), n_far, sel_body, sel)

    o_t = []
    dh = NSA_HEAD_DIM
    for g in groups:
        o_s = sel[g][1][0:dh, :] * (1.0 / sel[g][1][dh:dh + 1, :])
        o_w = win[g][1][0:dh, :] * (1.0 / win[g][1][dh:dh + 1, :])
        for r, h in enumerate(heads[g]):
            sl = slice(r * QT, (r + 1) * QT)
            o_t.append(gates_t[3 * h + 1:3 * h + 2, :] * o_s[:, sl] + gates_t[3 * h + 2:3 * h + 3, :] * o_w[:, sl])

    pairs = [jnp.concatenate(o_t[2 * k:2 * k + 2], axis=0).T for k in range(len(o_t) // 2)]
    y_nsa = jnp.concatenate(pairs, axis=1) + ocmp_ref[...]
    u = mssm_ref[...] + sgn_ref[...] * _bdot(y_nsa, wn_ref[...])
    h_ref[...] = x_ref[...] + _bdot(u, wo_ref[...])


def _nsa_main(c31, q, brg, sel, ocmp, ks, vs_t, kw, vw_t, b0, b1, mssm, sgn, x, wn, wo, batch):
    n, qw = q.shape
    d = x.shape[1]
    s = n // batch
    nq = s // QT
    rows = lambda b, i: (b * nq + i, 0)
    kspec = pl.BlockSpec((1,) + ks.shape[1:], lambda b, i: (b, 0, 0, 0))
    vspec = pl.BlockSpec((1,) + vs_t.shape[1:], lambda b, i: (b, 0, 0, 0, 0))
    c2 = lambda b, i: (0, 0)
    c3 = lambda b, i: (0, 0, 0)
    return pl.pallas_call(
        _nsa_main_kernel, grid=(batch, nq),
        in_specs=[pl.BlockSpec(memory_space=pltpu.SMEM),
                  pl.BlockSpec((QT, qw), rows), pl.BlockSpec((QT, LANES), rows),
                  pl.BlockSpec((1,) + sel.shape[1:3] + (QT,), lambda b, i: (b, 0, 0, i)),
                  pl.BlockSpec((QT, qw), rows), kspec, vspec, kspec, vspec,
                  pl.BlockSpec(b0.shape, c3), pl.BlockSpec(b1.shape, c3),
                  pl.BlockSpec((QT, d), rows), pl.BlockSpec((QT, d), rows), pl.BlockSpec((QT, d), rows),
                  pl.BlockSpec(wn.shape, c2), pl.BlockSpec(wo.shape, c2)],
        out_specs=pl.BlockSpec((QT, d), rows), out_shape=jax.ShapeDtypeStruct((n, d), F32),
        compiler_params=_cparams(("parallel", "parallel")), name="nsa_main",
    )(c31, q, brg, sel, ocmp, ks, vs_t, kw, vw_t, b0, b1, mssm, sgn, x, wn, wo)


META_EXPERT, META_RANK, META_GATE = 0, TOP_K, 2 * TOP_K


def _pack_bf16_pair(x):
    c = x.shape[1] // 2
    hi = pltpu.bitcast(x[:, :c].astype(BF16).astype(F32), jnp.int32)
    lo = pltpu.bitcast(x[:, c:].astype(BF16).astype(F32), jnp.int32)
    return hi | lax.shift_right_logical(lo, 16)


def _unpack_bf16_pair(w):
    hi = pltpu.bitcast(w & jnp.int32(-65536), F32).astype(BF16)
    lo = pltpu.bitcast(lax.shift_left(w, 16), F32).astype(BF16)
    return hi, lo


def _router_kernel(h_ref, nw_ref, wr_ref, br_ref, tri_ref, hn_ref, meta_ref, cnt_ref, base_s, *, n_real):
    i = pl.program_id(0)
    tm = h_ref.shape[0]

    @pl.when(i == 0)
    def _():
        base_s[...] = jnp.zeros_like(base_s)

    hn = _rms_rows(h_ref[...], nw_ref[...])
    hn_ref[...] = _pack_bf16_pair(hn)
    a_hi = hn.astype(BF16)
    a_lo = (hn - a_hi.astype(F32)).astype(BF16)
    w = wr_ref[...]
    w_hi = w.astype(BF16)
    w_lo = (w - w_hi.astype(F32)).astype(BF16)
    d = lambda a, b: jnp.dot(a, b, preferred_element_type=F32)
    v = (d(a_hi, w_hi) + (d(a_hi, w_lo) + d(a_lo, w_hi))) + br_ref[...]
    lane = lax.broadcasted_iota(jnp.int32, v.shape, 1)
    tops, idxs, hots = [], [], []
    for _ in range(TOP_K):
        m = jnp.max(v, axis=-1, keepdims=True)
        idx = jnp.min(jnp.where(v == m, lane, LANES), axis=-1, keepdims=True)
        hot = lane == idx
        tops.append(m)
        idxs.append(idx)
        hots.append(hot)
        v = jnp.where(hot, NEG, v)
    es = [jnp.exp(t - tops[0]) for t in tops]
    inv = 1.0 / functools.reduce(lambda a, b: a + b, es)
    row = i * tm + lax.broadcasted_iota(jnp.int32, v.shape, 0)
    onehot = jnp.zeros(v.shape, F32)
    for hot in hots:
        onehot = onehot + jnp.where(hot & (row < n_real), 1.0, 0.0)
    before = jnp.dot(tri_ref[...], onehot.astype(BF16), preferred_element_type=F32) + base_s[0:1, :]
    meta = jnp.zeros(v.shape, F32)
    for k in range(TOP_K):
        rank = jnp.sum(jnp.where(hots[k], before, 0.0), axis=-1, keepdims=True)
        meta = meta + jnp.where(lane == META_EXPERT + k, idxs[k].astype(F32), 0.0)
        meta = meta + jnp.where(lane == META_RANK + k, rank, 0.0)
        meta = meta + jnp.where(lane == META_GATE + k, es[k] * inv, 0.0)
    meta_ref[...] = meta
    base_s[0:1, :] = base_s[0:1, :] + jnp.sum(onehot, axis=0, keepdims=True)
    cnt_ref[...] = jnp.broadcast_to(base_s[0:1, :], cnt_ref.shape)


def _router(h, nw, wr, br, tm, n_real):
    n, d = h.shape
    c2 = lambda i: (0, 0)
    tri = jnp.asarray(np.tril(np.ones((tm, tm), np.float32), -1), BF16)
    return pl.pallas_call(
        functools.partial(_router_kernel, n_real=n_real), grid=(n // tm,),
        in_specs=[pl.BlockSpec((tm, d), lambda i: (i, 0)), pl.BlockSpec(nw.shape, c2),
                  pl.BlockSpec(wr.shape, c2), pl.BlockSpec(br.shape, c2), pl.BlockSpec(tri.shape, c2)],
        out_specs=[pl.BlockSpec((tm, d // 2), lambda i: (i, 0)), pl.BlockSpec((tm, LANES), lambda i: (i, 0)),
                   pl.BlockSpec((SUBLANES, LANES), c2)],
        out_shape=[jax.ShapeDtypeStruct((n, d // 2), jnp.int32), jax.ShapeDtypeStruct((n, LANES), F32),
                   jax.ShapeDtypeStruct((SUBLANES, LANES), F32)],
        scratch_shapes=[pltpu.VMEM((SUBLANES, LANES), F32)],
        compiler_params=_cparams(("arbitrary",)), name="moe_router",
    )(h, nw, wr, br, tri)


def _moe_group_kernel(sp_ref, x_ref, w1_ref, b1_ref, w2_ref, b2_ref, y_ref, w1_s, w2_s):
    t = pl.program_id(0)
    n_active = sp_ref[pl.num_programs(0)]
    new_expert = (t == 0) | (sp_ref[t] != sp_ref[jnp.maximum(t - 1, 0)])

    @pl.when(new_expert & (t < n_active))
    def _():
        w1_s[...] = w1_ref[0].astype(BF16)
        w2_s[...] = w2_ref[0].astype(BF16)

    @pl.when(t < n_active)
    def _():
        d_ff = w2_s.shape[0]
        x_a, x_b = _unpack_bf16_pair(x_ref[...])
        half = x_a.shape[1]
        y1 = (jnp.dot(x_a, w1_s[:half, :], preferred_element_type=F32)
              + jnp.dot(x_b, w1_s[half:, :], preferred_element_type=F32)) + b1_ref[0]
        gt = jnp.minimum(y1[:, :d_ff], SWIGLU_LIMIT)
        up = jnp.clip(y1[:, d_ff:], -SWIGLU_LIMIT, SWIGLU_LIMIT)
        act = (up + 1.0) * gt * _sigmoid(SWIGLU_ALPHA * gt)
        y_ref[...] = _pack_bf16_pair(_bdot(act, w2_s[...]) + b2_ref[0])

    @pl.when(t >= n_active)
    def _():
        y_ref[...] = jnp.zeros_like(y_ref)


def _moe_group(sp, x_sorted, w1, b1, w2, b2, tm):
    p, dx = x_sorted.shape
    d = w2.shape[2]
    rows = lambda t, sp: (t, 0)
    per_e = lambda t, sp: (sp[t], 0, 0)
    grid_spec = pltpu.PrefetchScalarGridSpec(
        num_scalar_prefetch=1, grid=(p // tm,),
        in_specs=[pl.BlockSpec((tm, dx), rows),
                  pl.BlockSpec((1,) + w1.shape[1:], per_e), pl.BlockSpec((1,) + b1.shape[1:], per_e),
                  pl.BlockSpec((1,) + w2.shape[1:], per_e), pl.BlockSpec((1,) + b2.shape[1:], per_e)],
        out_specs=pl.BlockSpec((tm, d // 2), rows),
        scratch_shapes=[pltpu.VMEM(w1.shape[1:], BF16), pltpu.VMEM(w2.shape[1:], BF16)])
    return pl.pallas_call(
        _moe_group_kernel, grid_spec=grid_spec, out_shape=jax.ShapeDtypeStruct((p, d // 2), jnp.int32),
        compiler_params=_cparams(("arbitrary",)), name="moe_experts",
    )(sp, x_sorted, w1, b1, w2, b2)


def _moe_combine_kernel(h_ref, yk_ref, meta_ref, o_ref):
    d = h_ref.shape[1]
    meta = meta_ref[...]
    kk = lax.broadcasted_iota(jnp.int32, (LANES, LANES), 0)
    acc_a = h_ref[:, :d // 2]
    acc_b = h_ref[:, d // 2:]
    for k in range(TOP_K):
        pick = jnp.where(kk == META_GATE + k, 1.0, 0.0).astype(BF16)
        g = _dot3(meta, pick)
        g = jnp.concatenate([g] * (d // 2 // LANES), axis=1)
        y_a, y_b = _unpack_bf16_pair(yk_ref[k])
        acc_a = acc_a + g * y_a.astype(F32)
        acc_b = acc_b + g * y_b.astype(F32)
    o_ref[:, :d // 2] = acc_a
    o_ref[:, d // 2:] = acc_b


def _moe_combine(h, yk, meta, tm):
    n, d = h.shape
    rows = lambda i: (i, 0)
    return pl.pallas_call(
        _moe_combine_kernel, grid=(n // tm,),
        in_specs=[pl.BlockSpec((tm, d), rows), pl.BlockSpec((TOP_K, tm, d // 2), lambda i: (0, i, 0)),
                  pl.BlockSpec((tm, LANES), rows)],
        out_specs=pl.BlockSpec((tm, d), rows), out_shape=jax.ShapeDtypeStruct((n, d), F32),
        compiler_params=_cparams(("parallel",)), name="moe_combine",
    )(h, yk, meta)


def _mamba_prep_kernel(xbc_ref, sconv_ref, dt_ref, cw_ref, cb_ref, hp_ref, e_ref,
                       conv_ref, xs_ref, bm_ref, ct_ref, dtx_ref, dec_ref):
    cdim = xbc_ref.shape[1]
    d_inner = xs_ref.shape[1]
    xb = xbc_ref[...]
    u = cb_ref[...] + cw_ref[D_CONV - 1:D_CONV, :] * xb
    for k in range(D_CONV - 1):
        u = u + cw_ref[k:k + 1, :] * sconv_ref[:, k * cdim:(k + 1) * cdim]
    conv_ref[:, :(D_CONV - 2) * cdim] = sconv_ref[:, cdim:]
    conv_ref[:, (D_CONV - 2) * cdim:] = xb
    u = u * _sigmoid(u)
    xs = u[:, :d_inner]
    xs_ref[...] = xs
    bm_ref[...] = u[:, d_inner:d_inner + N_SSM_GROUPS * D_STATE]
    cm = u[:, d_inner + N_SSM_GROUPS * D_STATE:]
    for g in range(N_SSM_GROUPS):
        ct_ref[g] = cm[:, g * D_STATE:(g + 1) * D_STATE].T
    dtv = _softplus(dt_ref[...] + hp_ref[0:1, :])
    dec_ref[...] = jnp.exp(dtv * (-jnp.exp(hp_ref[1:2, :])))
    dtx_ref[...] = (xs * _dot3(dtv, e_ref[...])).T


def _mamba_prep(xbc, sconv, dt, conv_w, conv_b, headp, expand):
    n, cdim = xbc.shape
    d_inner = expand.shape[1]
    gn = N_SSM_GROUPS * D_STATE
    out_shape = [jax.ShapeDtypeStruct(sconv.shape, F32), jax.ShapeDtypeStruct((n, d_inner), F32),
                 jax.ShapeDtypeStruct((n, gn), F32), jax.ShapeDtypeStruct((N_SSM_GROUPS, D_STATE, n), F32),
                 jax.ShapeDtypeStruct((d_inner, n), F32), jax.ShapeDtypeStruct((n, LANES), F32)]
    return pl.pallas_call(_mamba_prep_kernel, out_shape=out_shape,
                          compiler_params=pltpu.CompilerParams(vmem_limit_bytes=VMEM_LIMIT), name="mamba_prep",
                          )(xbc, sconv, dt, conv_w, conv_b, headp, expand)


def _bf16x3(a, b):
    a_hi = a.astype(BF16)
    a_lo = (a - a_hi.astype(F32)).astype(BF16)
    b_hi = b.astype(BF16)
    b_lo = (b - b_hi.astype(F32)).astype(BF16)
    d = lambda x, y: jnp.dot(x, y, preferred_element_type=F32)
    return d(a_hi, b_hi) + (d(a_hi, b_lo) + d(a_lo, b_hi))


def _mamba_state_kernel(dec_ref, h0_ref, dtx_ref, bm_ref, ct_ref, hn_ref, yt_ref):
    s = pl.program_id(0)
    n = bm_ref.shape[0]
    d_inner = dtx_ref.shape[0]
    gw = d_inner // N_SSM_GROUPS
    hpg = gw // SSM_HEAD_DIM

    @pl.when(s == 0)
    def _():
        yt_ref[...] = jnp.zeros_like(yt_ref)

    row_is_s = lax.broadcasted_iota(jnp.int32, (n, D_STATE), 0) == s
    col_is_s = lax.broadcasted_iota(jnp.int32, (D_STATE, n), 1) == s
    groups = range(N_SSM_GROUPS)
    sts = []
    for g in groups:
        b_s = jnp.where(row_is_s, bm_ref[:, g * D_STATE:(g + 1) * D_STATE], 0.0)
        sts.append(_bf16x3(dtx_ref[g * gw:(g + 1) * gw, :], b_s))
    hns = []
    for g in groups:
        parts = []
        for r in range(hpg):
            h = g * hpg + r
            rows = slice(h * SSM_HEAD_DIM, (h + 1) * SSM_HEAD_DIM)
            parts.append(h0_ref[0, rows, :] * dec_ref[s, h] + sts[g][r * SSM_HEAD_DIM:(r + 1) * SSM_HEAD_DIM, :])
        hns.append(jnp.concatenate(parts, axis=0))
        hn_ref[0, g * gw:(g + 1) * gw, :] = hns[g]
    for g in groups:
        c_s = jnp.where(col_is_s, ct_ref[g], 0.0)
        yt_ref[g * gw:(g + 1) * gw, :] += _bdot(hns[g], c_s)


def _mamba_state(dec, h0, dtx_t, bm, ct):
    n, rows, ns = h0.shape
    c2 = lambda s: (0, 0)
    return pl.pallas_call(
        _mamba_state_kernel, grid=(n,),
        in_specs=[pl.BlockSpec(memory_space=pltpu.SMEM),
                  pl.BlockSpec((1, rows, ns), lambda s: (s, 0, 0)), pl.BlockSpec(dtx_t.shape, c2),
                  pl.BlockSpec(bm.shape, c2), pl.BlockSpec(ct.shape, lambda s: (0, 0, 0))],
        out_specs=[pl.BlockSpec((1, rows, ns), lambda s: (s, 0, 0)), pl.BlockSpec(dtx_t.shape, c2)],
        out_shape=[jax.ShapeDtypeStruct(h0.shape, F32), jax.ShapeDtypeStruct(dtx_t.shape, F32)],
        compiler_params=_cparams(("arbitrary",)), name="mamba_state",
    )(dec, h0, dtx_t, bm, ct)


def _page_cmp_kernel(pt_ref, cache_ref, new_ref, w_ref, pec_ref, deint_ref, ones_ref, nw_ref, out_ref, buf, rows_s, sem):
    t = pl.program_id(0)
    nsteps = pl.num_programs(0)
    n_pages = pt_ref.shape[1]
    page = buf.shape[4]
    nstr = out_ref.shape[2]
    half = out_ref.shape[3]
    nch = half // LANES
    spp = page // CMP_STRIDE
    past_str = n_pages * spp

    def page_copy(step, slot, p):
        seq, kv = step // 2, step % 2
        return pltpu.make_async_copy(cache_ref.at[pt_ref[seq, p], pl.ds(nch * kv, nch)], buf.at[slot, p], sem.at[slot])

    @pl.when(t == 0)
    def _():
        rows_s[:, past_str:, :] = jnp.zeros((CMP_STRIDE, nstr - past_str, half), F32)
        for p in range(n_pages):
            page_copy(t, 0, p).start()

    slot = t % 2
    for p in range(n_pages):
        page_copy(t, slot, p).wait()

    kv = t % 2
    deint = deint_ref[...]

    pages_per_trip = math.gcd(n_pages, 8)

    def to_rows(trip, carry):
        @pl.when(t + 1 < nsteps)
        def _():
            for pp in range(pages_per_trip):
                page_copy(t + 1, 1 - slot, trip * pages_per_trip + pp).start()

        for pp in range(pages_per_trip):
            p = trip * pages_per_trip + pp
            xr = _bdot_nt(deint, buf[slot, p].reshape(nch * LANES, page))
            for i in range(CMP_STRIDE):
                rows_s[i, pl.ds(pl.multiple_of(p * spp, spp), spp), :] = xr[i * spp:(i + 1) * spp, :]
        return carry

    lax.fori_loop(0, n_pages // pages_per_trip, to_rows, 0)
    new = new_ref[0]
    rows_s[0, past_str:past_str + 1, :] = jnp.where(kv == 0, new[:, :half], new[:, half:])

    xcat = jnp.concatenate([rows_s[i].astype(BF16) for i in range(CMP_STRIDE)], axis=1)
    w_lo = w_ref[0, :CMP_STRIDE].reshape(CMP_STRIDE * half, half)
    w_hi = w_ref[0, CMP_STRIDE:].reshape(CMP_STRIDE * half, half)
    lo = jnp.dot(xcat, w_lo, preferred_element_type=F32) + pec_ref[0, 0:1, :]
    hi = jnp.dot(xcat, w_hi, preferred_element_type=F32) + pec_ref[0, 1:2, :]
    tok = lo + pltpu.roll(hi, nstr - 1, axis=0)
    out_ref[0, 0] = jnp.where(kv == 0, _head_rms(tok, ones_ref[...], nw_ref[...], NSA_HEAD_DIM), tok)


def _cmp_const_kernel(pe_ref, w_ref, out_ref):
    for kv in range(2):
        lo = hi = None
        for i in range(CMP_STRIDE):
            j = CMP_STRIDE + i
            a = _dot3(_rows8(pe_ref[kv, i:i + 1, :]), w_ref[kv, i])
            b = _dot3(_rows8(pe_ref[kv, j:j + 1, :]), w_ref[kv, j])
            lo = a if lo is None else lo + a
            hi = b if hi is None else hi + b
        out_ref[kv] = jnp.concatenate([lo[0:1, :], hi[0:1, :], jnp.zeros((SUBLANES - 2, lo.shape[1]), F32)], axis=0)


def _cmp_const(pe_kv, w_kv):
    return pl.pallas_call(_cmp_const_kernel, out_shape=jax.ShapeDtypeStruct((2, SUBLANES, pe_kv.shape[2]), F32),
                          compiler_params=pltpu.CompilerParams(vmem_limit_bytes=VMEM_LIMIT), name="cmp_const",
                          )(pe_kv, w_kv)


def _page_cmp(page_table, cache_t, new_rows, w_kv, pe_const, ones64, nw, nstr):
    nseq, n_pages = page_table.shape
    page = cache_t.shape[3]
    half = w_kv.shape[2]
    nch = half // LANES
    spp = page // CMP_STRIDE
    deint = np.zeros((page, page), np.float32)
    for i in range(CMP_STRIDE):
        for k in range(spp):
            deint[i * spp + k, CMP_STRIDE * k + i] = 1.0
    deint = jnp.asarray(deint, BF16)
    c2 = lambda t, pt: (0, 0)
    grid_spec = pltpu.PrefetchScalarGridSpec(
        num_scalar_prefetch=1, grid=(2 * nseq,),
        in_specs=[pl.BlockSpec(memory_space=pl.ANY),
                  pl.BlockSpec((1, 1, 2 * half), lambda t, pt: (t // 2, 0, 0)),
                  pl.BlockSpec((1,) + w_kv.shape[1:], lambda t, pt: (t % 2, 0, 0, 0)),
                  pl.BlockSpec((1,) + pe_const.shape[1:], lambda t, pt: (t % 2, 0, 0)),
                  pl.BlockSpec(deint.shape, c2), pl.BlockSpec(ones64.shape, c2), pl.BlockSpec(nw.shape, c2)],
        out_specs=pl.BlockSpec((1, 1, nstr, half), lambda t, pt: (t // 2, t % 2, 0, 0)),
        scratch_shapes=[pltpu.VMEM((2, n_pages, nch, LANES, page), F32),
                        pltpu.VMEM((CMP_STRIDE, nstr, half), F32),
                        pltpu.SemaphoreType.DMA((2,))])
    return pl.pallas_call(
        _page_cmp_kernel, grid_spec=grid_spec, out_shape=jax.ShapeDtypeStruct((nseq, 2, nstr, half), F32),
        compiler_params=_cparams(("arbitrary",)), name="page_cmp",
    )(page_table, cache_t, new_rows, w_kv, pe_const, deint, ones64, nw)


def _rows8(x):
    return jnp.broadcast_to(x, (SUBLANES, x.shape[1]))


def _group_q(q_row, g):
    parts = [q_row[:, (g * Q_PER_KV + r) * NSA_HEAD_DIM:(g * Q_PER_KV + r + 1) * NSA_HEAD_DIM]
             for r in range(Q_PER_KV)]
    parts.append(jnp.zeros((SUBLANES - Q_PER_KV, NSA_HEAD_DIM), F32))
    return jnp.concatenate(parts, axis=0)


def _heads_to_row(o):
    return jnp.concatenate([o[r:r + 1, :] for r in range(Q_PER_KV)], axis=1)


def _softmax_rows(s, keep):
    s = jnp.where(keep, s, NEG)
    m = jnp.max(s, axis=-1, keepdims=True)
    p = jnp.where(keep, jnp.exp2(s - m), 0.0)
    l = jnp.sum(p, axis=-1, keepdims=True)
    return p * jnp.where(l > 0.0, 1.0 / l, 0.0)


def _sample_cw_kernel(q_ref, brg_ref, kvc_ref, win_ref, wnew_ref, tbc_ref, tbw_ref, gexp_ref,
                      o_ref, sel_ref, wout_ref, wall_s, *, tq, past_w):
    nstr = kvc_ref.shape[2]
    half = kvc_ref.shape[3]
    wlen = win_ref.shape[1]
    nsel_pad = sel_ref.shape[2]
    wrows = wall_s.shape[0]
    q_row = q_ref[0]
    wall_s[0:wlen, :] = win_ref[0]
    wall_s[wlen:wlen + 1, :] = wnew_ref[0]
    wall_s[wlen + 1:, :] = jnp.zeros((wrows - wlen - 1, wall_s.shape[1]), F32)
    wout_ref[0] = wall_s[1:wlen + 1, :]

    nn = lax.broadcasted_iota(jnp.int32, (SUBLANES, nstr), 1)
    keep_c = (CMP_STRIDE * nn + (CMP_BLOCK - 1)) <= tq
    wi = lax.broadcasted_iota(jnp.int32, (SUBLANES, wrows), 1)
    dw = tq - (past_w + wi)
    keep_w = (dw >= 0) & (dw < WINDOW) & (past_w + wi >= 0) & (wi <= wlen)
    jb = lax.broadcasted_iota(jnp.int32, (nsel_pad, nstr), 0) * SEL_BLOCK
    cs = lax.broadcasted_iota(jnp.int32, (nsel_pad, nstr), 1) * CMP_STRIDE
    cover_t = jnp.where((cs < jb + SEL_BLOCK) & (cs + CMP_BLOCK > jb), 1.0, 0.0).astype(BF16)
    ji = lax.broadcasted_iota(jnp.int32, (nsel_pad, LANES), 0)
    valid = ji * SEL_BLOCK <= tq
    cur = tq // SEL_BLOCK
    forced = valid & ((ji == 0) | (ji == cur) | (ji == cur - 1))
    ii = lax.broadcasted_iota(jnp.int32, (nsel_pad, nsel_pad), 0)
    jj = lax.broadcasted_iota(jnp.int32, (nsel_pad, nsel_pad), 1)

    sel_ref[0] = jnp.zeros(sel_ref.shape[1:], F32)
    groups = range(N_KV_HEADS)
    k_lanes = [slice(g * NSA_HEAD_DIM, (g + 1) * NSA_HEAD_DIM) for g in groups]
    v_lanes = [slice(half + g * NSA_HEAD_DIM, half + (g + 1) * NSA_HEAD_DIM) for g in groups]
    qgs = [_group_q(q_row, g) for g in groups]
    sc = [_bdot_nt(qgs[g], kvc_ref[0, 0, :, k_lanes[g]]) + tbc_ref[g] for g in groups]
    sw = [_bdot_nt(qgs[g], wall_s[:, k_lanes[g]]) + tbw_ref[g] for g in groups]
    pc = [_softmax_rows(s, keep_c) for s in sc]
    pw = [_softmax_rows(s, keep_w) for s in sw]
    oc = [_heads_to_row(_bdot(pc[g], kvc_ref[0, 1, :, k_lanes[g]])) for g in groups]
    ow = [_heads_to_row(_bdot(pw[g], wall_s[:, v_lanes[g]])) for g in groups]
    imps = [_dot3_nt_l(cover_t, jnp.broadcast_to(jnp.sum(pc[g][0:Q_PER_KV, :], axis=0, keepdims=True), (LANES, nstr)))
            for g in groups]
    for g in groups:
        v_col = jnp.where(forced, FORCE_SCORE, jnp.where(valid, imps[g], NEG))
        v_row = jnp.concatenate([v_col[k * LANES:(k + 1) * LANES, :].T for k in range(nsel_pad // LANES)], axis=1)
        a = jnp.broadcast_to(v_row[0:1, :], (nsel_pad, nsel_pad))
        b = jnp.concatenate([v_col] * (nsel_pad // LANES), axis=1)
        beats = jnp.where(ii < jj, jnp.where(b >= a, 1.0, 0.0), jnp.where(b > a, 1.0, 0.0))
        cnt = jnp.sum(beats, axis=0, keepdims=True)
        sel_ref[0, g:g + 1, :] = jnp.where(cnt < N_SEL_BLOCKS, 1.0, 0.0)
    gates = _rows8(brg_ref[0])
    o = (jnp.concatenate(oc, axis=1) * _dot3(gates, gexp_ref[0])[0:1, :]
         + jnp.concatenate(ow, axis=1) * _dot3(gates, gexp_ref[2])[0:1, :])
    o_ref[0] = o


def _sample_cw(q, brg, kvc, win, wnew, tbc, tbw, gexp, tq, past_w, nsel_pad):
    nseq = q.shape[0]
    qw = q.shape[2]
    wlen, ww = win.shape[1], win.shape[2]
    wrows = -(-(wlen + 1) // SUBLANES) * SUBLANES
    per3 = lambda s: (s, 0, 0)
    c3 = lambda s: (0, 0, 0)
    return pl.pallas_call(
        functools.partial(_sample_cw_kernel, tq=tq, past_w=past_w), grid=(nseq,),
        in_specs=[pl.BlockSpec((1, 1, qw), per3), pl.BlockSpec((1, 1, LANES), per3),
                  pl.BlockSpec((1,) + kvc.shape[1:], lambda s: (s, 0, 0, 0)),
                  pl.BlockSpec((1, wlen, ww), per3), pl.BlockSpec((1, 1, ww), per3),
                  pl.BlockSpec(tbc.shape, c3), pl.BlockSpec(tbw.shape, c3), pl.BlockSpec(gexp.shape, c3)],
        out_specs=[pl.BlockSpec((1, 1, qw), per3), pl.BlockSpec((1, SUBLANES, nsel_pad), per3),
                   pl.BlockSpec((1, wlen, ww), per3)],
        out_shape=[jax.ShapeDtypeStruct((nseq, 1, qw), F32), jax.ShapeDtypeStruct((nseq, SUBLANES, nsel_pad), F32),
                   jax.ShapeDtypeStruct((nseq, wlen, ww), F32)],
        scratch_shapes=[pltpu.VMEM((wrows, ww), F32)],
        compiler_params=_cparams(("parallel",)), name="sample_cmp_win",
    )(q, brg, kvc, win, wnew, tbc, tbw, gexp)


def _sample_sel_kernel(pg_ref, c31_ref, f0_ref, cache_ref, q_ref, brg_ref, snew_ref, code_ref, ocw_ref, tbl_ref, gexp_ref,
                       o_ref, buf, sem):
    s = pl.program_id(0)
    nseq = pl.num_programs(0)
    nblk = pg_ref.shape[1] // N_KV_HEADS
    page = cache_ref.shape[4]
    half = snew_ref.shape[2] // 2

    def copies(seq, slot):
        out = []
        for g in range(N_KV_HEADS):
            for k in range(nblk):
                out.append(pltpu.make_async_copy(cache_ref.at[pg_ref[seq, g * nblk + k], :, g],
                                                 buf.at[slot, :, g, :, pl.ds(k * page, page)], sem.at[slot]))
        return out

    @pl.when(s == 0)
    def _():
        for cp in copies(s, 0):
            cp.start()

    slot = s % 2

    @pl.when(s + 1 < nseq)
    def _():
        for cp in copies(s + 1, 1 - slot):
            cp.start()

    for cp in copies(s, slot):
        cp.wait()

    q_row = q_ref[0]
    new = snew_ref[0]
    groups = range(N_KV_HEADS)
    qgs = [_group_q(q_row, g) for g in groups]
    raw = [jnp.dot(qgs[g].astype(BF16), buf[slot, 0, g].astype(BF16), preferred_element_type=F32) for g in groups]
    stats = []
    for g in groups:
        heads = [g * Q_PER_KV + r for r in range(Q_PER_KV)]
        per_head = lambda ref: jnp.concatenate([jnp.full((1, 1), ref[h], F32) for h in heads]
                                               + [jnp.zeros((SUBLANES - Q_PER_KV, 1), F32)], axis=0)
        code = _rows8(code_ref[0, g:g + 1, :])
        near = jnp.concatenate([tbl_ref[g]] * nblk, axis=1)
        sc = jnp.where(code > 0.5, raw[g] + jnp.where(code > 1.5, near, per_head(c31_ref)), NEG)
        k_new = new[:, g * NSA_HEAD_DIM:(g + 1) * NSA_HEAD_DIM]
        s_new = jnp.sum(qgs[g] * _rows8(k_new), axis=-1, keepdims=True) + per_head(f0_ref)
        new_on = _rows8(code_ref[0, N_KV_HEADS + g:N_KV_HEADS + g + 1, 0:1]) > 0.5
        s_new = jnp.where(new_on, s_new, NEG)
        m = jnp.maximum(jnp.max(sc, axis=-1, keepdims=True), s_new)
        p = jnp.where(code > 0.5, jnp.exp2(sc - m), 0.0)
        p_new = jnp.where(new_on, jnp.exp2(s_new - m), 0.0)
        l = jnp.sum(p, axis=-1, keepdims=True) + p_new
        stats.append((p, p_new, jnp.where(l > 0.0, 1.0 / l, 0.0)))
    outs = []
    for g in groups:
        p, p_new, inv = stats[g]
        v_new = new[:, half + g * NSA_HEAD_DIM:half + (g + 1) * NSA_HEAD_DIM]
        o = _bdot_nt(p, buf[slot, 1, g]) + p_new * _rows8(v_new)
        outs.append(_heads_to_row(o * inv))
    gates = _dot3(_rows8(brg_ref[0]), gexp_ref[1])[0:1, :]
    o_ref[0] = ocw_ref[0] + jnp.concatenate(outs, axis=1) * gates


def _sample_sel(pages, c31, f0, cache_t, q, brg, snew, code, ocw, tbl, gexp):
    nseq, qw = q.shape[0], q.shape[2]
    nblk = pages.shape[1] // N_KV_HEADS
    page = cache_t.shape[4]
    per3 = lambda s, *_: (s, 0, 0)
    c3 = lambda s, *_: (0, 0, 0)
    grid_spec = pltpu.PrefetchScalarGridSpec(
        num_scalar_prefetch=1, grid=(nseq,),
        in_specs=[pl.BlockSpec(memory_space=pltpu.SMEM), pl.BlockSpec(memory_space=pltpu.SMEM),
                  pl.BlockSpec(memory_space=pl.ANY),
                  pl.BlockSpec((1, 1, qw), per3), pl.BlockSpec((1, 1, LANES), per3),
                  pl.BlockSpec((1, 1, snew.shape[2]), per3), pl.BlockSpec((1,) + code.shape[1:], per3),
                  pl.BlockSpec((1, 1, qw), per3), pl.BlockSpec(tbl.shape, c3), pl.BlockSpec(gexp.shape, c3)],
        out_specs=pl.BlockSpec((1, 1, qw), per3),
        scratch_shapes=[pltpu.VMEM((2, 2, N_KV_HEADS, NSA_HEAD_DIM, nblk * page), F32),
                        pltpu.SemaphoreType.DMA((2,))])
    return pl.pallas_call(
        _sample_sel_kernel, grid_spec=grid_spec, out_shape=jax.ShapeDtypeStruct((nseq, 1, qw), F32),
        compiler_params=_cparams(("arbitrary",)), name="sample_sel",
    )(pages, c31, f0, cache_t, q, brg, snew, code, ocw, tbl, gexp)


def _sample_merge_kernel(yt_ref, xs_ref, z_ref, sgs_ref, dsk_ref, nw_ref, onesg_ref, ws_ref, ynsa_ref, sgn_ref, x_ref,
                         wn_ref, wo_ref, h_ref):
    gw = onesg_ref.shape[0]
    y = yt_ref[...].T + dsk_ref[...] * xs_ref[...]
    zz = z_ref[...]
    y = y * (zz * _sigmoid(zz))
    ms = _seg_sum(y * y, onesg_ref[...]) * (1.0 / gw)
    y = (y * lax.rsqrt(ms + RMS_EPS)) * nw_ref[...]
    u = sgs_ref[...] * _bdot(y, ws_ref[...]) + sgn_ref[...] * _bdot(ynsa_ref[...], wn_ref[...])
    h_ref[...] = x_ref[...] + _bdot(u, wo_ref[...])


def _sample_merge(y_t, xs, z, sgs, dsk, nw, onesg, ws, ynsa, sgn, x, wn, wo):
    return pl.pallas_call(_sample_merge_kernel, out_shape=jax.ShapeDtypeStruct(x.shape, F32),
                          compiler_params=pltpu.CompilerParams(vmem_limit_bytes=VMEM_LIMIT), name="sample_merge",
                          )(y_t, xs, z, sgs, dsk, nw, onesg, ws, ynsa, sgn, x, wn, wo)


def _bucket_lut():
    n = np.arange(MAX_DISTANCE + 1)
    max_exact = N_BUCKETS // 2
    nf = np.maximum(n, 1).astype(np.float32)
    large = max_exact + (np.log(nf / max_exact) / math.log(MAX_DISTANCE / max_exact)
                         * (N_BUCKETS - max_exact)).astype(np.int32)
    return np.where(n < max_exact, n, np.minimum(large, N_BUCKETS - 1))


def _bias_of_dist(rel_bias, dist):
    lut = _bucket_lut()
    idx = lut[np.clip(dist, 0, MAX_DISTANCE)]
    onehot = np.eye(N_BUCKETS, dtype=np.float32)[idx] * (np.asarray(dist) >= 0)[..., None]
    b = jnp.einsum("...b,bh->h...", jnp.asarray(onehot), rel_bias.astype(F32), precision=lax.Precision.HIGHEST)
    return b * LOG2E


def _block_diag(w, reps):
    n, d, e = w.shape
    eye = jnp.eye(reps, dtype=w.dtype)
    return jnp.einsum("ab,nde->nadbe", eye, w).reshape(n, reps * d, reps * e)


def _ones_blocks(size, seg):
    return jnp.asarray(np.kron(np.eye(size // seg), np.ones((seg, seg))), BF16)


def _pad_cols(w, width):
    return jnp.pad(w, ((0, 0), (0, width - w.shape[1])))


def _prep(p):
    d_model = p["w_in"].shape[1]
    d_inner = p["w_ssm_out"].shape[1]
    n_heads = d_inner // SSM_HEAD_DIM
    conv_dim = p["conv_w"].shape[2]
    qw = N_Q_HEADS * NSA_HEAD_DIM
    kvw = 2 * N_KV_HEADS * NSA_HEAD_DIM
    splits = (d_inner, conv_dim, n_heads, qw, kvw, kvw, kvw, 3 * N_Q_HEADS, d_model, d_model)
    offs = np.concatenate([[0], np.cumsum(splits)])
    w_in = p["w_in"][0]
    seg = lambda k: w_in[:, offs[k]:offs[k + 1]]
    bf = lambda a: a.astype(BF16)
    o = {}
    o["w_ssm_in"] = [bf(seg(0)), bf(seg(1)), bf(_pad_cols(seg(2), LANES))]
    o["w_nsa_in"] = [bf(seg(3)), bf(seg(4)), bf(seg(5)), bf(seg(6)), bf(_pad_cols(seg(7), LANES)), bf(seg(8)),
                     bf(seg(9))]
    o["norm_mix"] = p["norm_mix_w"][0][None, :]
    kv_half = kvw // 2
    head_w = jnp.zeros((SUBLANES, qw), F32)
    head_w = head_w.at[0].set(jnp.tile(p["q_norm_w"][0], N_Q_HEADS))
    head_w = head_w.at[1, :kv_half].set(jnp.tile(p["k_sel_norm_w"][0], N_KV_HEADS))
    head_w = head_w.at[2, :kv_half].set(jnp.tile(p["k_win_norm_w"][0], N_KV_HEADS))
    o["head_w"] = head_w
    o["ones64"] = _ones_blocks(kv_half, NSA_HEAD_DIM)
    o["conv_w"] = p["conv_w"][0]
    o["conv_b"] = p["conv_b"][0][None, :]
    headp = jnp.zeros((SUBLANES, LANES), F32)
    headp = headp.at[0, :n_heads].set(p["dt_bias"][0]).at[1, :n_heads].set(p["a_log"][0])
    o["headp"] = headp
    o["dsk"] = jnp.repeat(p["d_skip"][0], SSM_HEAD_DIM)[None, :]
    o["ssm_nw"] = p["ssm_norm_w"][0][None, :]
    expand = np.zeros((LANES, d_inner), np.float32)
    for h in range(n_heads):
        expand[h, h * SSM_HEAD_DIM:(h + 1) * SSM_HEAD_DIM] = 1.0
    o["expand"] = jnp.asarray(expand, BF16)
    o["tri"] = jnp.asarray(np.tril(np.ones((SSD_CHUNK, SSD_CHUNK), np.float32)), BF16)
    o["onesg"] = jnp.ones((d_inner // N_SSM_GROUPS,) * 2, BF16)
    o["w_ssm_out"] = bf(p["w_ssm_out"][0])
    o["cmp_wk"] = bf(_block_diag(p["cmp_w_k"][0], N_KV_HEADS))
    o["cmp_wv"] = bf(_block_diag(p["cmp_w_v"][0], N_KV_HEADS))
    o["cmp_pek"] = jnp.tile(p["cmp_pe_k"][0], (1, N_KV_HEADS))
    o["cmp_pev"] = jnp.tile(p["cmp_pe_v"][0], (1, N_KV_HEADS))
    o["kc_nw"] = jnp.tile(p["k_cmp_norm_w"][0], N_KV_HEADS)[None, :]
    rel = p["rel_bias"]
    band = 2 * CMP_STRIDE
    i = np.arange(QT)[:, None]
    c = np.arange(band)[None, :]
    d_band = i + CMP_STRIDE * (band // 2) - CMP_STRIDE * c - (CMP_BLOCK - 1)
    d_band = np.concatenate([d_band, np.full((QT, 1), MAX_DISTANCE)], axis=1)
    tb = _bias_of_dist(rel, d_band)
    o["cmp_tb"] = jnp.pad(tb, ((0, 0), (0, 0), (0, 2 * band - tb.shape[2])))
    jk = np.arange(QT)[:, None]
    iq = np.arange(QT)[None, :]
    o["b0"] = _bias_of_dist(rel, iq - jk)
    o["b1"] = _bias_of_dist(rel, QT + iq - jk)
    o["c31"] = rel[N_BUCKETS - 1].astype(F32) * LOG2E
    gexp = np.zeros((3, LANES, qw), np.float32)
    for h in range(N_Q_HEADS):
        for k in range(3):
            gexp[k, 3 * h + k, h * NSA_HEAD_DIM:(h + 1) * NSA_HEAD_DIM] = 1.0
    o["gexp"] = jnp.asarray(gexp, BF16)
    o["w_nsa_out"] = bf(p["w_nsa_out"][0])
    o["w_out"] = bf(p["w_out"][0])
    o["norm_ffn"] = p["norm_ffn_w"][0][None, :]
    ne = p["w_router"].shape[2]
    o["w_router"] = _pad_cols(p["w_router"][0], LANES)
    o["b_router"] = jnp.full((1, LANES), NEG, F32).at[0, :ne].set(p["b_router"][0])
    o["w1"] = p["w_gate_up"][0]
    o["b1e"] = p["b_gate_up"][0][:, None, :]
    o["w2"] = p["w_down"][0]
    o["b2e"] = p["b_down"][0][:, None, :]
    return o


def _kv_layouts(kv, batch):
    n = kv.shape[0]
    s = n // batch
    half = kv.shape[1] // 2
    k = kv[:, :half].astype(BF16).reshape(batch, s, N_KV_HEADS, NSA_HEAD_DIM).transpose(0, 2, 1, 3)
    v = kv[:, half:].astype(BF16).reshape(batch, s // QT, QT, N_KV_HEADS, NSA_HEAD_DIM).transpose(0, 3, 1, 4, 2)
    extra = jnp.zeros(v.shape[:3] + (SUBLANES, QT), BF16).at[:, :, :, 0, :].set(1.0)
    return k, jnp.concatenate([v, extra], axis=3)


def _prompt_mixer(x, o, batch):
    z, xbc, dt = _proj(x, o["norm_mix"], o["ones64"], o["head_w"], o["w_ssm_in"], ["raw", "raw", "raw"], 256)
    q, kvc, kvs, kvw, brg, sg_ssm, sg_nsa = _proj(
        x, o["norm_mix"], o["ones64"], o["head_w"], o["w_nsa_in"], ["q", "raw", "ks", "kw", "sig", "sig", "sig"], 256)
    m_ssm, h_t = _ssd_prompt(xbc, z, dt, sg_ssm, o["conv_w"], o["conv_b"], o["headp"], o["dsk"], o["ssm_nw"],
                             o["expand"], o["tri"], o["w_ssm_out"], batch)
    kc, vc = _cmp_build(kvc, o["cmp_wk"], o["cmp_wv"], o["cmp_pek"], o["cmp_pev"], o["ones64"], o["kc_nw"], batch)
    ocmp, sel = _cmp_attn(q, brg, kc, vc, o["cmp_tb"], o["gexp"], batch)
    ks, vs_t = _kv_layouts(kvs, batch)
    kw, vw_t = _kv_layouts(kvw, batch)
    h = _nsa_main(o["c31"], q, brg, sel, ocmp, ks, vs_t, kw, vw_t, o["b0"], o["b1"], m_ssm, sg_nsa, x,
                  o["w_nsa_out"], o["w_out"], batch)
    return h, (kvc, kvs, kvw, h_t, xbc)


MOE_TOKEN_TILE = 384
MOE_GROUP_TILE = 512


def _moe(h_all, o):
    n, d = h_all.shape
    tm, tg = MOE_TOKEN_TILE, MOE_GROUP_TILE
    ne = o["w1"].shape[0]
    n_pad = -(-n // tm) * tm
    hp = jnp.pad(h_all, ((0, n_pad - n), (0, 0))) if n_pad != n else h_all
    hn, meta, cnt = _router(hp, o["norm_ffn"], o["w_router"], o["b_router"], tm, n)
    eid = meta[:n, META_EXPERT:META_EXPERT + TOP_K].astype(jnp.int32)
    rank = meta[:n, META_RANK:META_RANK + TOP_K].astype(jnp.int32)
    count = cnt[0, :ne].astype(jnp.int32)
    tiles = (count + tg - 1) // tg
    tile_end = jnp.cumsum(tiles)
    first_row = (tile_end - tiles) * tg
    pos = jnp.sum(jnp.where(eid[..., None] == jnp.arange(ne), first_row, 0), axis=-1) + rank
    n_tiles = -(-(n * TOP_K) // tg) + ne
    tile_expert = jnp.minimum(jnp.sum(tile_end[None, :] <= jnp.arange(n_tiles)[:, None], axis=1), ne - 1)
    sp = jnp.concatenate([tile_expert, tile_end[-1:]]).astype(jnp.int32)
    src = jnp.zeros((n_tiles * tg,), jnp.int32).at[pos.reshape(-1)].set(
        jnp.arange(n * TOP_K, dtype=jnp.int32) // TOP_K)
    y_sorted = _moe_group(sp, hn[src], o["w1"], o["b1e"], o["w2"], o["b2e"], tg)
    yk = y_sorted[pos.T.reshape(-1)].reshape(TOP_K, n, d // 2)
    if n_pad != n:
        yk = jnp.pad(yk, ((0, 0), (0, n_pad - n), (0, 0)))
    return _moe_combine(hp, yk, meta, tm)[:n]


def kernel(x_prompt, x_sample, cache_cmp, cache_sel, cache_win, state_ssm, state_conv, page_table, norm_mix_w, w_in,
           conv_w, conv_b, dt_bias, a_log, d_skip, ssm_norm_w, w_ssm_out, q_norm_w, k_cmp_norm_w, k_sel_norm_w,
           k_win_norm_w, cmp_pe_k, cmp_w_k, cmp_pe_v, cmp_w_v, rel_bias, w_nsa_out, w_out, norm_ffn_w, w_router,
           b_router, w_gate_up, b_gate_up, w_down, b_down):
    params = dict(norm_mix_w=norm_mix_w, w_in=w_in, conv_w=conv_w, conv_b=conv_b, dt_bias=dt_bias, a_log=a_log,
                  d_skip=d_skip, ssm_norm_w=ssm_norm_w, w_ssm_out=w_ssm_out, q_norm_w=q_norm_w,
                  k_cmp_norm_w=k_cmp_norm_w, k_sel_norm_w=k_sel_norm_w, k_win_norm_w=k_win_norm_w,
                  cmp_pe_k=cmp_pe_k, cmp_w_k=cmp_w_k, cmp_pe_v=cmp_pe_v, cmp_w_v=cmp_w_v, rel_bias=rel_bias,
                  w_nsa_out=w_nsa_out, w_out=w_out, norm_ffn_w=norm_ffn_w, w_router=w_router, b_router=b_router,
                  w_gate_up=w_gate_up, b_gate_up=b_gate_up, w_down=w_down, b_down=b_down)
    o = _prep(params)
    bsz, s, d = x_prompt.shape
    db, t, _ = x_sample.shape
    kvshape = (2, N_KV_HEADS, NSA_HEAD_DIM)
    hp, (kvc, kvs, kvw, h_t, xbc) = _prompt_mixer(x_prompt.reshape(bsz * s, d), o, bsz)
    assert t == 1, "the sample group decodes one token per sequence"
    hs, (kvc_s, kvs_s, win_s, h_s, conv_s) = _sample_mixer(
        x_sample.reshape(db, d), o, cache_cmp[0], cache_sel[0], cache_win[0], state_ssm[0], state_conv[0],
        page_table, rel_bias)
    y_all = _moe(jnp.concatenate([hp, hs], axis=0), o)
    wlen = min(WINDOW, s)
    n_heads = h_t.shape[2] // SSM_HEAD_DIM
    outs_p = (kvc.reshape((1, bsz, s) + kvshape), kvs.reshape((1, bsz, s) + kvshape),
              kvw.reshape((bsz, s) + kvshape)[None, :, s - wlen:],
              h_t.reshape(bsz, D_STATE, n_heads, SSM_HEAD_DIM).transpose(0, 2, 3, 1)[None],
              xbc.reshape(bsz, s, -1)[None, :, s - (D_CONV - 1):])
    outs_s = (kvc_s.reshape((1, db, t) + kvshape), kvs_s.reshape((1, db, t) + kvshape),
              win_s.reshape((1, db, win_s.shape[1]) + kvshape), h_s.reshape((1, db, n_heads, SSM_HEAD_DIM, D_STATE)),
              conv_s.reshape(1, db, D_CONV - 1, -1))
    return (y_all[:bsz * s].reshape(bsz, s, d), y_all[bsz * s:].reshape(db, t, d)) + outs_p + outs_s


def _sample_mixer(x, o, cache_cmp, cache_sel, cache_win, state_ssm, state_conv, page_table, rel):
    n = x.shape[0]
    z, xbc, dt = _proj(x, o["norm_mix"], o["ones64"], o["head_w"], o["w_ssm_in"], ["raw", "raw", "raw"], n)
    q, kvc, kvs, kvw, brg, sg_ssm, sg_nsa = _proj(
        x, o["norm_mix"], o["ones64"], o["head_w"], o["w_nsa_in"], ["q", "raw", "ks", "kw", "sig", "sig", "sig"], n)
    d_inner = z.shape[1]
    n_heads = d_inner // SSM_HEAD_DIM
    conv_new, xs, bm, ct, dtx_t, dec = _mamba_prep(xbc, state_conv.reshape(n, -1), dt, o["conv_w"], o["conv_b"],
                                                   o["headp"], o["expand"])
    h_new, y_t = _mamba_state(dec[:, :n_heads], state_ssm.reshape(n, d_inner, D_STATE), dtx_t, bm, ct)
    pool, page = cache_cmp.shape[0], cache_cmp.shape[1]
    n_pages = page_table.shape[1]
    past = n_pages * page
    tq = past
    assert page >= MAX_DISTANCE and page % SEL_BLOCK == 0 and past % CMP_STRIDE == 0
    half = kvc.shape[1] // 2
    nch = half // LANES
    cmp_t = jnp.transpose(cache_cmp, (0, 2, 3, 4, 1)).reshape(pool, 2 * nch, LANES, page)
    sel_t = jnp.transpose(cache_sel, (0, 2, 3, 4, 1))
    nstr = -(-(past // CMP_STRIDE + 1) // SUBLANES) * SUBLANES
    w_kv = jnp.stack([o["cmp_wk"], o["cmp_wv"]])
    pe_kv = jnp.stack([o["cmp_pek"], o["cmp_pev"]])
    tok = _page_cmp(page_table, cmp_t, kvc[:, None, :], w_kv, _cmp_const(pe_kv, w_kv), o["ones64"], o["kc_nw"], nstr)
    pad_heads = lambda b: jnp.pad(b.reshape(N_KV_HEADS, Q_PER_KV, -1), ((0, 0), (0, SUBLANES - Q_PER_KV), (0, 0)))
    tbc = pad_heads(_bias_of_dist(rel, tq - (CMP_STRIDE * np.arange(nstr) + CMP_BLOCK - 1)))
    wlen = cache_win.shape[1]
    wrows = -(-(wlen + 1) // SUBLANES) * SUBLANES
    past_w = past - wlen
    tbw = pad_heads(_bias_of_dist(rel, tq - (past_w + np.arange(wrows))))
    n_past_sel = past // SEL_BLOCK
    nsel = n_past_sel + 1
    nsel_pad = -(-nsel // LANES) * LANES
    o_cw, selmask, win_new = _sample_cw(q[:, None, :], brg[:, None, :], tok, cache_win.reshape(n, wlen, -1),
                                        kvw[:, None, :], tbc, tbw, o["gexp"], tq, past_w, nsel_pad)
    nblk = min(N_SEL_BLOCKS, nsel)
    picked = selmask[:, :N_KV_HEADS, :nsel] > 0.5
    order = jnp.cumsum(picked, axis=-1) - 1
    hit = picked[..., None] & (order[..., None] == jnp.arange(nblk))
    idx = jnp.sum(jnp.where(hit, jnp.arange(nsel)[:, None], 0), axis=2)
    is_past = idx < n_past_sel
    jp = jnp.minimum(idx, n_past_sel - 1)
    per_page = page // SEL_BLOCK
    pg = jp // per_page
    phys = jnp.take_along_axis(page_table, pg.reshape(n, -1), axis=1).astype(jnp.int32)
    lane_blk = (np.arange(page) // SEL_BLOCK)[None, None, None, :]
    attended = is_past[..., None] & (lane_blk == (jp % per_page)[..., None])
    code = jnp.where(attended, jnp.where((pg == n_pages - 1)[..., None], 2.0, 1.0), 0.0).reshape(n, N_KV_HEADS, -1)
    new_on = jnp.any(idx >= n_past_sel, axis=-1).astype(F32)
    code = jnp.concatenate([code, jnp.broadcast_to(new_on[..., None], code.shape)], axis=1)
    tbl = pad_heads(_bias_of_dist(rel, tq - ((n_pages - 1) * page + np.arange(page))))
    f0 = rel[_bucket_lut()[0]].astype(F32) * LOG2E
    y_nsa = _sample_sel(phys, o["c31"], f0, sel_t, q[:, None, :], brg[:, None, :], kvs[:, None, :], code, o_cw, tbl,
                        o["gexp"])
    h = _sample_merge(y_t, xs, z, sg_ssm, o["dsk"], o["ssm_nw"], o["onesg"], o["w_ssm_out"], y_nsa[:, 0], sg_nsa, x,
                      o["w_nsa_out"], o["w_out"])
    return h, (kvc, kvs, win_new, h_new, conv_new)
```

```python
import functools
import math

import jax
import jax.numpy as jnp
import numpy as np
from jax import lax
from jax.experimental import pallas as pl
from jax.experimental.pallas import tpu as pltpu

F32 = jnp.float32
BF16 = jnp.bfloat16

SSM_HEAD_DIM = 64
N_SSM_GROUPS = 4
D_STATE = 128
D_CONV = 4
SSD_CHUNK = 128
NSA_HEAD_DIM = 64
N_Q_HEADS = 16
N_KV_HEADS = 4
Q_PER_KV = N_Q_HEADS // N_KV_HEADS
CMP_BLOCK = 32
CMP_STRIDE = 16
SEL_BLOCK = 64
N_SEL_BLOCKS = 16
WINDOW = 512
N_BUCKETS = 32
MAX_DISTANCE = 128
TOP_K = 4
SWIGLU_LIMIT = 7.0
SWIGLU_ALPHA = 1.702
RMS_EPS = 1e-6
LOG2E = math.log2(math.e)
NEG = -1e30
FORCE_SCORE = 1e9

LANES = 128
SUBLANES = 8
QT = 128
FAR_TILES_PER_TRIP = 4
VMEM_LIMIT = 56 * 1024 * 1024


def _cparams(sem):
    return pltpu.CompilerParams(dimension_semantics=sem, vmem_limit_bytes=VMEM_LIMIT)


def _bdot(a, b):
    return jnp.dot(a.astype(BF16), b.astype(BF16), preferred_element_type=F32)


def _bdot_nt(a, b):
    return lax.dot_general(a.astype(BF16), b.astype(BF16), (((1,), (1,)), ((), ())),
                           preferred_element_type=F32)


def _split3(a):
    hi = a.astype(BF16)
    r = a - hi.astype(F32)
    mid = r.astype(BF16)
    lo = (r - mid.astype(F32)).astype(BF16)
    return hi, mid, lo


def _dot3(a, b):
    hi, mid, lo = _split3(a)
    d = lambda p: jnp.dot(p, b, preferred_element_type=F32)
    return (d(hi) + d(mid)) + d(lo)


def _dot2(a, b):
    hi = a.astype(BF16)
    mid = (a - hi.astype(F32)).astype(BF16)
    return jnp.dot(hi, b, preferred_element_type=F32) + jnp.dot(mid, b, preferred_element_type=F32)


def _dot3_l(a, b):
    hi, mid, lo = _split3(b)
    d = lambda p: jnp.dot(a, p, preferred_element_type=F32)
    return (d(hi) + d(mid)) + d(lo)


def _dot3_nt_l(a, b):
    hi, mid, lo = _split3(b)
    d = lambda p: lax.dot_general(a, p, (((1,), (1,)), ((), ())), preferred_element_type=F32)
    return (d(hi) + d(mid)) + d(lo)


def _seg_sum(y, ones_blk):
    c = ones_blk.shape[0]
    outs = []
    for k in range(y.shape[1] // c):
        outs.append(_dot3(y[:, k * c:(k + 1) * c], ones_blk))
    return outs[0] if len(outs) == 1 else jnp.concatenate(outs, axis=1)


def _sigmoid(x):
    return 1.0 / (1.0 + jnp.exp(-x))


def _rms_rows(x, w):
    ms = jnp.mean(x * x, axis=-1, keepdims=True)
    return (x * lax.rsqrt(ms + RMS_EPS)) * w


def _head_rms(y, ones64, w, seg):
    ms = _seg_sum(y * y, ones64) * (1.0 / seg)
    return (y * lax.rsqrt(ms + RMS_EPS)) * w


def _proj_kernel(kinds, x_ref, nw_ref, ones_ref, hw_ref, *refs):
    n = len(kinds)
    w_refs, o_refs = refs[:n], refs[n:]
    xn = _rms_rows(x_ref[...], nw_ref[...]).astype(BF16)
    ones64 = ones_ref[...]
    for kind, w_ref, o_ref in zip(kinds, w_refs, o_refs):
        y = jnp.dot(xn, w_ref[...], preferred_element_type=F32)
        if kind == "sig":
            y = _sigmoid(y)
        elif kind == "q":
            y = _head_rms(y, ones64, hw_ref[0:1, :], NSA_HEAD_DIM) * (NSA_HEAD_DIM ** -0.5 * LOG2E)
        elif kind in ("ks", "kw"):
            row = 1 if kind == "ks" else 2
            half = y.shape[1] // 2
            k = _head_rms(y[:, :half], ones64, hw_ref[row:row + 1, :half], NSA_HEAD_DIM)
            y = jnp.concatenate([k, y[:, half:]], axis=1)
        o_ref[...] = y


def _proj(x, norm_w, ones64, head_w, weights, kinds, tm):
    n, d = x.shape
    assert n % tm == 0
    const = lambda i: (0, 0)
    in_specs = [pl.BlockSpec((tm, d), lambda i: (i, 0)),
                pl.BlockSpec((1, d), const),
                pl.BlockSpec(ones64.shape, const),
                pl.BlockSpec(head_w.shape, const)]
    in_specs += [pl.BlockSpec(w.shape, const) for w in weights]
    out_specs = [pl.BlockSpec((tm, w.shape[1]), lambda i: (i, 0)) for w in weights]
    out_shape = [jax.ShapeDtypeStruct((n, w.shape[1]), F32) for w in weights]
    return pl.pallas_call(
        functools.partial(_proj_kernel, tuple(kinds)),
        grid=(n // tm,), in_specs=in_specs, out_specs=out_specs, out_shape=out_shape,
        compiler_params=_cparams(("parallel",)), name="in_proj",
    )(x, norm_w, ones64, head_w, *weights)


def _softplus(x):
    return jnp.maximum(x, 0.0) + jnp.log1p(jnp.exp(-jnp.abs(x)))


def _ssd_kernel(xbc_ref, z_ref, dt_ref, sg_ref, cw_ref, cb_ref, hp_ref, dsk_ref, nw_ref, e_ref, tri_ref,
                wout_ref, o_ref, ht_ref, ht_s, ext_s):
    c = pl.program_id(1)
    q = SSD_CHUNK
    d_inner = z_ref.shape[1]
    gw = d_inner // N_SSM_GROUPS
    hpg = gw // SSM_HEAD_DIM
    tail = SUBLANES

    @pl.when(c == 0)
    def _():
        ht_s[...] = jnp.zeros_like(ht_s)
        ext_s[0:tail, :] = jnp.zeros((tail, ext_s.shape[1]), F32)

    xb = xbc_ref[...]
    ext_s[tail:tail + q, :] = xb
    u = cb_ref[...] + cw_ref[D_CONV - 1:D_CONV, :] * xb
    for k in range(1, D_CONV):
        u = u + cw_ref[D_CONV - 1 - k:D_CONV - k, :] * ext_s[tail - k:tail - k + q, :]
    ext_s[0:tail, :] = xb[q - tail:q, :]
    u = u * _sigmoid(u)
    xs = u[:, :d_inner]
    bm = u[:, d_inner:d_inner + N_SSM_GROUPS * D_STATE]
    cm = u[:, d_inner + N_SSM_GROUPS * D_STATE:]

    dtv = _softplus(dt_ref[...] + hp_ref[0:1, :])
    a = dtv * (-jnp.exp(hp_ref[1:2, :]))
    acum = _dot3_l(tri_ref[...], a)
    acum_t = acum.T
    eacum = jnp.exp(acum)
    w_end = jnp.exp(acum[q - 1:q, :] - acum) * dtv
    e = e_ref[...]
    dt_x = _dot2(dtv, e)
    we_x = _dot2(w_end, e)
    ea_x = _dot2(eacum, e)
    xdt = (xs * dt_x).astype(BF16)
    xdtw = (xs * we_x).astype(BF16)
    ii = lax.broadcasted_iota(jnp.int32, (q, q), 0)
    jj = lax.broadcasted_iota(jnp.int32, (q, q), 1)
    causal = jj <= ii

    groups = range(N_SSM_GROUPS)
    gsl = [slice(g * gw, (g + 1) * gw) for g in groups]
    cgs = [cm[:, g * D_STATE:(g + 1) * D_STATE].astype(BF16) for g in groups]
    bgs = [bm[:, g * D_STATE:(g + 1) * D_STATE] for g in groups]
    cbms = [_bdot_nt(cgs[g], bgs[g]) for g in groups]
    hgs = [ht_s[:, gsl[g]] for g in groups]
    y_inter = [jnp.dot(cgs[g], hgs[g].astype(BF16), preferred_element_type=F32) for g in groups]
    sts = [jnp.dot(bgs[g].T.astype(BF16), xdtw[:, gsl[g]], preferred_element_type=F32) for g in groups]
    ys = []
    for g in groups:
        yh = []
        for r in range(hpg):
            h = g * hpg + r
            seg = acum[:, h:h + 1] - acum_t[h:h + 1, :]
            dec = jnp.exp(jnp.where(causal, seg, NEG))
            m = (cbms[g] * dec).astype(BF16)
            yh.append(jnp.dot(m, xdt[:, h * SSM_HEAD_DIM:(h + 1) * SSM_HEAD_DIM], preferred_element_type=F32))
        ys.append(y_inter[g] * ea_x[:, gsl[g]] + jnp.concatenate(yh, axis=1))
        ht_s[:, gsl[g]] = hgs[g] * ea_x[q - 1:q, gsl[g]] + sts[g]
    y = jnp.concatenate(ys, axis=1) + dsk_ref[...] * xs
    zz = z_ref[...]
    y = y * (zz * _sigmoid(zz))
    normed = []
    for g in groups:
        yg = y[:, gsl[g]]
        normed.append(yg * lax.rsqrt(jnp.mean(yg * yg, axis=-1, keepdims=True) + RMS_EPS))
    y = jnp.concatenate(normed, axis=1) * nw_ref[...]
    o_ref[...] = sg_ref[...] * _bdot(y, wout_ref[...])

    @pl.when(c == pl.num_programs(1) - 1)
    def _():
        ht_ref[0] = ht_s[...]


def _ssd_prompt(xbc, z, dt, sg, conv_w, conv_b, headp, dsk, nw, expand, tri, wout, batch):
    n, conv_dim = xbc.shape
    d_inner = z.shape[1]
    d_model = wout.shape[1]
    q = SSD_CHUNK
    nc = n // batch // q
    const = lambda b, c: (0, 0)
    rows = lambda b, c: (b * nc + c, 0)
    in_specs = [pl.BlockSpec((q, conv_dim), rows), pl.BlockSpec((q, d_inner), rows),
                pl.BlockSpec((q, LANES), rows), pl.BlockSpec((q, d_model), rows),
                pl.BlockSpec(conv_w.shape, const), pl.BlockSpec(conv_b.shape, const),
                pl.BlockSpec(headp.shape, const), pl.BlockSpec(dsk.shape, const),
                pl.BlockSpec(nw.shape, const), pl.BlockSpec(expand.shape, const),
                pl.BlockSpec(tri.shape, const), pl.BlockSpec(wout.shape, const)]
    out_specs = [pl.BlockSpec((q, d_model), rows),
                 pl.BlockSpec((1, D_STATE, d_inner), lambda b, c: (b, 0, 0))]
    out_shape = [jax.ShapeDtypeStruct((n, d_model), F32),
                 jax.ShapeDtypeStruct((batch, D_STATE, d_inner), F32)]
    return pl.pallas_call(
        _ssd_kernel, grid=(batch, nc), in_specs=in_specs, out_specs=out_specs, out_shape=out_shape,
        scratch_shapes=[pltpu.VMEM((D_STATE, d_inner), F32), pltpu.VMEM((SUBLANES + q, conv_dim), F32)],
        compiler_params=_cparams(("parallel", "arbitrary")), name="ssd_prompt",
    )(xbc, z, dt, sg, conv_w, conv_b, headp, dsk, nw, expand, tri, wout)


def _cmp_build_kernel(kvc_ref, wk_ref, wv_ref, pek_ref, pev_ref, ones_ref, nw_ref, kc_ref, vc_ref):
    ns = kc_ref.shape[0]
    kw = kc_ref.shape[1]
    lo_k = hi_k = lo_v = hi_v = None
    add = lambda acc, v: v if acc is None else acc + v
    nch = 2 * kw // LANES
    for i in range(CMP_STRIDE):
        x = jnp.concatenate([kvc_ref[pl.ds(nch * i + c, ns, stride=nch * CMP_STRIDE), :] for c in range(nch)], axis=1)
        xk, xv = x[:, :kw], x[:, kw:]
        j = CMP_STRIDE + i
        lo_k = add(lo_k, _bdot(xk + pek_ref[i:i + 1, :], wk_ref[i]))
        hi_k = add(hi_k, _bdot(xk + pek_ref[j:j + 1, :], wk_ref[j]))
        lo_v = add(lo_v, _bdot(xv + pev_ref[i:i + 1, :], wv_ref[i]))
        hi_v = add(hi_v, _bdot(xv + pev_ref[j:j + 1, :], wv_ref[j]))
    kc = lo_k + pltpu.roll(hi_k, ns - 1, axis=0)
    vc_ref[...] = lo_v + pltpu.roll(hi_v, ns - 1, axis=0)
    kc_ref[...] = _head_rms(kc, ones_ref[...], nw_ref[...], NSA_HEAD_DIM)


def _cmp_build(kvc, wk, wv, pek, pev, ones64, nw, batch):
    n, w = kvc.shape
    s = n // batch
    ns = s // CMP_STRIDE
    kw = w // 2
    c2 = lambda b: (0, 0)
    c3 = lambda b: (0, 0, 0)
    nch = w // LANES
    kvc = kvc.reshape(n * nch, LANES)
    return pl.pallas_call(
        _cmp_build_kernel, grid=(batch,),
        in_specs=[pl.BlockSpec((s * nch, LANES), lambda b: (b, 0)), pl.BlockSpec(wk.shape, c3),
                  pl.BlockSpec(wv.shape, c3),
                  pl.BlockSpec(pek.shape, c2), pl.BlockSpec(pev.shape, c2), pl.BlockSpec(ones64.shape, c2),
                  pl.BlockSpec(nw.shape, c2)],
        out_specs=[pl.BlockSpec((ns, kw), lambda b: (b, 0)), pl.BlockSpec((ns, kw), lambda b: (b, 0))],
        out_shape=[jax.ShapeDtypeStruct((batch * ns, kw), F32)] * 2,
        compiler_params=_cparams(("parallel",)), name="cmp_build",
    )(kvc, wk, wv, pek, pev, ones64, nw)


def _rank_select(v, n_keep):
    n = v.shape[0]
    rows = SUBLANES
    sub = lax.broadcasted_iota(jnp.int32, (rows, v.shape[1]), 0)
    chunks = [v[c * rows:(c + 1) * rows, :] for c in range(n // rows)]
    cnts = [jnp.zeros(ch.shape, F32) for ch in chunks]
    for i in range(n):
        ri = v[i:i + 1, :]
        for c, ch in enumerate(chunks):
            if c * rows > i:
                beats = jnp.where(ri >= ch, 1.0, 0.0)
            elif (c + 1) * rows - 1 <= i:
                beats = jnp.where(ri > ch, 1.0, 0.0)
            else:
                beats = jnp.where(sub > i - c * rows, jnp.where(ri >= ch, 1.0, 0.0), jnp.where(ri > ch, 1.0, 0.0))
            cnts[c] = cnts[c] + beats
    return jnp.where(jnp.concatenate(cnts, axis=0) < n_keep, 1.0, 0.0)


def _cmp_attn_kernel(q_ref, brg_ref, kc_ref, vc_ref, tb_ref, gexp_ref, ocmp_ref, sel_ref):
    qi = pl.program_id(1)
    t0 = qi * QT
    ns = kc_ref.shape[0]
    nsel = sel_ref.shape[2]
    nb = tb_ref.shape[2]
    band = 2 * CMP_STRIDE
    tt = t0 + lax.broadcasted_iota(jnp.int32, (QT, ns), 0)
    nn = lax.broadcasted_iota(jnp.int32, (QT, ns), 1)
    mask = (CMP_STRIDE * nn + (CMP_BLOCK - 1)) <= tt
    first = (QT // CMP_STRIDE) * qi - band // 2
    cc = lax.broadcasted_iota(jnp.int32, (nb, ns), 0)
    n2 = lax.broadcasted_iota(jnp.int32, (nb, ns), 1)
    shift = jnp.where(((cc < band) & (n2 == first + cc)) | ((cc == band) & (n2 < first)), 1.0, 0.0).astype(BF16)
    jb = lax.broadcasted_iota(jnp.int32, (nsel, ns), 0) * SEL_BLOCK
    cs = lax.broadcasted_iota(jnp.int32, (nsel, ns), 1) * CMP_STRIDE
    cover_t = jnp.where((cs < jb + SEL_BLOCK) & (cs + CMP_BLOCK > jb), 1.0, 0.0).astype(BF16)
    jidx = lax.broadcasted_iota(jnp.int32, (nsel, QT), 0)
    tq = t0 + lax.broadcasted_iota(jnp.int32, (nsel, QT), 1)
    valid = jidx * SEL_BLOCK <= tq
    cur = tq // SEL_BLOCK
    forced = valid & ((jidx == 0) | (jidx == cur) | (jidx == cur - 1))

    q = q_ref[...].astype(BF16)
    head_cols = lambda a, g: a[:, g * NSA_HEAD_DIM:(g + 1) * NSA_HEAD_DIM]
    kgs = [head_cols(kc_ref, g).astype(BF16) for g in range(N_KV_HEADS)]
    vgs = [head_cols(vc_ref, g).astype(BF16) for g in range(N_KV_HEADS)]
    raw = [_bdot_nt(head_cols(q, h), kgs[h // Q_PER_KV]) + _dot3(tb_ref[h], shift) for h in range(N_Q_HEADS)]
    ps = []
    for s in raw:
        s = jnp.where(mask, s, NEG)
        m = jnp.max(s, axis=-1, keepdims=True)
        p = jnp.where(mask, jnp.exp2(s - m), 0.0)
        l = jnp.sum(p, axis=-1, keepdims=True)
        ps.append(p * jnp.where(l > 0.0, 1.0 / l, 0.0))
    outs = [_bdot(ps[h], vgs[h // Q_PER_KV]) for h in range(N_Q_HEADS)]
    imps = []
    for g in range(N_KV_HEADS):
        psum = functools.reduce(lambda a, b: a + b, ps[g * Q_PER_KV:(g + 1) * Q_PER_KV])
        imps.append(_dot3_nt_l(cover_t, psum))
    for g in range(N_KV_HEADS):
        v = jnp.where(forced, FORCE_SCORE, jnp.where(valid, imps[g], NEG))
        sel_ref[0, g] = _rank_select(v, N_SEL_BLOCKS)
    ocmp_ref[...] = jnp.concatenate(outs, axis=1) * _dot3(brg_ref[...], gexp_ref[0])


def _cmp_attn(q, brg, kc, vc, tb, gexp, batch):
    n, qw = q.shape
    s = n // batch
    nq = s // QT
    ns, kw = kc.shape[0] // batch, kc.shape[1]
    nsel = s // SEL_BLOCK
    rows = lambda b, i: (b * nq + i, 0)
    per_b = lambda b, i: (b, 0)
    return pl.pallas_call(
        _cmp_attn_kernel, grid=(batch, nq),
        in_specs=[pl.BlockSpec((QT, qw), rows), pl.BlockSpec((QT, LANES), rows),
                  pl.BlockSpec((ns, kw), per_b), pl.BlockSpec((ns, kw), per_b),
                  pl.BlockSpec(tb.shape, lambda b, i: (0, 0, 0)),
                  pl.BlockSpec((1,) + gexp.shape[1:], lambda b, i: (0, 0, 0))],
        out_specs=[pl.BlockSpec((QT, qw), rows), pl.BlockSpec((1, N_KV_HEADS, nsel, QT), lambda b, i: (b, 0, 0, i))],
        out_shape=[jax.ShapeDtypeStruct((n, qw), F32), jax.ShapeDtypeStruct((batch, N_KV_HEADS, nsel, s), F32)],
        compiler_params=_cparams(("parallel", "parallel")), name="cmp_attn",
    )(q, brg, kc, vc, tb, gexp)


def _attn_round(states, raw, keeps, biases, vts, shifts):
    stats = []
    for st, s, keep, bias, shift in zip(states, raw, keeps, biases, shifts):
        s = _masked_scores(s, keep, bias)
        tile_max = jnp.max(s, axis=0, keepdims=True)
        if shift is not None:
            tile_max = tile_max + shift
        m_new = tile_max if st is None else jnp.maximum(st[0], tile_max)
        alpha = None if st is None else jnp.exp2(st[0] - m_new)
        p = jnp.exp2(s - (m_new if shift is None else m_new - shift)).astype(BF16)
        stats.append((m_new, alpha, p))
    out = []
    for st, (m_new, alpha, p), vt in zip(states, stats, vts):
        pv = jnp.dot(vt, p, preferred_element_type=F32)
        out.append((m_new, pv if st is None else alpha * st[1] + pv))
    return out


def _masked_scores(s, keep, bias=None):
    parts = []
    for r in range(s.shape[1] // QT):
        v = s[:, r * QT:(r + 1) * QT]
        parts.append(jnp.where(keep, v if bias is None else v + bias[r], NEG))
    return jnp.concatenate(parts, axis=1)


def _nsa_main_kernel(c31_ref, q_ref, brg_ref, sel_ref, ocmp_ref, ks_ref, vs_ref, kw_ref, vw_ref, b0_ref, b1_ref,
                     mssm_ref, sgn_ref, x_ref, wn_ref, wo_ref, h_ref):
    qi = pl.program_id(1)
    nwt = WINDOW // QT
    blk_per_tile = QT // SEL_BLOCK
    jk = lax.broadcasted_iota(jnp.int32, (QT, QT), 0)
    iq = lax.broadcasted_iota(jnp.int32, (QT, QT), 1)
    causal_t = jk <= iq
    q_t = q_ref[...].T.astype(BF16)
    gates_t = brg_ref[...].T

    def sel_keep(g, kt, tiles=1):
        rows = [jnp.broadcast_to(sel_ref[0, g, pl.ds(blk_per_tile * kt + b, 1), :], (SEL_BLOCK, QT))
                for b in range(blk_per_tile * tiles)]
        return jnp.concatenate(rows, axis=0) > 0.5

    def key_tile(ref, g, kt):
        return ref[0, g, pl.ds(pl.multiple_of(kt * QT, QT), QT), :]

    groups = range(N_KV_HEADS)
    heads = [[g * Q_PER_KV + r for r in range(Q_PER_KV)] for g in groups]
    qg = [jnp.concatenate([q_t[h * NSA_HEAD_DIM:(h + 1) * NSA_HEAD_DIM, :] for h in heads[g]], axis=1) for g in groups]
    far = [jnp.concatenate([jnp.full((1, QT), c31_ref[h], F32) for h in heads[g]], axis=1) for g in groups]
    near0 = [[b0_ref[h] for h in heads[g]] for g in groups]
    near1 = [[b1_ref[h] for h in heads[g]] for g in groups]
    scores = lambda ref, g, kt: jnp.dot(key_tile(ref, g, kt), qg[g], preferred_element_type=F32)
    prev = jnp.maximum(qi - 1, 0)
    has_prev = jnp.broadcast_to(qi >= 1, (QT, QT))

    none4 = [None] * N_KV_HEADS
    st = _attn_round(
        none4 + none4,
        [scores(ks_ref, g, qi) for g in groups] + [scores(kw_ref, g, qi) for g in groups],
        [causal_t & sel_keep(g, qi) for g in groups] + [causal_t] * N_KV_HEADS, near0 + near0,
        [vs_ref[0, g, qi] for g in groups] + [vw_ref[0, g, qi] for g in groups], none4 + none4)
    st = _attn_round(
        st, [scores(ks_ref, g, prev) for g in groups] + [scores(kw_ref, g, prev) for g in groups],
        [sel_keep(g, prev) & has_prev for g in groups] + [has_prev] * N_KV_HEADS, near1 + near1,
        [vs_ref[0, g, prev] for g in groups] + [vw_ref[0, g, prev] for g in groups], none4 + none4)
    sel, win = st[:N_KV_HEADS], st[N_KV_HEADS:]
    backs = list(range(nwt, 1, -1))
    kts = [jnp.maximum(qi - back, 0) for back in backs]
    keep_w = jnp.concatenate(
        [jnp.where((qi >= back) & ((jk > iq) if back == nwt else True), 1.0, 0.0) * jnp.ones((QT, QT), F32)
         for back in backs], axis=0) > 0.5
    win = _attn_round(
        win, [jnp.dot(jnp.concatenate([key_tile(kw_ref, g, kt) for kt in kts], axis=0), qg[g],
                      preferred_element_type=F32) for g in groups],
        [keep_w] * N_KV_HEADS, none4,
        [jnp.concatenate([vw_ref[0, g, kt] for kt in kts], axis=1) for g in groups], far)

    def sel_body(kt, state):
        return tuple(_attn_round(list(state), [scores(ks_ref, g, kt) for g in groups],
                                 [sel_keep(g, kt) for g in groups], none4, [vs_ref[0, g, kt] for g in groups], far))

    wide = FAR_TILES_PER_TRIP

    def sel_body_wide(j, state):
        kt = wide * j
        raw = [jnp.dot(ks_ref[0, g, pl.ds(pl.multiple_of(kt * QT, QT), wide * QT), :], qg[g],
                       preferred_element_type=F32) for g in groups]
        keeps = [sel_keep(g, kt, wide) for g in groups]
        vts = [jnp.concatenate([vs_ref[0, g, kt + i] for i in range(wide)], axis=1) for g in groups]
        return tuple(_attn_round(list(state), raw, keeps, none4, vts, far))

    n_far = jnp.maximum(qi - 1, 0)
    sel = lax.fori_loop(0, n_far // wide, sel_body_wide, tuple(sel))
    sel = lax.fori_loop(wide * (n_far // wide), n_far, sel_body, sel)

    o_t = []
    dh = NSA_HEAD_DIM
    for g in groups:
        o_s = sel[g][1][0:dh, :] * (1.0 / sel[g][1][dh:dh + 1, :])
        o_w = win[g][1][0:dh, :] * (1.0 / win[g][1][dh:dh + 1, :])
        for r, h in enumerate(heads[g]):
            sl = slice(r * QT, (r + 1) * QT)
            o_t.append(gates_t[3 * h + 1:3 * h + 2, :] * o_s[:, sl] + gates_t[3 * h + 2:3 * h + 3, :] * o_w[:, sl])

    pairs = [jnp.concatenate(o_t[2 * k:2 * k + 2], axis=0).T for k in range(len(o_t) // 2)]
    y_nsa = jnp.concatenate(pairs, axis=1) + ocmp_ref[...]
    u = mssm_ref[...] + sgn_ref[...] * _bdot(y_nsa, wn_ref[...])
    h_ref[...] = x_ref[...] + _bdot(u, wo_ref[...])


def _nsa_main(c31, q, brg, sel, ocmp, ks, vs_t, kw, vw_t, b0, b1, mssm, sgn, x, wn, wo, batch):
    n, qw = q.shape
    d = x.shape[1]
    s = n // batch
    nq = s // QT
    rows = lambda b, i: (b * nq + i, 0)
    kspec = pl.BlockSpec((1,) + ks.shape[1:], lambda b, i: (b, 0, 0, 0))
    vspec = pl.BlockSpec((1,) + vs_t.shape[1:], lambda b, i: (b, 0, 0, 0, 0))
    c2 = lambda b, i: (0, 0)
    c3 = lambda b, i: (0, 0, 0)
    return pl.pallas_call(
        _nsa_main_kernel, grid=(batch, nq),
        in_specs=[pl.BlockSpec(memory_space=pltpu.SMEM),
                  pl.BlockSpec((QT, qw), rows), pl.BlockSpec((QT, LANES), rows),
                  pl.BlockSpec((1,) + sel.shape[1:3] + (QT,), lambda b, i: (b, 0, 0, i)),
                  pl.BlockSpec((QT, qw), rows), kspec, vspec, kspec, vspec,
                  pl.BlockSpec(b0.shape, c3), pl.BlockSpec(b1.shape, c3),
                  pl.BlockSpec((QT, d), rows), pl.BlockSpec((QT, d), rows), pl.BlockSpec((QT, d), rows),
                  pl.BlockSpec(wn.shape, c2), pl.BlockSpec(wo.shape, c2)],
        out_specs=pl.BlockSpec((QT, d), rows), out_shape=jax.ShapeDtypeStruct((n, d), F32),
        compiler_params=_cparams(("parallel", "parallel")), name="nsa_main",
    )(c31, q, brg, sel, ocmp, ks, vs_t, kw, vw_t, b0, b1, mssm, sgn, x, wn, wo)


META_EXPERT, META_RANK, META_GATE = 0, TOP_K, 2 * TOP_K


def _pack_bf16_pair(x):
    c = x.shape[1] // 2
    hi = pltpu.bitcast(x[:, :c].astype(BF16).astype(F32), jnp.int32)
    lo = pltpu.bitcast(x[:, c:].astype(BF16).astype(F32), jnp.int32)
    return hi | lax.shift_right_logical(lo, 16)


def _unpack_bf16_pair(w):
    hi = pltpu.bitcast(w & jnp.int32(-65536), F32).astype(BF16)
    lo = pltpu.bitcast(lax.shift_left(w, 16), F32).astype(BF16)
    return hi, lo


def _router_kernel(h_ref, nw_ref, wr_ref, br_ref, tri_ref, hn_ref, meta_ref, cnt_ref, base_s, *, n_real):
    i = pl.program_id(0)
    tm = h_ref.shape[0]

    @pl.when(i == 0)
    def _():
        base_s[...] = jnp.zeros_like(base_s)

    hn = _rms_rows(h_ref[...], nw_ref[...])
    hn_ref[...] = _pack_bf16_pair(hn)
    a_hi = hn.astype(BF16)
    a_lo = (hn - a_hi.astype(F32)).astype(BF16)
    w = wr_ref[...]
    w_hi = w.astype(BF16)
    w_lo = (w - w_hi.astype(F32)).astype(BF16)
    d = lambda a, b: jnp.dot(a, b, preferred_element_type=F32)
    v = (d(a_hi, w_hi) + (d(a_hi, w_lo) + d(a_lo, w_hi))) + br_ref[...]
    lane = lax.broadcasted_iota(jnp.int32, v.shape, 1)
    tops, idxs, hots = [], [], []
    for _ in range(TOP_K):
        m = jnp.max(v, axis=-1, keepdims=True)
        idx = jnp.min(jnp.where(v == m, lane, LANES), axis=-1, keepdims=True)
        hot = lane == idx
        tops.append(m)
        idxs.append(idx)
        hots.append(hot)
        v = jnp.where(hot, NEG, v)
    es = [jnp.exp(t - tops[0]) for t in tops]
    inv = 1.0 / functools.reduce(lambda a, b: a + b, es)
    row = i * tm + lax.broadcasted_iota(jnp.int32, v.shape, 0)
    onehot = jnp.zeros(v.shape, F32)
    for hot in hots:
        onehot = onehot + jnp.where(hot & (row < n_real), 1.0, 0.0)
    before = jnp.dot(tri_ref[...], onehot.astype(BF16), preferred_element_type=F32) + base_s[0:1, :]
    meta = jnp.zeros(v.shape, F32)
    for k in range(TOP_K):
        rank = jnp.sum(jnp.where(hots[k], before, 0.0), axis=-1, keepdims=True)
        meta = meta + jnp.where(lane == META_EXPERT + k, idxs[k].astype(F32), 0.0)
        meta = meta + jnp.where(lane == META_RANK + k, rank, 0.0)
        meta = meta + jnp.where(lane == META_GATE + k, es[k] * inv, 0.0)
    meta_ref[...] = meta
    base_s[0:1, :] = base_s[0:1, :] + jnp.sum(onehot, axis=0, keepdims=True)
    cnt_ref[...] = jnp.broadcast_to(base_s[0:1, :], cnt_ref.shape)


def _router(h, nw, wr, br, tm, n_real):
    n, d = h.shape
    c2 = lambda i: (0, 0)
    tri = jnp.asarray(np.tril(np.ones((tm, tm), np.float32), -1), BF16)
    return pl.pallas_call(
        functools.partial(_router_kernel, n_real=n_real), grid=(n // tm,),
        in_specs=[pl.BlockSpec((tm, d), lambda i: (i, 0)), pl.BlockSpec(nw.shape, c2),
                  pl.BlockSpec(wr.shape, c2), pl.BlockSpec(br.shape, c2), pl.BlockSpec(tri.shape, c2)],
        out_specs=[pl.BlockSpec((tm, d // 2), lambda i: (i, 0)), pl.BlockSpec((tm, LANES), lambda i: (i, 0)),
                   pl.BlockSpec((SUBLANES, LANES), c2)],
        out_shape=[jax.ShapeDtypeStruct((n, d // 2), jnp.int32), jax.ShapeDtypeStruct((n, LANES), F32),
                   jax.ShapeDtypeStruct((SUBLANES, LANES), F32)],
        scratch_shapes=[pltpu.VMEM((SUBLANES, LANES), F32)],
        compiler_params=_cparams(("arbitrary",)), name="moe_router",
    )(h, nw, wr, br, tri)


def _moe_group_kernel(sp_ref, x_ref, w1_ref, b1_ref, w2_ref, b2_ref, y_ref, w1_s, w2_s):
    t = pl.program_id(0)
    n_active = sp_ref[pl.num_programs(0)]
    new_expert = (t == 0) | (sp_ref[t] != sp_ref[jnp.maximum(t - 1, 0)])

    @pl.when(new_expert & (t < n_active))
    def _():
        w1_s[...] = w1_ref[0].astype(BF16)
        w2_s[...] = w2_ref[0].astype(BF16)

    @pl.when(t < n_active)
    def _():
        d_ff = w2_s.shape[0]
        x_a, x_b = _unpack_bf16_pair(x_ref[...])
        half = x_a.shape[1]
        y1 = (jnp.dot(x_a, w1_s[:half, :], preferred_element_type=F32)
              + jnp.dot(x_b, w1_s[half:, :], preferred_element_type=F32)) + b1_ref[0]
        gt = jnp.minimum(y1[:, :d_ff], SWIGLU_LIMIT)
        up = jnp.clip(y1[:, d_ff:], -SWIGLU_LIMIT, SWIGLU_LIMIT)
        act = (up + 1.0) * gt * _sigmoid(SWIGLU_ALPHA * gt)
        y_ref[...] = _pack_bf16_pair(_bdot(act, w2_s[...]) + b2_ref[0])

    @pl.when(t >= n_active)
    def _():
        y_ref[...] = jnp.zeros_like(y_ref)


def _moe_group(sp, x_sorted, w1, b1, w2, b2, tm):
    p, dx = x_sorted.shape
    d = w2.shape[2]
    rows = lambda t, sp: (t, 0)
    per_e = lambda t, sp: (sp[t], 0, 0)
    grid_spec = pltpu.PrefetchScalarGridSpec(
        num_scalar_prefetch=1, grid=(p // tm,),
        in_specs=[pl.BlockSpec((tm, dx), rows),
                  pl.BlockSpec((1,) + w1.shape[1:], per_e), pl.BlockSpec((1,) + b1.shape[1:], per_e),
                  pl.BlockSpec((1,) + w2.shape[1:], per_e), pl.BlockSpec((1,) + b2.shape[1:], per_e)],
        out_specs=pl.BlockSpec((tm, d // 2), rows),
        scratch_shapes=[pltpu.VMEM(w1.shape[1:], BF16), pltpu.VMEM(w2.shape[1:], BF16)])
    return pl.pallas_call(
        _moe_group_kernel, grid_spec=grid_spec, out_shape=jax.ShapeDtypeStruct((p, d // 2), jnp.int32),
        compiler_params=_cparams(("arbitrary",)), name="moe_experts",
    )(sp, x_sorted, w1, b1, w2, b2)


def _moe_combine_kernel(h_ref, yk_ref, meta_ref, o_ref):
    d = h_ref.shape[1]
    meta = meta_ref[...]
    kk = lax.broadcasted_iota(jnp.int32, (LANES, LANES), 0)
    acc_a = h_ref[:, :d // 2]
    acc_b = h_ref[:, d // 2:]
    for k in range(TOP_K):
        pick = jnp.where(kk == META_GATE + k, 1.0, 0.0).astype(BF16)
        g = _dot3(meta, pick)
        g = jnp.concatenate([g] * (d // 2 // LANES), axis=1)
        y_a, y_b = _unpack_bf16_pair(yk_ref[k])
        acc_a = acc_a + g * y_a.astype(F32)
        acc_b = acc_b + g * y_b.astype(F32)
    o_ref[:, :d // 2] = acc_a
    o_ref[:, d // 2:] = acc_b


def _moe_combine(h, yk, meta, tm):
    n, d = h.shape
    rows = lambda i: (i, 0)
    return pl.pallas_call(
        _moe_combine_kernel, grid=(n // tm,),
        in_specs=[pl.BlockSpec((tm, d), rows), pl.BlockSpec((TOP_K, tm, d // 2), lambda i: (0, i, 0)),
                  pl.BlockSpec((tm, LANES), rows)],
        out_specs=pl.BlockSpec((tm, d), rows), out_shape=jax.ShapeDtypeStruct((n, d), F32),
        compiler_params=_cparams(("parallel",)), name="moe_combine",
    )(h, yk, meta)


def _mamba_prep_kernel(xbc_ref, sconv_ref, dt_ref, cw_ref, cb_ref, hp_ref, e_ref,
                       conv_ref, xs_ref, bm_ref, ct_ref, dtx_ref, dec_ref):
    cdim = xbc_ref.shape[1]
    d_inner = xs_ref.shape[1]
    xb = xbc_ref[...]
    u = cb_ref[...] + cw_ref[D_CONV - 1:D_CONV, :] * xb
    for k in range(D_CONV - 1):
        u = u + cw_ref[k:k + 1, :] * sconv_ref[:, k * cdim:(k + 1) * cdim]
    conv_ref[:, :(D_CONV - 2) * cdim] = sconv_ref[:, cdim:]
    conv_ref[:, (D_CONV - 2) * cdim:] = xb
    u = u * _sigmoid(u)
    xs = u[:, :d_inner]
    xs_ref[...] = xs
    bm_ref[...] = u[:, d_inner:d_inner + N_SSM_GROUPS * D_STATE]
    cm = u[:, d_inner + N_SSM_GROUPS * D_STATE:]
    for g in range(N_SSM_GROUPS):
        ct_ref[g] = cm[:, g * D_STATE:(g + 1) * D_STATE].T
    dtv = _softplus(dt_ref[...] + hp_ref[0:1, :])
    dec_ref[...] = jnp.exp(dtv * (-jnp.exp(hp_ref[1:2, :])))
    dtx_ref[...] = (xs * _dot3(dtv, e_ref[...])).T


def _mamba_prep(xbc, sconv, dt, conv_w, conv_b, headp, expand):
    n, cdim = xbc.shape
    d_inner = expand.shape[1]
    gn = N_SSM_GROUPS * D_STATE
    out_shape = [jax.ShapeDtypeStruct(sconv.shape, F32), jax.ShapeDtypeStruct((n, d_inner), F32),
                 jax.ShapeDtypeStruct((n, gn), F32), jax.ShapeDtypeStruct((N_SSM_GROUPS, D_STATE, n), F32),
                 jax.ShapeDtypeStruct((d_inner, n), F32), jax.ShapeDtypeStruct((n, LANES), F32)]
    return pl.pallas_call(_mamba_prep_kernel, out_shape=out_shape,
                          compiler_params=pltpu.CompilerParams(vmem_limit_bytes=VMEM_LIMIT), name="mamba_prep",
                          )(xbc, sconv, dt, conv_w, conv_b, headp, expand)


def _bf16x3(a, b):
    a_hi = a.astype(BF16)
    a_lo = (a - a_hi.astype(F32)).astype(BF16)
    b_hi = b.astype(BF16)
    b_lo = (b - b_hi.astype(F32)).astype(BF16)
    d = lambda x, y: jnp.dot(x, y, preferred_element_type=F32)
    return d(a_hi, b_hi) + (d(a_hi, b_lo) + d(a_lo, b_hi))


def _mamba_state_kernel(dec_ref, h0_ref, dtx_ref, bm_ref, ct_ref, hn_ref, yt_ref):
    s = pl.program_id(0)
    n = bm_ref.shape[0]
    d_inner = dtx_ref.shape[0]
    gw = d_inner // N_SSM_GROUPS
    hpg = gw // SSM_HEAD_DIM

    @pl.when(s == 0)
    def _():
        yt_ref[...] = jnp.zeros_like(yt_ref)

    row_is_s = lax.broadcasted_iota(jnp.int32, (n, D_STATE), 0) == s
    col_is_s = lax.broadcasted_iota(jnp.int32, (D_STATE, n), 1) == s
    groups = range(N_SSM_GROUPS)
    sts = []
    for g in groups:
        b_s = jnp.where(row_is_s, bm_ref[:, g * D_STATE:(g + 1) * D_STATE], 0.0)
        sts.append(_bf16x3(dtx_ref[g * gw:(g + 1) * gw, :], b_s))
    hns = []
    for g in groups:
        parts = []
        for r in range(hpg):
            h = g * hpg + r
            rows = slice(h * SSM_HEAD_DIM, (h + 1) * SSM_HEAD_DIM)
            parts.append(h0_ref[0, rows, :] * dec_ref[s, h] + sts[g][r * SSM_HEAD_DIM:(r + 1) * SSM_HEAD_DIM, :])
        hns.append(jnp.concatenate(parts, axis=0))
        hn_ref[0, g * gw:(g + 1) * gw, :] = hns[g]
    for g in groups:
        c_s = jnp.where(col_is_s, ct_ref[g], 0.0)
        yt_ref[g * gw:(g + 1) * gw, :] += _bdot(hns[g], c_s)


def _mamba_state(dec, h0, dtx_t, bm, ct):
    n, rows, ns = h0.shape
    c2 = lambda s: (0, 0)
    return pl.pallas_call(
        _mamba_state_kernel, grid=(n,),
        in_specs=[pl.BlockSpec(memory_space=pltpu.SMEM),
                  pl.BlockSpec((1, rows, ns), lambda s: (s, 0, 0)), pl.BlockSpec(dtx_t.shape, c2),
                  pl.BlockSpec(bm.shape, c2), pl.BlockSpec(ct.shape, lambda s: (0, 0, 0))],
        out_specs=[pl.BlockSpec((1, rows, ns), lambda s: (s, 0, 0)), pl.BlockSpec(dtx_t.shape, c2)],
        out_shape=[jax.ShapeDtypeStruct(h0.shape, F32), jax.ShapeDtypeStruct(dtx_t.shape, F32)],
        compiler_params=_cparams(("arbitrary",)), name="mamba_state",
    )(dec, h0, dtx_t, bm, ct)


def _page_cmp_kernel(pt_ref, cache_ref, new_ref, w_ref, pec_ref, deint_ref, ones_ref, nw_ref, out_ref, buf, rows_s, sem):
    t = pl.program_id(0)
    nsteps = pl.num_programs(0)
    n_pages = pt_ref.shape[1]
    page = buf.shape[4]
    nstr = out_ref.shape[2]
    half = out_ref.shape[3]
    nch = half // LANES
    spp = page // CMP_STRIDE
    past_str = n_pages * spp

    def page_copy(step, slot, p):
        seq, kv = step // 2, step % 2
        return pltpu.make_async_copy(cache_ref.at[pt_ref[seq, p], pl.ds(nch * kv, nch)], buf.at[slot, p], sem.at[slot])

    @pl.when(t == 0)
    def _():
        rows_s[:, past_str:, :] = jnp.zeros((CMP_STRIDE, nstr - past_str, half), F32)
        for p in range(n_pages):
            page_copy(t, 0, p).start()

    slot = t % 2

    @pl.when(t + 1 < nsteps)
    def _():
        for p in range(n_pages):
            page_copy(t + 1, 1 - slot, p).start()

    for p in range(n_pages):
        page_copy(t, slot, p).wait()

    kv = t % 2
    deint = deint_ref[...]

    pages_per_trip = math.gcd(n_pages, 8)

    def to_rows(trip, carry):
        for pp in range(pages_per_trip):
            p = trip * pages_per_trip + pp
            xr = _bdot_nt(deint, buf[slot, p].reshape(nch * LANES, page))
            for i in range(CMP_STRIDE):
                rows_s[i, pl.ds(pl.multiple_of(p * spp, spp), spp), :] = xr[i * spp:(i + 1) * spp, :]
        return carry

    lax.fori_loop(0, n_pages // pages_per_trip, to_rows, 0)
    new = new_ref[0]
    rows_s[0, past_str:past_str + 1, :] = jnp.where(kv == 0, new[:, :half], new[:, half:])

    xcat = jnp.concatenate([rows_s[i].astype(BF16) for i in range(CMP_STRIDE)], axis=1)
    w_lo = w_ref[0, :CMP_STRIDE].reshape(CMP_STRIDE * half, half)
    w_hi = w_ref[0, CMP_STRIDE:].reshape(CMP_STRIDE * half, half)
    lo = jnp.dot(xcat, w_lo, preferred_element_type=F32) + pec_ref[0, 0:1, :]
    hi = jnp.dot(xcat, w_hi, preferred_element_type=F32) + pec_ref[0, 1:2, :]
    tok = lo + pltpu.roll(hi, nstr - 1, axis=0)
    out_ref[0, 0] = jnp.where(kv == 0, _head_rms(tok, ones_ref[...], nw_ref[...], NSA_HEAD_DIM), tok)


def _cmp_const_kernel(pe_ref, w_ref, out_ref):
    for kv in range(2):
        lo = hi = None
        for i in range(CMP_STRIDE):
            j = CMP_STRIDE + i
            a = _dot3(_rows8(pe_ref[kv, i:i + 1, :]), w_ref[kv, i])
            b = _dot3(_rows8(pe_ref[kv, j:j + 1, :]), w_ref[kv, j])
            lo = a if lo is None else lo + a
            hi = b if hi is None else hi + b
        out_ref[kv] = jnp.concatenate([lo[0:1, :], hi[0:1, :], jnp.zeros((SUBLANES - 2, lo.shape[1]), F32)], axis=0)


def _cmp_const(pe_kv, w_kv):
    return pl.pallas_call(_cmp_const_kernel, out_shape=jax.ShapeDtypeStruct((2, SUBLANES, pe_kv.shape[2]), F32),
                          compiler_params=pltpu.CompilerParams(vmem_limit_bytes=VMEM_LIMIT), name="cmp_const",
                          )(pe_kv, w_kv)


def _page_cmp(page_table, cache_t, new_rows, w_kv, pe_const, ones64, nw, nstr):
    nseq, n_pages = page_table.shape
    page = cache_t.shape[3]
    half = w_kv.shape[2]
    nch = half // LANES
    spp = page // CMP_STRIDE
    deint = np.zeros((page, page), np.float32)
    for i in range(CMP_STRIDE):
        for k in range(spp):
            deint[i * spp + k, CMP_STRIDE * k + i] = 1.0
    deint = jnp.asarray(deint, BF16)
    c2 = lambda t, pt: (0, 0)
    grid_spec = pltpu.PrefetchScalarGridSpec(
        num_scalar_prefetch=1, grid=(2 * nseq,),
        in_specs=[pl.BlockSpec(memory_space=pl.ANY),
                  pl.BlockSpec((1, 1, 2 * half), lambda t, pt: (t // 2, 0, 0)),
                  pl.BlockSpec((1,) + w_kv.shape[1:], lambda t, pt: (t % 2, 0, 0, 0)),
                  pl.BlockSpec((1,) + pe_const.shape[1:], lambda t, pt: (t % 2, 0, 0)),
                  pl.BlockSpec(deint.shape, c2), pl.BlockSpec(ones64.shape, c2), pl.BlockSpec(nw.shape, c2)],
        out_specs=pl.BlockSpec((1, 1, nstr, half), lambda t, pt: (t // 2, t % 2, 0, 0)),
        scratch_shapes=[pltpu.VMEM((2, n_pages, nch, LANES, page), F32),
                        pltpu.VMEM((CMP_STRIDE, nstr, half), F32),
                        pltpu.SemaphoreType.DMA((2,))])
    return pl.pallas_call(
        _page_cmp_kernel, grid_spec=grid_spec, out_shape=jax.ShapeDtypeStruct((nseq, 2, nstr, half), F32),
        compiler_params=_cparams(("arbitrary",)), name="page_cmp",
    )(page_table, cache_t, new_rows, w_kv, pe_const, deint, ones64, nw)


def _rows8(x):
    return jnp.broadcast_to(x, (SUBLANES, x.shape[1]))


def _group_q(q_row, g):
    parts = [q_row[:, (g * Q_PER_KV + r) * NSA_HEAD_DIM:(g * Q_PER_KV + r + 1) * NSA_HEAD_DIM]
             for r in range(Q_PER_KV)]
    parts.append(jnp.zeros((SUBLANES - Q_PER_KV, NSA_HEAD_DIM), F32))
    return jnp.concatenate(parts, axis=0)


def _heads_to_row(o):
    return jnp.concatenate([o[r:r + 1, :] for r in range(Q_PER_KV)], axis=1)


def _softmax_rows(s, keep):
    s = jnp.where(keep, s, NEG)
    m = jnp.max(s, axis=-1, keepdims=True)
    p = jnp.where(keep, jnp.exp2(s - m), 0.0)
    l = jnp.sum(p, axis=-1, keepdims=True)
    return p * jnp.where(l > 0.0, 1.0 / l, 0.0)


def _sample_cw_kernel(q_ref, brg_ref, kvc_ref, win_ref, wnew_ref, tbc_ref, tbw_ref, gexp_ref,
                      o_ref, sel_ref, wout_ref, wall_s, *, tq, past_w):
    nstr = kvc_ref.shape[2]
    half = kvc_ref.shape[3]
    wlen = win_ref.shape[1]
    nsel_pad = sel_ref.shape[2]
    wrows = wall_s.shape[0]
    q_row = q_ref[0]
    wall_s[0:wlen, :] = win_ref[0]
    wall_s[wlen:wlen + 1, :] = wnew_ref[0]
    wall_s[wlen + 1:, :] = jnp.zeros((wrows - wlen - 1, wall_s.shape[1]), F32)
    wout_ref[0] = wall_s[1:wlen + 1, :]

    nn = lax.broadcasted_iota(jnp.int32, (SUBLANES, nstr), 1)
    keep_c = (CMP_STRIDE * nn + (CMP_BLOCK - 1)) <= tq
    wi = lax.broadcasted_iota(jnp.int32, (SUBLANES, wrows), 1)
    dw = tq - (past_w + wi)
    keep_w = (dw >= 0) & (dw < WINDOW) & (past_w + wi >= 0) & (wi <= wlen)
    jb = lax.broadcasted_iota(jnp.int32, (nsel_pad, nstr), 0) * SEL_BLOCK
    cs = lax.broadcasted_iota(jnp.int32, (nsel_pad, nstr), 1) * CMP_STRIDE
    cover_t = jnp.where((cs < jb + SEL_BLOCK) & (cs + CMP_BLOCK > jb), 1.0, 0.0).astype(BF16)
    ji = lax.broadcasted_iota(jnp.int32, (nsel_pad, LANES), 0)
    valid = ji * SEL_BLOCK <= tq
    cur = tq // SEL_BLOCK
    forced = valid & ((ji == 0) | (ji == cur) | (ji == cur - 1))
    ii = lax.broadcasted_iota(jnp.int32, (nsel_pad, nsel_pad), 0)
    jj = lax.broadcasted_iota(jnp.int32, (nsel_pad, nsel_pad), 1)

    sel_ref[0] = jnp.zeros(sel_ref.shape[1:], F32)
    groups = range(N_KV_HEADS)
    k_lanes = [slice(g * NSA_HEAD_DIM, (g + 1) * NSA_HEAD_DIM) for g in groups]
    v_lanes = [slice(half + g * NSA_HEAD_DIM, half + (g + 1) * NSA_HEAD_DIM) for g in groups]
    qgs = [_group_q(q_row, g) for g in groups]
    sc = [_bdot_nt(qgs[g], kvc_ref[0, 0, :, k_lanes[g]]) + tbc_ref[g] for g in groups]
    sw = [_bdot_nt(qgs[g], wall_s[:, k_lanes[g]]) + tbw_ref[g] for g in groups]
    pc = [_softmax_rows(s, keep_c) for s in sc]
    pw = [_softmax_rows(s, keep_w) for s in sw]
    oc = [_heads_to_row(_bdot(pc[g], kvc_ref[0, 1, :, k_lanes[g]])) for g in groups]
    ow = [_heads_to_row(_bdot(pw[g], wall_s[:, v_lanes[g]])) for g in groups]
    imps = [_dot3_nt_l(cover_t, jnp.broadcast_to(jnp.sum(pc[g][0:Q_PER_KV, :], axis=0, keepdims=True), (LANES, nstr)))
            for g in groups]
    for g in groups:
        v_col = jnp.where(forced, FORCE_SCORE, jnp.where(valid, imps[g], NEG))
        v_row = jnp.concatenate([v_col[k * LANES:(k + 1) * LANES, :].T for k in range(nsel_pad // LANES)], axis=1)
        a = jnp.broadcast_to(v_row[0:1, :], (nsel_pad, nsel_pad))
        b = jnp.concatenate([v_col] * (nsel_pad // LANES), axis=1)
        beats = jnp.where(ii < jj, jnp.where(b >= a, 1.0, 0.0), jnp.where(b > a, 1.0, 0.0))
        cnt = jnp.sum(beats, axis=0, keepdims=True)
        sel_ref[0, g:g + 1, :] = jnp.where(cnt < N_SEL_BLOCKS, 1.0, 0.0)
    gates = _rows8(brg_ref[0])
    o = (jnp.concatenate(oc, axis=1) * _dot3(gates, gexp_ref[0])[0:1, :]
         + jnp.concatenate(ow, axis=1) * _dot3(gates, gexp_ref[2])[0:1, :])
    o_ref[0] = o


def _sample_cw(q, brg, kvc, win, wnew, tbc, tbw, gexp, tq, past_w, nsel_pad):
    nseq = q.shape[0]
    qw = q.shape[2]
    wlen, ww = win.shape[1], win.shape[2]
    wrows = -(-(wlen + 1) // SUBLANES) * SUBLANES
    per3 = lambda s: (s, 0, 0)
    c3 = lambda s: (0, 0, 0)
    return pl.pallas_call(
        functools.partial(_sample_cw_kernel, tq=tq, past_w=past_w), grid=(nseq,),
        in_specs=[pl.BlockSpec((1, 1, qw), per3), pl.BlockSpec((1, 1, LANES), per3),
                  pl.BlockSpec((1,) + kvc.shape[1:], lambda s: (s, 0, 0, 0)),
                  pl.BlockSpec((1, wlen, ww), per3), pl.BlockSpec((1, 1, ww), per3),
                  pl.BlockSpec(tbc.shape, c3), pl.BlockSpec(tbw.shape, c3), pl.BlockSpec(gexp.shape, c3)],
        out_specs=[pl.BlockSpec((1, 1, qw), per3), pl.BlockSpec((1, SUBLANES, nsel_pad), per3),
                   pl.BlockSpec((1, wlen, ww), per3)],
        out_shape=[jax.ShapeDtypeStruct((nseq, 1, qw), F32), jax.ShapeDtypeStruct((nseq, SUBLANES, nsel_pad), F32),
                   jax.ShapeDtypeStruct((nseq, wlen, ww), F32)],
        scratch_shapes=[pltpu.VMEM((wrows, ww), F32)],
        compiler_params=_cparams(("parallel",)), name="sample_cmp_win",
    )(q, brg, kvc, win, wnew, tbc, tbw, gexp)


def _sample_sel_kernel(pg_ref, c31_ref, f0_ref, cache_ref, q_ref, brg_ref, snew_ref, code_ref, ocw_ref, tbl_ref, gexp_ref,
                       o_ref, buf, sem):
    s = pl.program_id(0)
    nseq = pl.num_programs(0)
    nblk = pg_ref.shape[1] // N_KV_HEADS
    page = cache_ref.shape[4]
    half = snew_ref.shape[2] // 2

    def copies(seq, slot):
        out = []
        for g in range(N_KV_HEADS):
            for k in range(nblk):
                out.append(pltpu.make_async_copy(cache_ref.at[pg_ref[seq, g * nblk + k], :, g],
                                                 buf.at[slot, :, g, :, pl.ds(k * page, page)], sem.at[slot]))
        return out

    @pl.when(s == 0)
    def _():
        for cp in copies(s, 0):
            cp.start()

    slot = s % 2

    @pl.when(s + 1 < nseq)
    def _():
        for cp in copies(s + 1, 1 - slot):
            cp.start()

    for cp in copies(s, slot):
        cp.wait()

    q_row = q_ref[0]
    new = snew_ref[0]
    groups = range(N_KV_HEADS)
    qgs = [_group_q(q_row, g) for g in groups]
    raw = [jnp.dot(qgs[g].astype(BF16), buf[slot, 0, g].astype(BF16), preferred_element_type=F32) for g in groups]
    stats = []
    for g in groups:
        heads = [g * Q_PER_KV + r for r in range(Q_PER_KV)]
        per_head = lambda ref: jnp.concatenate([jnp.full((1, 1), ref[h], F32) for h in heads]
                                               + [jnp.zeros((SUBLANES - Q_PER_KV, 1), F32)], axis=0)
        code = _rows8(code_ref[0, g:g + 1, :])
        near = jnp.concatenate([tbl_ref[g]] * nblk, axis=1)
        sc = jnp.where(code > 0.5, raw[g] + jnp.where(code > 1.5, near, per_head(c31_ref)), NEG)
        k_new = new[:, g * NSA_HEAD_DIM:(g + 1) * NSA_HEAD_DIM]
        s_new = jnp.sum(qgs[g] * _rows8(k_new), axis=-1, keepdims=True) + per_head(f0_ref)
        new_on = _rows8(code_ref[0, N_KV_HEADS + g:N_KV_HEADS + g + 1, 0:1]) > 0.5
        s_new = jnp.where(new_on, s_new, NEG)
        m = jnp.maximum(jnp.max(sc, axis=-1, keepdims=True), s_new)
        p = jnp.where(code > 0.5, jnp.exp2(sc - m), 0.0)
        p_new = jnp.where(new_on, jnp.exp2(s_new - m), 0.0)
        l = jnp.sum(p, axis=-1, keepdims=True) + p_new
        stats.append((p, p_new, jnp.where(l > 0.0, 1.0 / l, 0.0)))
    outs = []
    for g in groups:
        p, p_new, inv = stats[g]
        v_new = new[:, half + g * NSA_HEAD_DIM:half + (g + 1) * NSA_HEAD_DIM]
        o = _bdot_nt(p, buf[slot, 1, g]) + p_new * _rows8(v_new)
        outs.append(_heads_to_row(o * inv))
    gates = _dot3(_rows8(brg_ref[0]), gexp_ref[1])[0:1, :]
    o_ref[0] = ocw_ref[0] + jnp.concatenate(outs, axis=1) * gates


def _sample_sel(pages, c31, f0, cache_t, q, brg, snew, code, ocw, tbl, gexp):
    nseq, qw = q.shape[0], q.shape[2]
    nblk = pages.shape[1] // N_KV_HEADS
    page = cache_t.shape[4]
    per3 = lambda s, *_: (s, 0, 0)
    c3 = lambda s, *_: (0, 0, 0)
    grid_spec = pltpu.PrefetchScalarGridSpec(
        num_scalar_prefetch=1, grid=(nseq,),
        in_specs=[pl.BlockSpec(memory_space=pltpu.SMEM), pl.BlockSpec(memory_space=pltpu.SMEM),
                  pl.BlockSpec(memory_space=pl.ANY),
                  pl.BlockSpec((1, 1, qw), per3), pl.BlockSpec((1, 1, LANES), per3),
                  pl.BlockSpec((1, 1, snew.shape[2]), per3), pl.BlockSpec((1,) + code.shape[1:], per3),
                  pl.BlockSpec((1, 1, qw), per3), pl.BlockSpec(tbl.shape, c3), pl.BlockSpec(gexp.shape, c3)],
        out_specs=pl.BlockSpec((1, 1, qw), per3),
        scratch_shapes=[pltpu.VMEM((2, 2, N_KV_HEADS, NSA_HEAD_DIM, nblk * page), F32),
                        pltpu.SemaphoreType.DMA((2,))])
    return pl.pallas_call(
        _sample_sel_kernel, grid_spec=grid_spec, out_shape=jax.ShapeDtypeStruct((nseq, 1, qw), F32),
        compiler_params=_cparams(("arbitrary",)), name="sample_sel",
    )(pages, c31, f0, cache_t, q, brg, snew, code, ocw, tbl, gexp)


def _sample_merge_kernel(yt_ref, xs_ref, z_ref, sgs_ref, dsk_ref, nw_ref, onesg_ref, ws_ref, ynsa_ref, sgn_ref, x_ref,
                         wn_ref, wo_ref, h_ref):
    gw = onesg_ref.shape[0]
    y = yt_ref[...].T + dsk_ref[...] * xs_ref[...]
    zz = z_ref[...]
    y = y * (zz * _sigmoid(zz))
    ms = _seg_sum(y * y, onesg_ref[...]) * (1.0 / gw)
    y = (y * lax.rsqrt(ms + RMS_EPS)) * nw_ref[...]
    u = sgs_ref[...] * _bdot(y, ws_ref[...]) + sgn_ref[...] * _bdot(ynsa_ref[...], wn_ref[...])
    h_ref[...] = x_ref[...] + _bdot(u, wo_ref[...])


def _sample_merge(y_t, xs, z, sgs, dsk, nw, onesg, ws, ynsa, sgn, x, wn, wo):
    return pl.pallas_call(_sample_merge_kernel, out_shape=jax.ShapeDtypeStruct(x.shape, F32),
                          compiler_params=pltpu.CompilerParams(vmem_limit_bytes=VMEM_LIMIT), name="sample_merge",
                          )(y_t, xs, z, sgs, dsk, nw, onesg, ws, ynsa, sgn, x, wn, wo)


def _bucket_lut():
    n = np.arange(MAX_DISTANCE + 1)
    max_exact = N_BUCKETS // 2
    nf = np.maximum(n, 1).astype(np.float32)
    large = max_exact + (np.log(nf / max_exact) / math.log(MAX_DISTANCE / max_exact)
                         * (N_BUCKETS - max_exact)).astype(np.int32)
    return np.where(n < max_exact, n, np.minimum(large, N_BUCKETS - 1))


def _bias_of_dist(rel_bias, dist):
    lut = _bucket_lut()
    idx = lut[np.clip(dist, 0, MAX_DISTANCE)]
    onehot = np.eye(N_BUCKETS, dtype=np.float32)[idx] * (np.asarray(dist) >= 0)[..., None]
    b = jnp.einsum("...b,bh->h...", jnp.asarray(onehot), rel_bias.astype(F32), precision=lax.Precision.HIGHEST)
    return b * LOG2E


def _block_diag(w, reps):
    n, d, e = w.shape
    eye = jnp.eye(reps, dtype=w.dtype)
    return jnp.einsum("ab,nde->nadbe", eye, w).reshape(n, reps * d, reps * e)


def _ones_blocks(size, seg):
    return jnp.asarray(np.kron(np.eye(size // seg), np.ones((seg, seg))), BF16)


def _pad_cols(w, width):
    return jnp.pad(w, ((0, 0), (0, width - w.shape[1])))


def _prep(p):
    d_model = p["w_in"].shape[1]
    d_inner = p["w_ssm_out"].shape[1]
    n_heads = d_inner // SSM_HEAD_DIM
    conv_dim = p["conv_w"].shape[2]
    qw = N_Q_HEADS * NSA_HEAD_DIM
    kvw = 2 * N_KV_HEADS * NSA_HEAD_DIM
    splits = (d_inner, conv_dim, n_heads, qw, kvw, kvw, kvw, 3 * N_Q_HEADS, d_model, d_model)
    offs = np.concatenate([[0], np.cumsum(splits)])
    w_in = p["w_in"][0]
    seg = lambda k: w_in[:, offs[k]:offs[k + 1]]
    bf = lambda a: a.astype(BF16)
    o = {}
    o["w_ssm_in"] = [bf(seg(0)), bf(seg(1)), bf(_pad_cols(seg(2), LANES))]
    o["w_nsa_in"] = [bf(seg(3)), bf(seg(4)), bf(seg(5)), bf(seg(6)), bf(_pad_cols(seg(7), LANES)), bf(seg(8)),
                     bf(seg(9))]
    o["norm_mix"] = p["norm_mix_w"][0][None, :]
    kv_half = kvw // 2
    head_w = jnp.zeros((SUBLANES, qw), F32)
    head_w = head_w.at[0].set(jnp.tile(p["q_norm_w"][0], N_Q_HEADS))
    head_w = head_w.at[1, :kv_half].set(jnp.tile(p["k_sel_norm_w"][0], N_KV_HEADS))
    head_w = head_w.at[2, :kv_half].set(jnp.tile(p["k_win_norm_w"][0], N_KV_HEADS))
    o["head_w"] = head_w
    o["ones64"] = _ones_blocks(kv_half, NSA_HEAD_DIM)
    o["conv_w"] = p["conv_w"][0]
    o["conv_b"] = p["conv_b"][0][None, :]
    headp = jnp.zeros((SUBLANES, LANES), F32)
    headp = headp.at[0, :n_heads].set(p["dt_bias"][0]).at[1, :n_heads].set(p["a_log"][0])
    o["headp"] = headp
    o["dsk"] = jnp.repeat(p["d_skip"][0], SSM_HEAD_DIM)[None, :]
    o["ssm_nw"] = p["ssm_norm_w"][0][None, :]
    expand = np.zeros((LANES, d_inner), np.float32)
    for h in range(n_heads):
        expand[h, h * SSM_HEAD_DIM:(h + 1) * SSM_HEAD_DIM] = 1.0
    o["expand"] = jnp.asarray(expand, BF16)
    o["tri"] = jnp.asarray(np.tril(np.ones((SSD_CHUNK, SSD_CHUNK), np.float32)), BF16)
    o["onesg"] = jnp.ones((d_inner // N_SSM_GROUPS,) * 2, BF16)
    o["w_ssm_out"] = bf(p["w_ssm_out"][0])
    o["cmp_wk"] = bf(_block_diag(p["cmp_w_k"][0], N_KV_HEADS))
    o["cmp_wv"] = bf(_block_diag(p["cmp_w_v"][0], N_KV_HEADS))
    o["cmp_pek"] = jnp.tile(p["cmp_pe_k"][0], (1, N_KV_HEADS))
    o["cmp_pev"] = jnp.tile(p["cmp_pe_v"][0], (1, N_KV_HEADS))
    o["kc_nw"] = jnp.tile(p["k_cmp_norm_w"][0], N_KV_HEADS)[None, :]
    rel = p["rel_bias"]
    band = 2 * CMP_STRIDE
    i = np.arange(QT)[:, None]
    c = np.arange(band)[None, :]
    d_band = i + CMP_STRIDE * (band // 2) - CMP_STRIDE * c - (CMP_BLOCK - 1)
    d_band = np.concatenate([d_band, np.full((QT, 1), MAX_DISTANCE)], axis=1)
    tb = _bias_of_dist(rel, d_band)
    o["cmp_tb"] = jnp.pad(tb, ((0, 0), (0, 0), (0, 2 * band - tb.shape[2])))
    jk = np.arange(QT)[:, None]
    iq = np.arange(QT)[None, :]
    o["b0"] = _bias_of_dist(rel, iq - jk)
    o["b1"] = _bias_of_dist(rel, QT + iq - jk)
    o["c31"] = rel[N_BUCKETS - 1].astype(F32) * LOG2E
    gexp = np.zeros((3, LANES, qw), np.float32)
    for h in range(N_Q_HEADS):
        for k in range(3):
            gexp[k, 3 * h + k, h * NSA_HEAD_DIM:(h + 1) * NSA_HEAD_DIM] = 1.0
    o["gexp"] = jnp.asarray(gexp, BF16)
    o["w_nsa_out"] = bf(p["w_nsa_out"][0])
    o["w_out"] = bf(p["w_out"][0])
    o["norm_ffn"] = p["norm_ffn_w"][0][None, :]
    ne = p["w_router"].shape[2]
    o["w_router"] = _pad_cols(p["w_router"][0], LANES)
    o["b_router"] = jnp.full((1, LANES), NEG, F32).at[0, :ne].set(p["b_router"][0])
    o["w1"] = p["w_gate_up"][0]
    o["b1e"] = p["b_gate_up"][0][:, None, :]
    o["w2"] = p["w_down"][0]
    o["b2e"] = p["b_down"][0][:, None, :]
    return o


def _kv_layouts(kv, batch):
    n = kv.shape[0]
    s = n // batch
    half = kv.shape[1] // 2
    k = kv[:, :half].astype(BF16).reshape(batch, s, N_KV_HEADS, NSA_HEAD_DIM).transpose(0, 2, 1, 3)
    v = kv[:, half:].astype(BF16).reshape(batch, s // QT, QT, N_KV_HEADS, NSA_HEAD_DIM).transpose(0, 3, 1, 4, 2)
    extra = jnp.zeros(v.shape[:3] + (SUBLANES, QT), BF16).at[:, :, :, 0, :].set(1.0)
    return k, jnp.concatenate([v, extra], axis=3)


def _prompt_mixer(x, o, batch):
    z, xbc, dt = _proj(x, o["norm_mix"], o["ones64"], o["head_w"], o["w_ssm_in"], ["raw", "raw", "raw"], 256)
    q, kvc, kvs, kvw, brg, sg_ssm, sg_nsa = _proj(
        x, o["norm_mix"], o["ones64"], o["head_w"], o["w_nsa_in"], ["q", "raw", "ks", "kw", "sig", "sig", "sig"], 256)
    m_ssm, h_t = _ssd_prompt(xbc, z, dt, sg_ssm, o["conv_w"], o["conv_b"], o["headp"], o["dsk"], o["ssm_nw"],
                             o["expand"], o["tri"], o["w_ssm_out"], batch)
    kc, vc = _cmp_build(kvc, o["cmp_wk"], o["cmp_wv"], o["cmp_pek"], o["cmp_pev"], o["ones64"], o["kc_nw"], batch)
    ocmp, sel = _cmp_attn(q, brg, kc, vc, o["cmp_tb"], o["gexp"], batch)
    ks, vs_t = _kv_layouts(kvs, batch)
    kw, vw_t = _kv_layouts(kvw, batch)
    h = _nsa_main(o["c31"], q, brg, sel, ocmp, ks, vs_t, kw, vw_t, o["b0"], o["b1"], m_ssm, sg_nsa, x,
                  o["w_nsa_out"], o["w_out"], batch)
    return h, (kvc, kvs, kvw, h_t, xbc)


MOE_TOKEN_TILE = 384
MOE_GROUP_TILE = 512


def _moe(h_all, o):
    n, d = h_all.shape
    tm, tg = MOE_TOKEN_TILE, MOE_GROUP_TILE
    ne = o["w1"].shape[0]
    n_pad = -(-n // tm) * tm
    hp = jnp.pad(h_all, ((0, n_pad - n), (0, 0))) if n_pad != n else h_all
    hn, meta, cnt = _router(hp, o["norm_ffn"], o["w_router"], o["b_router"], tm, n)
    eid = meta[:n, META_EXPERT:META_EXPERT + TOP_K].astype(jnp.int32)
    rank = meta[:n, META_RANK:META_RANK + TOP_K].astype(jnp.int32)
    count = cnt[0, :ne].astype(jnp.int32)
    tiles = (count + tg - 1) // tg
    tile_end = jnp.cumsum(tiles)
    first_row = (tile_end - tiles) * tg
    pos = jnp.sum(jnp.where(eid[..., None] == jnp.arange(ne), first_row, 0), axis=-1) + rank
    n_tiles = -(-(n * TOP_K) // tg) + ne
    tile_expert = jnp.minimum(jnp.sum(tile_end[None, :] <= jnp.arange(n_tiles)[:, None], axis=1), ne - 1)
    sp = jnp.concatenate([tile_expert, tile_end[-1:]]).astype(jnp.int32)
    src = jnp.zeros((n_tiles * tg,), jnp.int32).at[pos.reshape(-1)].set(
        jnp.arange(n * TOP_K, dtype=jnp.int32) // TOP_K, unique_indices=True, mode="promise_in_bounds")
    y_sorted = _moe_group(sp, hn[src], o["w1"], o["b1e"], o["w2"], o["b2e"], tg)
    yk = y_sorted[pos.T.reshape(-1)].reshape(TOP_K, n, d // 2)
    if n_pad != n:
        yk = jnp.pad(yk, ((0, 0), (0, n_pad - n), (0, 0)))
    return _moe_combine(hp, yk, meta, tm)[:n]


def kernel(x_prompt, x_sample, cache_cmp, cache_sel, cache_win, state_ssm, state_conv, page_table, norm_mix_w, w_in,
           conv_w, conv_b, dt_bias, a_log, d_skip, ssm_norm_w, w_ssm_out, q_norm_w, k_cmp_norm_w, k_sel_norm_w,
           k_win_norm_w, cmp_pe_k, cmp_w_k, cmp_pe_v, cmp_w_v, rel_bias, w_nsa_out, w_out, norm_ffn_w, w_router,
           b_router, w_gate_up, b_gate_up, w_down, b_down):
    params = dict(norm_mix_w=norm_mix_w, w_in=w_in, conv_w=conv_w, conv_b=conv_b, dt_bias=dt_bias, a_log=a_log,
                  d_skip=d_skip, ssm_norm_w=ssm_norm_w, w_ssm_out=w_ssm_out, q_norm_w=q_norm_w,
                  k_cmp_norm_w=k_cmp_norm_w, k_sel_norm_w=k_sel_norm_w, k_win_norm_w=k_win_norm_w,
                  cmp_pe_k=cmp_pe_k, cmp_w_k=cmp_w_k, cmp_pe_v=cmp_pe_v, cmp_w_v=cmp_w_v, rel_bias=rel_bias,
                  w_nsa_out=w_nsa_out, w_out=w_out, norm_ffn_w=norm_ffn_w, w_router=w_router, b_router=b_router,
                  w_gate_up=w_gate_up, b_gate_up=b_gate_up, w_down=w_down, b_down=b_down)
    o = _prep(params)
    bsz, s, d = x_prompt.shape
    db, t, _ = x_sample.shape
    kvshape = (2, N_KV_HEADS, NSA_HEAD_DIM)
    hp, (kvc, kvs, kvw, h_t, xbc) = _prompt_mixer(x_prompt.reshape(bsz * s, d), o, bsz)
    assert t == 1, "the sample group decodes one token per sequence"
    hs, (kvc_s, kvs_s, win_s, h_s, conv_s) = _sample_mixer(
        x_sample.reshape(db, d), o, cache_cmp[0], cache_sel[0], cache_win[0], state_ssm[0], state_conv[0],
        page_table, rel_bias)
    y_all = _moe(jnp.concatenate([hp, hs], axis=0), o)
    wlen = min(WINDOW, s)
    n_heads = h_t.shape[2] // SSM_HEAD_DIM
    outs_p = (kvc.reshape((1, bsz, s) + kvshape), kvs.reshape((1, bsz, s) + kvshape),
              kvw.reshape((bsz, s) + kvshape)[None, :, s - wlen:],
              h_t.reshape(bsz, D_STATE, n_heads, SSM_HEAD_DIM).transpose(0, 2, 3, 1)[None],
              xbc.reshape(bsz, s, -1)[None, :, s - (D_CONV - 1):])
    outs_s = (kvc_s.reshape((1, db, t) + kvshape), kvs_s.reshape((1, db, t) + kvshape),
              win_s.reshape((1, db, win_s.shape[1]) + kvshape), h_s.reshape((1, db, n_heads, SSM_HEAD_DIM, D_STATE)),
              conv_s.reshape(1, db, D_CONV - 1, -1))
    return (y_all[:bsz * s].reshape(bsz, s, d), y_all[bsz * s:].reshape(db, t, d)) + outs_p + outs_s


def _sample_mixer(x, o, cache_cmp, cache_sel, cache_win, state_ssm, state_conv, page_table, rel):
    n = x.shape[0]
    z, xbc, dt = _proj(x, o["norm_mix"], o["ones64"], o["head_w"], o["w_ssm_in"], ["raw", "raw", "raw"], n)
    q, kvc, kvs, kvw, brg, sg_ssm, sg_nsa = _proj(
        x, o["norm_mix"], o["ones64"], o["head_w"], o["w_nsa_in"], ["q", "raw", "ks", "kw", "sig", "sig", "sig"], n)
    d_inner = z.shape[1]
    n_heads = d_inner // SSM_HEAD_DIM
    conv_new, xs, bm, ct, dtx_t, dec = _mamba_prep(xbc, state_conv.reshape(n, -1), dt, o["conv_w"], o["conv_b"],
                                                   o["headp"], o["expand"])
    h_new, y_t = _mamba_state(dec[:, :n_heads], state_ssm.reshape(n, d_inner, D_STATE), dtx_t, bm, ct)
    pool, page = cache_cmp.shape[0], cache_cmp.shape[1]
    n_pages = page_table.shape[1]
    past = n_pages * page
    tq = past
    assert page >= MAX_DISTANCE and page % SEL_BLOCK == 0 and past % CMP_STRIDE == 0
    half = kvc.shape[1] // 2
    nch = half // LANES
    cmp_t = jnp.transpose(cache_cmp, (0, 2, 3, 4, 1)).reshape(pool, 2 * nch, LANES, page)
    sel_t = jnp.transpose(cache_sel, (0, 2, 3, 4, 1))
    nstr = -(-(past // CMP_STRIDE + 1) // SUBLANES) * SUBLANES
    w_kv = jnp.stack([o["cmp_wk"], o["cmp_wv"]])
    pe_kv = jnp.stack([o["cmp_pek"], o["cmp_pev"]])
    tok = _page_cmp(page_table, cmp_t, kvc[:, None, :], w_kv, _cmp_const(pe_kv, w_kv), o["ones64"], o["kc_nw"], nstr)
    pad_heads = lambda b: jnp.pad(b.reshape(N_KV_HEADS, Q_PER_KV, -1), ((0, 0), (0, SUBLANES - Q_PER_KV), (0, 0)))
    tbc = pad_heads(_bias_of_dist(rel, tq - (CMP_STRIDE * np.arange(nstr) + CMP_BLOCK - 1)))
    wlen = cache_win.shape[1]
    wrows = -(-(wlen + 1) // SUBLANES) * SUBLANES
    past_w = past - wlen
    tbw = pad_heads(_bias_of_dist(rel, tq - (past_w + np.arange(wrows))))
    n_past_sel = past // SEL_BLOCK
    nsel = n_past_sel + 1
    nsel_pad = -(-nsel // LANES) * LANES
    o_cw, selmask, win_new = _sample_cw(q[:, None, :], brg[:, None, :], tok, cache_win.reshape(n, wlen, -1),
                                        kvw[:, None, :], tbc, tbw, o["gexp"], tq, past_w, nsel_pad)
    nblk = min(N_SEL_BLOCKS, nsel)
    picked = selmask[:, :N_KV_HEADS, :nsel] > 0.5
    order = jnp.cumsum(picked, axis=-1) - 1
    hit = picked[..., None] & (order[..., None] == jnp.arange(nblk))
    idx = jnp.sum(jnp.where(hit, jnp.arange(nsel)[:, None], 0), axis=2)
    is_past = idx < n_past_sel
    jp = jnp.minimum(idx, n_past_sel - 1)
    per_page = page // SEL_BLOCK
    pg = jp // per_page
    phys = jnp.take_along_axis(page_table, pg.reshape(n, -1), axis=1).astype(jnp.int32)
    lane_blk = (np.arange(page) // SEL_BLOCK)[None, None, None, :]
    attended = is_past[..., None] & (lane_blk == (jp % per_page)[..., None])
    code = jnp.where(attended, jnp.where((pg == n_pages - 1)[..., None], 2.0, 1.0), 0.0).reshape(n, N_KV_HEADS, -1)
    new_on = jnp.any(idx >= n_past_sel, axis=-1).astype(F32)
    code = jnp.concatenate([code, jnp.broadcast_to(new_on[..., None], code.shape)], axis=1)
    tbl = pad_heads(_bias_of_dist(rel, tq - ((n_pages - 1) * page + np.arange(page))))
    f0 = rel[_bucket_lut()[0]].astype(F32) * LOG2E
    y_nsa = _sample_sel(phys, o["c31"], f0, sel_t, q[:, None, :], brg[:, None, :], kvs[:, None, :], code, o_cw, tbl,
                        o["gexp"])
    h = _sample_merge(y_t, xs, z, sg_ssm, o["dsk"], o["ssm_nw"], o["onesg"], o["w_ssm_out"], y_nsa[:, 0], sg_nsa, x,
                      o["w_nsa_out"], o["w_out"])
    return h, (kvc, kvs, win_new, h_new, conv_new)
```

```python
import functools
import math

import jax
import jax.numpy as jnp
import numpy as np
from jax import lax
from jax.experimental import pallas as pl
from jax.experimental.pallas import tpu as pltpu

F32 = jnp.float32
BF16 = jnp.bfloat16

SSM_HEAD_DIM = 64
N_SSM_GROUPS = 4
D_STATE = 128
D_CONV = 4
SSD_CHUNK = 128
NSA_HEAD_DIM = 64
N_Q_HEADS = 16
N_KV_HEADS = 4
Q_PER_KV = N_Q_HEADS // N_KV_HEADS
CMP_BLOCK = 32
CMP_STRIDE = 16
SEL_BLOCK = 64
N_SEL_BLOCKS = 16
WINDOW = 512
N_BUCKETS = 32
MAX_DISTANCE = 128
TOP_K = 4
SWIGLU_LIMIT = 7.0
SWIGLU_ALPHA = 1.702
RMS_EPS = 1e-6
LOG2E = math.log2(math.e)
NEG = -1e30
FORCE_SCORE = 1e9

LANES = 128
SUBLANES = 8
QT = 128
FAR_TILES_PER_TRIP = 4
VMEM_LIMIT = 56 * 1024 * 1024


def _cparams(sem):
    return pltpu.CompilerParams(dimension_semantics=sem, vmem_limit_bytes=VMEM_LIMIT)


def _bdot(a, b):
    return jnp.dot(a.astype(BF16), b.astype(BF16), preferred_element_type=F32)


def _bdot_nt(a, b):
    return lax.dot_general(a.astype(BF16), b.astype(BF16), (((1,), (1,)), ((), ())),
                           preferred_element_type=F32)


def _split3(a):
    hi = a.astype(BF16)
    r = a - hi.astype(F32)
    mid = r.astype(BF16)
    lo = (r - mid.astype(F32)).astype(BF16)
    return hi, mid, lo


def _dot3(a, b):
    hi, mid, lo = _split3(a)
    d = lambda p: jnp.dot(p, b, preferred_element_type=F32)
    return (d(hi) + d(mid)) + d(lo)


def _dot2(a, b):
    hi = a.astype(BF16)
    mid = (a - hi.astype(F32)).astype(BF16)
    return jnp.dot(hi, b, preferred_element_type=F32) + jnp.dot(mid, b, preferred_element_type=F32)


def _dot3_l(a, b):
    hi, mid, lo = _split3(b)
    d = lambda p: jnp.dot(a, p, preferred_element_type=F32)
    return (d(hi) + d(mid)) + d(lo)


def _dot3_nt_l(a, b):
    hi, mid, lo = _split3(b)
    d = lambda p: lax.dot_general(a, p, (((1,), (1,)), ((), ())), preferred_element_type=F32)
    return (d(hi) + d(mid)) + d(lo)


def _seg_sum(y, ones_blk):
    c = ones_blk.shape[0]
    outs = []
    for k in range(y.shape[1] // c):
        outs.append(_dot3(y[:, k * c:(k + 1) * c], ones_blk))
    return outs[0] if len(outs) == 1 else jnp.concatenate(outs, axis=1)


def _sigmoid(x):
    return 1.0 / (1.0 + jnp.exp(-x))


def _rms_rows(x, w):
    ms = jnp.mean(x * x, axis=-1, keepdims=True)
    return (x * lax.rsqrt(ms + RMS_EPS)) * w


def _head_rms(y, ones64, w, seg):
    ms = _seg_sum(y * y, ones64) * (1.0 / seg)
    return (y * lax.rsqrt(ms + RMS_EPS)) * w


def _proj_kernel(kinds, x_ref, nw_ref, ones_ref, hw_ref, *refs):
    n = len(kinds)
    w_refs, o_refs = refs[:n], refs[n:]
    xn = _rms_rows(x_ref[...], nw_ref[...]).astype(BF16)
    ones64 = ones_ref[...]
    for kind, w_ref, o_ref in zip(kinds, w_refs, o_refs):
        y = jnp.dot(xn, w_ref[...], preferred_element_type=F32)
        if kind == "sig":
            y = _sigmoid(y)
        elif kind == "q":
            y = _head_rms(y, ones64, hw_ref[0:1, :], NSA_HEAD_DIM) * (NSA_HEAD_DIM ** -0.5 * LOG2E)
        elif kind in ("ks", "kw"):
            row = 1 if kind == "ks" else 2
            half = y.shape[1] // 2
            k = _head_rms(y[:, :half], ones64, hw_ref[row:row + 1, :half], NSA_HEAD_DIM)
            y = jnp.concatenate([k, y[:, half:]], axis=1)
        o_ref[...] = y


def _proj(x, norm_w, ones64, head_w, weights, kinds, tm):
    n, d = x.shape
    assert n % tm == 0
    const = lambda i: (0, 0)
    in_specs = [pl.BlockSpec((tm, d), lambda i: (i, 0)),
                pl.BlockSpec((1, d), const),
                pl.BlockSpec(ones64.shape, const),
                pl.BlockSpec(head_w.shape, const)]
    in_specs += [pl.BlockSpec(w.shape, const) for w in weights]
    out_specs = [pl.BlockSpec((tm, w.shape[1]), lambda i: (i, 0)) for w in weights]
    out_shape = [jax.ShapeDtypeStruct((n, w.shape[1]), F32) for w in weights]
    return pl.pallas_call(
        functools.partial(_proj_kernel, tuple(kinds)),
        grid=(n // tm,), in_specs=in_specs, out_specs=out_specs, out_shape=out_shape,
        compiler_params=_cparams(("parallel",)), name="in_proj",
    )(x, norm_w, ones64, head_w, *weights)


def _softplus(x):
    return jnp.maximum(x, 0.0) + jnp.log1p(jnp.exp(-jnp.abs(x)))


def _ssd_kernel(xbc_ref, z_ref, dt_ref, sg_ref, cw_ref, cb_ref, hp_ref, dsk_ref, nw_ref, e_ref, tri_ref,
                wout_ref, o_ref, ht_ref, ht_s, ext_s):
    c = pl.program_id(1)
    q = SSD_CHUNK
    d_inner = z_ref.shape[1]
    gw = d_inner // N_SSM_GROUPS
    hpg = gw // SSM_HEAD_DIM
    tail = SUBLANES

    @pl.when(c == 0)
    def _():
        ht_s[...] = jnp.zeros_like(ht_s)
        ext_s[0:tail, :] = jnp.zeros((tail, ext_s.shape[1]), F32)

    xb = xbc_ref[...]
    ext_s[tail:tail + q, :] = xb
    u = cb_ref[...] + cw_ref[D_CONV - 1:D_CONV, :] * xb
    for k in range(1, D_CONV):
        u = u + cw_ref[D_CONV - 1 - k:D_CONV - k, :] * ext_s[tail - k:tail - k + q, :]
    ext_s[0:tail, :] = xb[q - tail:q, :]
    u = u * _sigmoid(u)
    xs = u[:, :d_inner]
    bm = u[:, d_inner:d_inner + N_SSM_GROUPS * D_STATE]
    cm = u[:, d_inner + N_SSM_GROUPS * D_STATE:]

    dtv = _softplus(dt_ref[...] + hp_ref[0:1, :])
    a = dtv * (-jnp.exp(hp_ref[1:2, :]))
    acum = _dot3_l(tri_ref[...], a)
    acum_t = acum.T
    eacum = jnp.exp(acum)
    w_end = jnp.exp(acum[q - 1:q, :] - acum) * dtv
    e = e_ref[...]
    dt_x = _dot2(dtv, e)
    we_x = _dot2(w_end, e)
    ea_x = _dot2(eacum, e)
    xdt = (xs * dt_x).astype(BF16)
    xdtw = (xs * we_x).astype(BF16)
    ii = lax.broadcasted_iota(jnp.int32, (q, q), 0)
    jj = lax.broadcasted_iota(jnp.int32, (q, q), 1)
    causal = jj <= ii

    groups = range(N_SSM_GROUPS)
    gsl = [slice(g * gw, (g + 1) * gw) for g in groups]
    cgs = [cm[:, g * D_STATE:(g + 1) * D_STATE].astype(BF16) for g in groups]
    bgs = [bm[:, g * D_STATE:(g + 1) * D_STATE] for g in groups]
    cbms = [_bdot_nt(cgs[g], bgs[g]) for g in groups]
    hgs = [ht_s[:, gsl[g]] for g in groups]
    y_inter = [jnp.dot(cgs[g], hgs[g].astype(BF16), preferred_element_type=F32) for g in groups]
    sts = [jnp.dot(bgs[g].T.astype(BF16), xdtw[:, gsl[g]], preferred_element_type=F32) for g in groups]
    ys = []
    for g in groups:
        yh = []
        for r in range(hpg):
            h = g * hpg + r
            seg = acum[:, h:h + 1] - acum_t[h:h + 1, :]
            dec = jnp.exp(jnp.where(causal, seg, NEG))
            m = (cbms[g] * dec).astype(BF16)
            yh.append(jnp.dot(m, xdt[:, h * SSM_HEAD_DIM:(h + 1) * SSM_HEAD_DIM], preferred_element_type=F32))
        ys.append(y_inter[g] * ea_x[:, gsl[g]] + jnp.concatenate(yh, axis=1))
        ht_s[:, gsl[g]] = hgs[g] * ea_x[q - 1:q, gsl[g]] + sts[g]
    y = jnp.concatenate(ys, axis=1) + dsk_ref[...] * xs
    zz = z_ref[...]
    y = y * (zz * _sigmoid(zz))
    normed = []
    for g in groups:
        yg = y[:, gsl[g]]
        normed.append(yg * lax.rsqrt(jnp.mean(yg * yg, axis=-1, keepdims=True) + RMS_EPS))
    y = jnp.concatenate(normed, axis=1) * nw_ref[...]
    o_ref[...] = sg_ref[...] * _bdot(y, wout_ref[...])

    @pl.when(c == pl.num_programs(1) - 1)
    def _():
        ht_ref[0] = ht_s[...]


def _ssd_prompt(xbc, z, dt, sg, conv_w, conv_b, headp, dsk, nw, expand, tri, wout, batch):
    n, conv_dim = xbc.shape
    d_inner = z.shape[1]
    d_model = wout.shape[1]
    q = SSD_CHUNK
    nc = n // batch // q
    const = lambda b, c: (0, 0)
    rows = lambda b, c: (b * nc + c, 0)
    in_specs = [pl.BlockSpec((q, conv_dim), rows), pl.BlockSpec((q, d_inner), rows),
                pl.BlockSpec((q, LANES), rows), pl.BlockSpec((q, d_model), rows),
                pl.BlockSpec(conv_w.shape, const), pl.BlockSpec(conv_b.shape, const),
                pl.BlockSpec(headp.shape, const), pl.BlockSpec(dsk.shape, const),
                pl.BlockSpec(nw.shape, const), pl.BlockSpec(expand.shape, const),
                pl.BlockSpec(tri.shape, const), pl.BlockSpec(wout.shape, const)]
    out_specs = [pl.BlockSpec((q, d_model), rows),
                 pl.BlockSpec((1, D_STATE, d_inner), lambda b, c: (b, 0, 0))]
    out_shape = [jax.ShapeDtypeStruct((n, d_model), F32),
                 jax.ShapeDtypeStruct((batch, D_STATE, d_inner), F32)]
    return pl.pallas_call(
        _ssd_kernel, grid=(batch, nc), in_specs=in_specs, out_specs=out_specs, out_shape=out_shape,
        scratch_shapes=[pltpu.VMEM((D_STATE, d_inner), F32), pltpu.VMEM((SUBLANES + q, conv_dim), F32)],
        compiler_params=_cparams(("parallel", "arbitrary")), name="ssd_prompt",
    )(xbc, z, dt, sg, conv_w, conv_b, headp, dsk, nw, expand, tri, wout)


def _cmp_build_kernel(kvc_ref, wk_ref, wv_ref, pek_ref, pev_ref, ones_ref, nw_ref, kc_ref, vc_ref):
    ns = kc_ref.shape[0]
    kw = kc_ref.shape[1]
    lo_k = hi_k = lo_v = hi_v = None
    add = lambda acc, v: v if acc is None else acc + v
    nch = 2 * kw // LANES
    for i in range(CMP_STRIDE):
        x = jnp.concatenate([kvc_ref[pl.ds(nch * i + c, ns, stride=nch * CMP_STRIDE), :] for c in range(nch)], axis=1)
        xk, xv = x[:, :kw], x[:, kw:]
        j = CMP_STRIDE + i
        lo_k = add(lo_k, _bdot(xk + pek_ref[i:i + 1, :], wk_ref[i]))
        hi_k = add(hi_k, _bdot(xk + pek_ref[j:j + 1, :], wk_ref[j]))
        lo_v = add(lo_v, _bdot(xv + pev_ref[i:i + 1, :], wv_ref[i]))
        hi_v = add(hi_v, _bdot(xv + pev_ref[j:j + 1, :], wv_ref[j]))
    kc = lo_k + pltpu.roll(hi_k, ns - 1, axis=0)
    vc_ref[...] = lo_v + pltpu.roll(hi_v, ns - 1, axis=0)
    kc_ref[...] = _head_rms(kc, ones_ref[...], nw_ref[...], NSA_HEAD_DIM)


def _cmp_build(kvc, wk, wv, pek, pev, ones64, nw, batch):
    n, w = kvc.shape
    s = n // batch
    ns = s // CMP_STRIDE
    kw = w // 2
    c2 = lambda b: (0, 0)
    c3 = lambda b: (0, 0, 0)
    nch = w // LANES
    kvc = kvc.reshape(n * nch, LANES)
    return pl.pallas_call(
        _cmp_build_kernel, grid=(batch,),
        in_specs=[pl.BlockSpec((s * nch, LANES), lambda b: (b, 0)), pl.BlockSpec(wk.shape, c3),
                  pl.BlockSpec(wv.shape, c3),
                  pl.BlockSpec(pek.shape, c2), pl.BlockSpec(pev.shape, c2), pl.BlockSpec(ones64.shape, c2),
                  pl.BlockSpec(nw.shape, c2)],
        out_specs=[pl.BlockSpec((ns, kw), lambda b: (b, 0)), pl.BlockSpec((ns, kw), lambda b: (b, 0))],
        out_shape=[jax.ShapeDtypeStruct((batch * ns, kw), F32)] * 2,
        compiler_params=_cparams(("parallel",)), name="cmp_build",
    )(kvc, wk, wv, pek, pev, ones64, nw)


def _rank_select(v, n_keep):
    n = v.shape[0]
    rows = SUBLANES
    sub = lax.broadcasted_iota(jnp.int32, (rows, v.shape[1]), 0)
    chunks = [v[c * rows:(c + 1) * rows, :] for c in range(n // rows)]
    cnts = [jnp.zeros(ch.shape, F32) for ch in chunks]
    for i in range(n):
        ri = v[i:i + 1, :]
        for c, ch in enumerate(chunks):
            if c * rows > i:
                beats = jnp.where(ri >= ch, 1.0, 0.0)
            elif (c + 1) * rows - 1 <= i:
                beats = jnp.where(ri > ch, 1.0, 0.0)
            else:
                beats = jnp.where(sub > i - c * rows, jnp.where(ri >= ch, 1.0, 0.0), jnp.where(ri > ch, 1.0, 0.0))
            cnts[c] = cnts[c] + beats
    return jnp.where(jnp.concatenate(cnts, axis=0) < n_keep, 1.0, 0.0)


def _cmp_attn_kernel(q_ref, brg_ref, kc_ref, vc_ref, tb_ref, gexp_ref, ocmp_ref, sel_ref):
    qi = pl.program_id(1)
    t0 = qi * QT
    ns = kc_ref.shape[0]
    nsel = sel_ref.shape[2]
    nb = tb_ref.shape[2]
    band = 2 * CMP_STRIDE
    tt = t0 + lax.broadcasted_iota(jnp.int32, (QT, ns), 0)
    nn = lax.broadcasted_iota(jnp.int32, (QT, ns), 1)
    mask = (CMP_STRIDE * nn + (CMP_BLOCK - 1)) <= tt
    first = (QT // CMP_STRIDE) * qi - band // 2
    cc = lax.broadcasted_iota(jnp.int32, (nb, ns), 0)
    n2 = lax.broadcasted_iota(jnp.int32, (nb, ns), 1)
    shift = jnp.where(((cc < band) & (n2 == first + cc)) | ((cc == band) & (n2 < first)), 1.0, 0.0).astype(BF16)
    jb = lax.broadcasted_iota(jnp.int32, (nsel, ns), 0) * SEL_BLOCK
    cs = lax.broadcasted_iota(jnp.int32, (nsel, ns), 1) * CMP_STRIDE
    cover_t = jnp.where((cs < jb + SEL_BLOCK) & (cs + CMP_BLOCK > jb), 1.0, 0.0).astype(BF16)
    jidx = lax.broadcasted_iota(jnp.int32, (nsel, QT), 0)
    tq = t0 + lax.broadcasted_iota(jnp.int32, (nsel, QT), 1)
    valid = jidx * SEL_BLOCK <= tq
    cur = tq // SEL_BLOCK
    forced = valid & ((jidx == 0) | (jidx == cur) | (jidx == cur - 1))

    q = q_ref[...].astype(BF16)
    head_cols = lambda a, g: a[:, g * NSA_HEAD_DIM:(g + 1) * NSA_HEAD_DIM]
    kgs = [head_cols(kc_ref, g).astype(BF16) for g in range(N_KV_HEADS)]
    vgs = [head_cols(vc_ref, g).astype(BF16) for g in range(N_KV_HEADS)]
    raw = [_bdot_nt(head_cols(q, h), kgs[h // Q_PER_KV]) + _dot3(tb_ref[h], shift) for h in range(N_Q_HEADS)]
    ps = []
    for s in raw:
        s = jnp.where(mask, s, NEG)
        m = jnp.max(s, axis=-1, keepdims=True)
        p = jnp.where(mask, jnp.exp2(s - m), 0.0)
        l = jnp.sum(p, axis=-1, keepdims=True)
        ps.append(p * jnp.where(l > 0.0, 1.0 / l, 0.0))
    outs = [_bdot(ps[h], vgs[h // Q_PER_KV]) for h in range(N_Q_HEADS)]
    imps = []
    for g in range(N_KV_HEADS):
        psum = functools.reduce(lambda a, b: a + b, ps[g * Q_PER_KV:(g + 1) * Q_PER_KV])
        imps.append(_dot3_nt_l(cover_t, psum))
    for g in range(N_KV_HEADS):
        v = jnp.where(forced, FORCE_SCORE, jnp.where(valid, imps[g], NEG))
        sel_ref[0, g] = _rank_select(v, N_SEL_BLOCKS)
    ocmp_ref[...] = jnp.concatenate(outs, axis=1) * _dot3(brg_ref[...], gexp_ref[0])


def _cmp_attn(q, brg, kc, vc, tb, gexp, batch):
    n, qw = q.shape
    s = n // batch
    nq = s // QT
    ns, kw = kc.shape[0] // batch, kc.shape[1]
    nsel = s // SEL_BLOCK
    rows = lambda b, i: (b * nq + i, 0)
    per_b = lambda b, i: (b, 0)
    return pl.pallas_call(
        _cmp_attn_kernel, grid=(batch, nq),
        in_specs=[pl.BlockSpec((QT, qw), rows), pl.BlockSpec((QT, LANES), rows),
                  pl.BlockSpec((ns, kw), per_b), pl.BlockSpec((ns, kw), per_b),
                  pl.BlockSpec(tb.shape, lambda b, i: (0, 0, 0)),
                  pl.BlockSpec((1,) + gexp.shape[1:], lambda b, i: (0, 0, 0))],
        out_specs=[pl.BlockSpec((QT, qw), rows), pl.BlockSpec((1, N_KV_HEADS, nsel, QT), lambda b, i: (b, 0, 0, i))],
        out_shape=[jax.ShapeDtypeStruct((n, qw), F32), jax.ShapeDtypeStruct((batch, N_KV_HEADS, nsel, s), F32)],
        compiler_params=_cparams(("parallel", "parallel")), name="cmp_attn",
    )(q, brg, kc, vc, tb, gexp)


def _attn_round(states, raw, keeps, biases, vts, shifts):
    stats = []
    for st, s, keep, bias, shift in zip(states, raw, keeps, biases, shifts):
        s = _masked_scores(s, keep, bias)
        tile_max = jnp.max(s, axis=0, keepdims=True)
        if shift is not None:
            tile_max = tile_max + shift
        m_new = tile_max if st is None else jnp.maximum(st[0], tile_max)
        alpha = None if st is None else jnp.exp2(st[0] - m_new)
        p = jnp.exp2(s - (m_new if shift is None else m_new - shift)).astype(BF16)
        stats.append((m_new, alpha, p))
    out = []
    for st, (m_new, alpha, p), vt in zip(states, stats, vts):
        pv = jnp.dot(vt, p, preferred_element_type=F32)
        out.append((m_new, pv if st is None else alpha * st[1] + pv))
    return out


def _masked_scores(s, keep, bias=None):
    parts = []
    for r in range(s.shape[1] // QT):
        v = s[:, r * QT:(r + 1) * QT]
        parts.append(jnp.where(keep, v if bias is None else v + bias[r], NEG))
    return jnp.concatenate(parts, axis=1)


def _nsa_main_kernel(c31_ref, q_ref, brg_ref, sel_ref, ocmp_ref, ks_ref, vs_ref, kw_ref, vw_ref, b0_ref, b1_ref,
                     mssm_ref, sgn_ref, x_ref, wn_ref, wo_ref, h_ref):
    qi = pl.program_id(1)
    nwt = WINDOW // QT
    blk_per_tile = QT // SEL_BLOCK
    jk = lax.broadcasted_iota(jnp.int32, (QT, QT), 0)
    iq = lax.broadcasted_iota(jnp.int32, (QT, QT), 1)
    causal_t = jk <= iq
    q_t = q_ref[...].T.astype(BF16)
    gates_t = brg_ref[...].T

    def sel_keep(g, kt, tiles=1):
        rows = [jnp.broadcast_to(sel_ref[0, g, pl.ds(blk_per_tile * kt + b, 1), :], (SEL_BLOCK, QT))
                for b in range(blk_per_tile * tiles)]
        return jnp.concatenate(rows, axis=0) > 0.5

    def key_tile(ref, g, kt):
        return ref[0, g, pl.ds(pl.multiple_of(kt * QT, QT), QT), :]

    groups = range(N_KV_HEADS)
    heads = [[g * Q_PER_KV + r for r in range(Q_PER_KV)] for g in groups]
    qg = [jnp.concatenate([q_t[h * NSA_HEAD_DIM:(h + 1) * NSA_HEAD_DIM, :] for h in heads[g]], axis=1) for g in groups]
    far = [jnp.concatenate([jnp.full((1, QT), c31_ref[h], F32) for h in heads[g]], axis=1) for g in groups]
    near0 = [[b0_ref[h] for h in heads[g]] for g in groups]
    near1 = [[b1_ref[h] for h in heads[g]] for g in groups]
    scores = lambda ref, g, kt: jnp.dot(key_tile(ref, g, kt), qg[g], preferred_element_type=F32)
    prev = jnp.maximum(qi - 1, 0)
    has_prev = jnp.broadcast_to(qi >= 1, (QT, QT))

    none4 = [None] * N_KV_HEADS
    st = _attn_round(
        none4 + none4,
        [scores(ks_ref, g, qi) for g in groups] + [scores(kw_ref, g, qi) for g in groups],
        [causal_t & sel_keep(g, qi) for g in groups] + [causal_t] * N_KV_HEADS, near0 + near0,
        [vs_ref[0, g, qi] for g in groups] + [vw_ref[0, g, qi] for g in groups], none4 + none4)
    st = _attn_round(
        st, [scores(ks_ref, g, prev) for g in groups] + [scores(kw_ref, g, prev) for g in groups],
        [sel_keep(g, prev) & has_prev for g in groups] + [has_prev] * N_KV_HEADS, near1 + near1,
        [vs_ref[0, g, prev] for g in groups] + [vw_ref[0, g, prev] for g in groups], none4 + none4)
    sel, win = st[:N_KV_HEADS], st[N_KV_HEADS:]
    backs = list(range(nwt, 1, -1))
    kts = [jnp.maximum(qi - back, 0) for back in backs]
    keep_w = jnp.concatenate(
        [jnp.where((qi >= back) & ((jk > iq) if back == nwt else True), 1.0, 0.0) * jnp.ones((QT, QT), F32)
         for back in backs], axis=0) > 0.5
    win = _attn_round(
        win, [jnp.dot(jnp.concatenate([key_tile(kw_ref, g, kt) for kt in kts], axis=0), qg[g],
                      preferred_element_type=F32) for g in groups],
        [keep_w] * N_KV_HEADS, none4,
        [jnp.concatenate([vw_ref[0, g, kt] for kt in kts], axis=1) for g in groups], far)

    def sel_body(kt, state):
        return tuple(_attn_round(list(state), [scores(ks_ref, g, kt) for g in groups],
                                 [sel_keep(g, kt) for g in groups], none4, [vs_ref[0, g, kt] for g in groups], far))

    def sel_body_wide(wide, first):
        def body(j, state):
            kt = first + wide * j
            raw = [jnp.dot(ks_ref[0, g, pl.ds(pl.multiple_of(kt * QT, QT), wide * QT), :], qg[g],
                           preferred_element_type=F32) for g in groups]
            keeps = [sel_keep(g, kt, wide) for g in groups]
            vts = [jnp.concatenate([vs_ref[0, g, kt + i] for i in range(wide)], axis=1) for g in groups]
            return tuple(_attn_round(list(state), raw, keeps, none4, vts, far))
        return body

    n_far = jnp.maximum(qi - 1, 0)
    wide, half_wide = FAR_TILES_PER_TRIP, FAR_TILES_PER_TRIP // 2
    done = wide * (n_far // wide)
    sel = lax.fori_loop(0, n_far // wide, sel_body_wide(wide, 0), tuple(sel))
    more = (n_far - done) // half_wide
    sel = lax.fori_loop(0, more, sel_body_wide(half_wide, done), sel)
    sel = lax.fori_loop(done + half_wide * more, n_far, sel_body, sel)

    o_t = []
    dh = NSA_HEAD_DIM
    for g in groups:
        o_s = sel[g][1][0:dh, :] * (1.0 / sel[g][1][dh:dh + 1, :])
        o_w = win[g][1][0:dh, :] * (1.0 / win[g][1][dh:dh + 1, :])
        for r, h in enumerate(heads[g]):
            sl = slice(r * QT, (r + 1) * QT)
            o_t.append(gates_t[3 * h + 1:3 * h + 2, :] * o_s[:, sl] + gates_t[3 * h + 2:3 * h + 3, :] * o_w[:, sl])

    pairs = [jnp.concatenate(o_t[2 * k:2 * k + 2], axis=0).T for k in range(len(o_t) // 2)]
    y_nsa = jnp.concatenate(pairs, axis=1) + ocmp_ref[...]
    u = mssm_ref[...] + sgn_ref[...] * _bdot(y_nsa, wn_ref[...])
    h_ref[...] = x_ref[...] + _bdot(u, wo_ref[...])


def _nsa_main(c31, q, brg, sel, ocmp, ks, vs_t, kw, vw_t, b0, b1, mssm, sgn, x, wn, wo, batch):
    n, qw = q.shape
    d = x.shape[1]
    s = n // batch
    nq = s // QT
    rows = lambda b, i: (b * nq + i, 0)
    kspec = pl.BlockSpec((1,) + ks.shape[1:], lambda b, i: (b, 0, 0, 0))
    vspec = pl.BlockSpec((1,) + vs_t.shape[1:], lambda b, i: (b, 0, 0, 0, 0))
    c2 = lambda b, i: (0, 0)
    c3 = lambda b, i: (0, 0, 0)
    return pl.pallas_call(
        _nsa_main_kernel, grid=(batch, nq),
        in_specs=[pl.BlockSpec(memory_space=pltpu.SMEM),
                  pl.BlockSpec((QT, qw), rows), pl.BlockSpec((QT, LANES), rows),
                  pl.BlockSpec((1,) + sel.shape[1:3] + (QT,), lambda b, i: (b, 0, 0, i)),
                  pl.BlockSpec((QT, qw), rows), kspec, vspec, kspec, vspec,
                  pl.BlockSpec(b0.shape, c3), pl.BlockSpec(b1.shape, c3),
                  pl.BlockSpec((QT, d), rows), pl.BlockSpec((QT, d), rows), pl.BlockSpec((QT, d), rows),
                  pl.BlockSpec(wn.shape, c2), pl.BlockSpec(wo.shape, c2)],
        out_specs=pl.BlockSpec((QT, d), rows), out_shape=jax.ShapeDtypeStruct((n, d), F32),
        compiler_params=_cparams(("parallel", "parallel")), name="nsa_main",
    )(c31, q, brg, sel, ocmp, ks, vs_t, kw, vw_t, b0, b1, mssm, sgn, x, wn, wo)


META_EXPERT, META_RANK, META_GATE = 0, TOP_K, 2 * TOP_K


def _pack_bf16_pair(x):
    c = x.shape[1] // 2
    hi = pltpu.bitcast(x[:, :c].astype(BF16).astype(F32), jnp.int32)
    lo = pltpu.bitcast(x[:, c:].astype(BF16).astype(F32), jnp.int32)
    return hi | lax.shift_right_logical(lo, 16)


def _unpack_bf16_pair(w):
    hi = pltpu.bitcast(w & jnp.int32(-65536), F32).astype(BF16)
    lo = pltpu.bitcast(lax.shift_left(w, 16), F32).astype(BF16)
    return hi, lo


def _router_kernel(h_ref, nw_ref, wr_ref, br_ref, tri_ref, hn_ref, meta_ref, cnt_ref, base_s, *, n_real):
    i = pl.program_id(0)
    tm = h_ref.shape[0]

    @pl.when(i == 0)
    def _():
        base_s[...] = jnp.zeros_like(base_s)

    hn = _rms_rows(h_ref[...], nw_ref[...])
    hn_ref[...] = _pack_bf16_pair(hn)
    a_hi = hn.astype(BF16)
    a_lo = (hn - a_hi.astype(F32)).astype(BF16)
    w = wr_ref[...]
    w_hi = w.astype(BF16)
    w_lo = (w - w_hi.astype(F32)).astype(BF16)
    d = lambda a, b: jnp.dot(a, b, preferred_element_type=F32)
    v = (d(a_hi, w_hi) + (d(a_hi, w_lo) + d(a_lo, w_hi))) + br_ref[...]
    lane = lax.broadcasted_iota(jnp.int32, v.shape, 1)
    tops, idxs, hots = [], [], []
    for _ in range(TOP_K):
        m = jnp.max(v, axis=-1, keepdims=True)
        idx = jnp.min(jnp.where(v == m, lane, LANES), axis=-1, keepdims=True)
        hot = lane == idx
        tops.append(m)
        idxs.append(idx)
        hots.append(hot)
        v = jnp.where(hot, NEG, v)
    es = [jnp.exp(t - tops[0]) for t in tops]
    inv = 1.0 / functools.reduce(lambda a, b: a + b, es)
    row = i * tm + lax.broadcasted_iota(jnp.int32, v.shape, 0)
    onehot = jnp.zeros(v.shape, F32)
    for hot in hots:
        onehot = onehot + jnp.where(hot & (row < n_real), 1.0, 0.0)
    before = jnp.dot(tri_ref[...], onehot.astype(BF16), preferred_element_type=F32) + base_s[0:1, :]
    meta = jnp.zeros(v.shape, F32)
    for k in range(TOP_K):
        rank = jnp.sum(jnp.where(hots[k], before, 0.0), axis=-1, keepdims=True)
        meta = meta + jnp.where(lane == META_EXPERT + k, idxs[k].astype(F32), 0.0)
        meta = meta + jnp.where(lane == META_RANK + k, rank, 0.0)
        meta = meta + jnp.where(lane == META_GATE + k, es[k] * inv, 0.0)
    meta_ref[...] = meta
    base_s[0:1, :] = base_s[0:1, :] + jnp.sum(onehot, axis=0, keepdims=True)
    cnt_ref[...] = jnp.broadcast_to(base_s[0:1, :], cnt_ref.shape)


def _router(h, nw, wr, br, tm, n_real):
    n, d = h.shape
    c2 = lambda i: (0, 0)
    tri = jnp.asarray(np.tril(np.ones((tm, tm), np.float32), -1), BF16)
    return pl.pallas_call(
        functools.partial(_router_kernel, n_real=n_real), grid=(n // tm,),
        in_specs=[pl.BlockSpec((tm, d), lambda i: (i, 0)), pl.BlockSpec(nw.shape, c2),
                  pl.BlockSpec(wr.shape, c2), pl.BlockSpec(br.shape, c2), pl.BlockSpec(tri.shape, c2)],
        out_specs=[pl.BlockSpec((tm, d // 2), lambda i: (i, 0)), pl.BlockSpec((tm, LANES), lambda i: (i, 0)),
                   pl.BlockSpec((SUBLANES, LANES), c2)],
        out_shape=[jax.ShapeDtypeStruct((n, d // 2), jnp.int32), jax.ShapeDtypeStruct((n, LANES), F32),
                   jax.ShapeDtypeStruct((SUBLANES, LANES), F32)],
        scratch_shapes=[pltpu.VMEM((SUBLANES, LANES), F32)],
        compiler_params=_cparams(("arbitrary",)), name="moe_router",
    )(h, nw, wr, br, tri)


def _moe_group_kernel(sp_ref, x_ref, w1_ref, b1_ref, w2_ref, b2_ref, y_ref, w1_s, w2_s):
    t = pl.program_id(0)
    n_active = sp_ref[pl.num_programs(0)]
    new_expert = (t == 0) | (sp_ref[t] != sp_ref[jnp.maximum(t - 1, 0)])

    @pl.when(new_expert & (t < n_active))
    def _():
        w1_s[...] = w1_ref[0].astype(BF16)
        w2_s[...] = w2_ref[0].astype(BF16)

    @pl.when(t < n_active)
    def _():
        d_ff = w2_s.shape[0]
        x_a, x_b = _unpack_bf16_pair(x_ref[...])
        half = x_a.shape[1]
        y1 = (jnp.dot(x_a, w1_s[:half, :], preferred_element_type=F32)
              + jnp.dot(x_b, w1_s[half:, :], preferred_element_type=F32)) + b1_ref[0]
        gt = jnp.minimum(y1[:, :d_ff], SWIGLU_LIMIT)
        up = jnp.clip(y1[:, d_ff:], -SWIGLU_LIMIT, SWIGLU_LIMIT)
        act = (up + 1.0) * gt * _sigmoid(SWIGLU_ALPHA * gt)
        y_ref[...] = _pack_bf16_pair(_bdot(act, w2_s[...]) + b2_ref[0])

    @pl.when(t >= n_active)
    def _():
        y_ref[...] = jnp.zeros_like(y_ref)


def _moe_group(sp, x_sorted, w1, b1, w2, b2, tm):
    p, dx = x_sorted.shape
    d = w2.shape[2]
    rows = lambda t, sp: (t, 0)
    per_e = lambda t, sp: (sp[t], 0, 0)
    grid_spec = pltpu.PrefetchScalarGridSpec(
        num_scalar_prefetch=1, grid=(p // tm,),
        in_specs=[pl.BlockSpec((tm, dx), rows),
                  pl.BlockSpec((1,) + w1.shape[1:], per_e), pl.BlockSpec((1,) + b1.shape[1:], per_e),
                  pl.BlockSpec((1,) + w2.shape[1:], per_e), pl.BlockSpec((1,) + b2.shape[1:], per_e)],
        out_specs=pl.BlockSpec((tm, d // 2), rows),
        scratch_shapes=[pltpu.VMEM(w1.shape[1:], BF16), pltpu.VMEM(w2.shape[1:], BF16)])
    return pl.pallas_call(
        _moe_group_kernel, grid_spec=grid_spec, out_shape=jax.ShapeDtypeStruct((p, d // 2), jnp.int32),
        compiler_params=_cparams(("arbitrary",)), name="moe_experts",
    )(sp, x_sorted, w1, b1, w2, b2)


def _moe_combine_kernel(h_ref, yk_ref, meta_ref, o_ref):
    d = h_ref.shape[1]
    meta = meta_ref[...]
    kk = lax.broadcasted_iota(jnp.int32, (LANES, LANES), 0)
    acc_a = h_ref[:, :d // 2]
    acc_b = h_ref[:, d // 2:]
    for k in range(TOP_K):
        pick = jnp.where(kk == META_GATE + k, 1.0, 0.0).astype(BF16)
        g = _dot3(meta, pick)
        g = jnp.concatenate([g] * (d // 2 // LANES), axis=1)
        y_a, y_b = _unpack_bf16_pair(yk_ref[k])
        acc_a = acc_a + g * y_a.astype(F32)
        acc_b = acc_b + g * y_b.astype(F32)
    o_ref[:, :d // 2] = acc_a
    o_ref[:, d // 2:] = acc_b


def _moe_combine(h, yk, meta, tm):
    n, d = h.shape
    rows = lambda i: (i, 0)
    return pl.pallas_call(
        _moe_combine_kernel, grid=(n // tm,),
        in_specs=[pl.BlockSpec((tm, d), rows), pl.BlockSpec((TOP_K, tm, d // 2), lambda i: (0, i, 0)),
                  pl.BlockSpec((tm, LANES), rows)],
        out_specs=pl.BlockSpec((tm, d), rows), out_shape=jax.ShapeDtypeStruct((n, d), F32),
        compiler_params=_cparams(("parallel",)), name="moe_combine",
    )(h, yk, meta)


def _mamba_prep_kernel(xbc_ref, sconv_ref, dt_ref, cw_ref, cb_ref, hp_ref, e_ref,
                       conv_ref, xs_ref, bm_ref, ct_ref, dtx_ref, dec_ref):
    cdim = xbc_ref.shape[1]
    d_inner = xs_ref.shape[1]
    xb = xbc_ref[...]
    u = cb_ref[...] + cw_ref[D_CONV - 1:D_CONV, :] * xb
    for k in range(D_CONV - 1):
        u = u + cw_ref[k:k + 1, :] * sconv_ref[:, k * cdim:(k + 1) * cdim]
    conv_ref[:, :(D_CONV - 2) * cdim] = sconv_ref[:, cdim:]
    conv_ref[:, (D_CONV - 2) * cdim:] = xb
    u = u * _sigmoid(u)
    xs = u[:, :d_inner]
    xs_ref[...] = xs
    bm_ref[...] = u[:, d_inner:d_inner + N_SSM_GROUPS * D_STATE]
    cm = u[:, d_inner + N_SSM_GROUPS * D_STATE:]
    for g in range(N_SSM_GROUPS):
        ct_ref[g] = cm[:, g * D_STATE:(g + 1) * D_STATE].T
    dtv = _softplus(dt_ref[...] + hp_ref[0:1, :])
    dec_ref[...] = jnp.exp(dtv * (-jnp.exp(hp_ref[1:2, :])))
    dtx_ref[...] = (xs * _dot3(dtv, e_ref[...])).T


def _mamba_prep(xbc, sconv, dt, conv_w, conv_b, headp, expand):
    n, cdim = xbc.shape
    d_inner = expand.shape[1]
    gn = N_SSM_GROUPS * D_STATE
    out_shape = [jax.ShapeDtypeStruct(sconv.shape, F32), jax.ShapeDtypeStruct((n, d_inner), F32),
                 jax.ShapeDtypeStruct((n, gn), F32), jax.ShapeDtypeStruct((N_SSM_GROUPS, D_STATE, n), F32),
                 jax.ShapeDtypeStruct((d_inner, n), F32), jax.ShapeDtypeStruct((n, LANES), F32)]
    return pl.pallas_call(_mamba_prep_kernel, out_shape=out_shape,
                          compiler_params=pltpu.CompilerParams(vmem_limit_bytes=VMEM_LIMIT), name="mamba_prep",
                          )(xbc, sconv, dt, conv_w, conv_b, headp, expand)


def _bf16x3(a, b):
    a_hi = a.astype(BF16)
    a_lo = (a - a_hi.astype(F32)).astype(BF16)
    b_hi = b.astype(BF16)
    b_lo = (b - b_hi.astype(F32)).astype(BF16)
    d = lambda x, y: jnp.dot(x, y, preferred_element_type=F32)
    return d(a_hi, b_hi) + (d(a_hi, b_lo) + d(a_lo, b_hi))


def _mamba_state_kernel(dec_ref, h0_ref, dtx_ref, bm_ref, ct_ref, hn_ref, yt_ref):
    s = pl.program_id(0)
    n = bm_ref.shape[0]
    d_inner = dtx_ref.shape[0]
    gw = d_inner // N_SSM_GROUPS
    hpg = gw // SSM_HEAD_DIM

    @pl.when(s == 0)
    def _():
        yt_ref[...] = jnp.zeros_like(yt_ref)

    row_is_s = lax.broadcasted_iota(jnp.int32, (n, D_STATE), 0) == s
    col_is_s = lax.broadcasted_iota(jnp.int32, (D_STATE, n), 1) == s
    groups = range(N_SSM_GROUPS)
    sts = []
    for g in groups:
        b_s = jnp.where(row_is_s, bm_ref[:, g * D_STATE:(g + 1) * D_STATE], 0.0)
        sts.append(_bf16x3(dtx_ref[g * gw:(g + 1) * gw, :], b_s))
    hns = []
    for g in groups:
        parts = []
        for r in range(hpg):
            h = g * hpg + r
            rows = slice(h * SSM_HEAD_DIM, (h + 1) * SSM_HEAD_DIM)
            parts.append(h0_ref[0, rows, :] * dec_ref[s, h] + sts[g][r * SSM_HEAD_DIM:(r + 1) * SSM_HEAD_DIM, :])
        hns.append(jnp.concatenate(parts, axis=0))
        hn_ref[0, g * gw:(g + 1) * gw, :] = hns[g]
    for g in groups:
        c_s = jnp.where(col_is_s, ct_ref[g], 0.0)
        yt_ref[g * gw:(g + 1) * gw, :] += _bdot(hns[g], c_s)


def _mamba_state(dec, h0, dtx_t, bm, ct):
    n, rows, ns = h0.shape
    c2 = lambda s: (0, 0)
    return pl.pallas_call(
        _mamba_state_kernel, grid=(n,),
        in_specs=[pl.BlockSpec(memory_space=pltpu.SMEM),
                  pl.BlockSpec((1, rows, ns), lambda s: (s, 0, 0)), pl.BlockSpec(dtx_t.shape, c2),
                  pl.BlockSpec(bm.shape, c2), pl.BlockSpec(ct.shape, lambda s: (0, 0, 0))],
        out_specs=[pl.BlockSpec((1, rows, ns), lambda s: (s, 0, 0)), pl.BlockSpec(dtx_t.shape, c2)],
        out_shape=[jax.ShapeDtypeStruct(h0.shape, F32), jax.ShapeDtypeStruct(dtx_t.shape, F32)],
        compiler_params=_cparams(("arbitrary",)), name="mamba_state",
    )(dec, h0, dtx_t, bm, ct)


def _page_cmp_kernel(pt_ref, cache_ref, new_ref, w_ref, pec_ref, deint_ref, ones_ref, nw_ref, out_ref, buf, rows_s, sem):
    t = pl.program_id(0)
    nsteps = pl.num_programs(0)
    n_pages = pt_ref.shape[1]
    page = buf.shape[4]
    nstr = out_ref.shape[2]
    half = out_ref.shape[3]
    nch = half // LANES
    spp = page // CMP_STRIDE
    past_str = n_pages * spp

    def page_copy(step, slot, p):
        seq, kv = step // 2, step % 2
        return pltpu.make_async_copy(cache_ref.at[pt_ref[seq, p], pl.ds(nch * kv, nch)], buf.at[slot, p], sem.at[slot])

    @pl.when(t == 0)
    def _():
        rows_s[:, past_str:, :] = jnp.zeros((CMP_STRIDE, nstr - past_str, half), F32)
        for p in range(n_pages):
            page_copy(t, 0, p).start()

    slot = t % 2

    @pl.when(t + 1 < nsteps)
    def _():
        for p in range(n_pages):
            page_copy(t + 1, 1 - slot, p).start()

    for p in range(n_pages):
        page_copy(t, slot, p).wait()

    kv = t % 2
    deint = deint_ref[...]

    pages_per_trip = math.gcd(n_pages, 16)

    def to_rows(trip, carry):
        for pp in range(pages_per_trip):
            p = trip * pages_per_trip + pp
            xr = _bdot_nt(deint, buf[slot, p].reshape(nch * LANES, page))
            for i in range(CMP_STRIDE):
                rows_s[i, pl.ds(pl.multiple_of(p * spp, spp), spp), :] = xr[i * spp:(i + 1) * spp, :]
        return carry

    lax.fori_loop(0, n_pages // pages_per_trip, to_rows, 0)
    new = new_ref[0]
    rows_s[0, past_str:past_str + 1, :] = jnp.where(kv == 0, new[:, :half], new[:, half:])

    xcat = jnp.concatenate([rows_s[i].astype(BF16) for i in range(CMP_STRIDE)], axis=1)
    w_lo = w_ref[0, :CMP_STRIDE].reshape(CMP_STRIDE * half, half)
    w_hi = w_ref[0, CMP_STRIDE:].reshape(CMP_STRIDE * half, half)
    lo = jnp.dot(xcat, w_lo, preferred_element_type=F32) + pec_ref[0, 0:1, :]
    hi = jnp.dot(xcat, w_hi, preferred_element_type=F32) + pec_ref[0, 1:2, :]
    tok = lo + pltpu.roll(hi, nstr - 1, axis=0)
    out_ref[0, 0] = jnp.where(kv == 0, _head_rms(tok, ones_ref[...], nw_ref[...], NSA_HEAD_DIM), tok)


def _cmp_const_kernel(pe_ref, w_ref, out_ref):
    for kv in range(2):
        lo = hi = None
        for i in range(CMP_STRIDE):
            j = CMP_STRIDE + i
            a = _dot3(_rows8(pe_ref[kv, i:i + 1, :]), w_ref[kv, i])
            b = _dot3(_rows8(pe_ref[kv, j:j + 1, :]), w_ref[kv, j])
            lo = a if lo is None else lo + a
            hi = b if hi is None else hi + b
        out_ref[kv] = jnp.concatenate([lo[0:1, :], hi[0:1, :], jnp.zeros((SUBLANES - 2, lo.shape[1]), F32)], axis=0)


def _cmp_const(pe_kv, w_kv):
    return pl.pallas_call(_cmp_const_kernel, out_shape=jax.ShapeDtypeStruct((2, SUBLANES, pe_kv.shape[2]), F32),
                          compiler_params=pltpu.CompilerParams(vmem_limit_bytes=VMEM_LIMIT), name="cmp_const",
                          )(pe_kv, w_kv)


def _page_cmp(page_table, cache_t, new_rows, w_kv, pe_const, ones64, nw, nstr):
    nseq, n_pages = page_table.shape
    page = cache_t.shape[3]
    half = w_kv.shape[2]
    nch = half // LANES
    spp = page // CMP_STRIDE
    deint = np.zeros((page, page), np.float32)
    for i in range(CMP_STRIDE):
        for k in range(spp):
            deint[i * spp + k, CMP_STRIDE * k + i] = 1.0
    deint = jnp.asarray(deint, BF16)
    c2 = lambda t, pt: (0, 0)
    grid_spec = pltpu.PrefetchScalarGridSpec(
        num_scalar_prefetch=1, grid=(2 * nseq,),
        in_specs=[pl.BlockSpec(memory_space=pl.ANY),
                  pl.BlockSpec((1, 1, 2 * half), lambda t, pt: (t // 2, 0, 0)),
                  pl.BlockSpec((1,) + w_kv.shape[1:], lambda t, pt: (t % 2, 0, 0, 0)),
                  pl.BlockSpec((1,) + pe_const.shape[1:], lambda t, pt: (t % 2, 0, 0)),
                  pl.BlockSpec(deint.shape, c2), pl.BlockSpec(ones64.shape, c2), pl.BlockSpec(nw.shape, c2)],
        out_specs=pl.BlockSpec((1, 1, nstr, half), lambda t, pt: (t // 2, t % 2, 0, 0)),
        scratch_shapes=[pltpu.VMEM((2, n_pages, nch, LANES, page), F32),
                        pltpu.VMEM((CMP_STRIDE, nstr, half), F32),
                        pltpu.SemaphoreType.DMA((2,))])
    return pl.pallas_call(
        _page_cmp_kernel, grid_spec=grid_spec, out_shape=jax.ShapeDtypeStruct((nseq, 2, nstr, half), F32),
        compiler_params=_cparams(("arbitrary",)), name="page_cmp",
    )(page_table, cache_t, new_rows, w_kv, pe_const, deint, ones64, nw)


def _rows8(x):
    return jnp.broadcast_to(x, (SUBLANES, x.shape[1]))


def _group_q(q_row, g):
    parts = [q_row[:, (g * Q_PER_KV + r) * NSA_HEAD_DIM:(g * Q_PER_KV + r + 1) * NSA_HEAD_DIM]
             for r in range(Q_PER_KV)]
    parts.append(jnp.zeros((SUBLANES - Q_PER_KV, NSA_HEAD_DIM), F32))
    return jnp.concatenate(parts, axis=0)


def _heads_to_row(o):
    return jnp.concatenate([o[r:r + 1, :] for r in range(Q_PER_KV)], axis=1)


def _softmax_rows(s, keep):
    s = jnp.where(keep, s, NEG)
    m = jnp.max(s, axis=-1, keepdims=True)
    p = jnp.where(keep, jnp.exp2(s - m), 0.0)
    l = jnp.sum(p, axis=-1, keepdims=True)
    return p * jnp.where(l > 0.0, 1.0 / l, 0.0)


def _sample_cw_kernel(q_ref, brg_ref, kvc_ref, win_ref, wnew_ref, tbc_ref, tbw_ref, gexp_ref,
                      o_ref, sel_ref, wout_ref, wall_s, *, tq, past_w):
    nstr = kvc_ref.shape[2]
    half = kvc_ref.shape[3]
    wlen = win_ref.shape[1]
    nsel_pad = sel_ref.shape[2]
    wrows = wall_s.shape[0]
    q_row = q_ref[0]
    wall_s[0:wlen, :] = win_ref[0]
    wall_s[wlen:wlen + 1, :] = wnew_ref[0]
    wall_s[wlen + 1:, :] = jnp.zeros((wrows - wlen - 1, wall_s.shape[1]), F32)
    wout_ref[0] = wall_s[1:wlen + 1, :]

    nn = lax.broadcasted_iota(jnp.int32, (SUBLANES, nstr), 1)
    keep_c = (CMP_STRIDE * nn + (CMP_BLOCK - 1)) <= tq
    wi = lax.broadcasted_iota(jnp.int32, (SUBLANES, wrows), 1)
    dw = tq - (past_w + wi)
    keep_w = (dw >= 0) & (dw < WINDOW) & (past_w + wi >= 0) & (wi <= wlen)
    jb = lax.broadcasted_iota(jnp.int32, (nsel_pad, nstr), 0) * SEL_BLOCK
    cs = lax.broadcasted_iota(jnp.int32, (nsel_pad, nstr), 1) * CMP_STRIDE
    cover_t = jnp.where((cs < jb + SEL_BLOCK) & (cs + CMP_BLOCK > jb), 1.0, 0.0).astype(BF16)
    ji = lax.broadcasted_iota(jnp.int32, (nsel_pad, LANES), 0)
    valid = ji * SEL_BLOCK <= tq
    cur = tq // SEL_BLOCK
    forced = valid & ((ji == 0) | (ji == cur) | (ji == cur - 1))
    ii = lax.broadcasted_iota(jnp.int32, (nsel_pad, nsel_pad), 0)
    jj = lax.broadcasted_iota(jnp.int32, (nsel_pad, nsel_pad), 1)

    sel_ref[0] = jnp.zeros(sel_ref.shape[1:], F32)
    groups = range(N_KV_HEADS)
    k_lanes = [slice(g * NSA_HEAD_DIM, (g + 1) * NSA_HEAD_DIM) for g in groups]
    v_lanes = [slice(half + g * NSA_HEAD_DIM, half + (g + 1) * NSA_HEAD_DIM) for g in groups]
    qgs = [_group_q(q_row, g) for g in groups]
    sc = [_bdot_nt(qgs[g], kvc_ref[0, 0, :, k_lanes[g]]) + tbc_ref[g] for g in groups]
    sw = [_bdot_nt(qgs[g], wall_s[:, k_lanes[g]]) + tbw_ref[g] for g in groups]
    pc = [_softmax_rows(s, keep_c) for s in sc]
    pw = [_softmax_rows(s, keep_w) for s in sw]
    oc = [_heads_to_row(_bdot(pc[g], kvc_ref[0, 1, :, k_lanes[g]])) for g in groups]
    ow = [_heads_to_row(_bdot(pw[g], wall_s[:, v_lanes[g]])) for g in groups]
    imps = [_dot3_nt_l(cover_t, jnp.broadcast_to(jnp.sum(pc[g][0:Q_PER_KV, :], axis=0, keepdims=True), (LANES, nstr)))
            for g in groups]
    for g in groups:
        v_col = jnp.where(forced, FORCE_SCORE, jnp.where(valid, imps[g], NEG))
        v_row = jnp.concatenate([v_col[k * LANES:(k + 1) * LANES, :].T for k in range(nsel_pad // LANES)], axis=1)
        a = jnp.broadcast_to(v_row[0:1, :], (nsel_pad, nsel_pad))
        b = jnp.concatenate([v_col] * (nsel_pad // LANES), axis=1)
        beats = jnp.where(ii < jj, jnp.where(b >= a, 1.0, 0.0), jnp.where(b > a, 1.0, 0.0))
        cnt = jnp.sum(beats, axis=0, keepdims=True)
        sel_ref[0, g:g + 1, :] = jnp.where(cnt < N_SEL_BLOCKS, 1.0, 0.0)
    gates = _rows8(brg_ref[0])
    o = (jnp.concatenate(oc, axis=1) * _dot3(gates, gexp_ref[0])[0:1, :]
         + jnp.concatenate(ow, axis=1) * _dot3(gates, gexp_ref[2])[0:1, :])
    o_ref[0] = o


def _sample_cw(q, brg, kvc, win, wnew, tbc, tbw, gexp, tq, past_w, nsel_pad):
    nseq = q.shape[0]
    qw = q.shape[2]
    wlen, ww = win.shape[1], win.shape[2]
    wrows = -(-(wlen + 1) // SUBLANES) * SUBLANES
    per3 = lambda s: (s, 0, 0)
    c3 = lambda s: (0, 0, 0)
    return pl.pallas_call(
        functools.partial(_sample_cw_kernel, tq=tq, past_w=past_w), grid=(nseq,),
        in_specs=[pl.BlockSpec((1, 1, qw), per3), pl.BlockSpec((1, 1, LANES), per3),
                  pl.BlockSpec((1,) + kvc.shape[1:], lambda s: (s, 0, 0, 0)),
                  pl.BlockSpec((1, wlen, ww), per3), pl.BlockSpec((1, 1, ww), per3),
                  pl.BlockSpec(tbc.shape, c3), pl.BlockSpec(tbw.shape, c3), pl.BlockSpec(gexp.shape, c3)],
        out_specs=[pl.BlockSpec((1, 1, qw), per3), pl.BlockSpec((1, SUBLANES, nsel_pad), per3),
                   pl.BlockSpec((1, wlen, ww), per3)],
        out_shape=[jax.ShapeDtypeStruct((nseq, 1, qw), F32), jax.ShapeDtypeStruct((nseq, SUBLANES, nsel_pad), F32),
                   jax.ShapeDtypeStruct((nseq, wlen, ww), F32)],
        scratch_shapes=[pltpu.VMEM((wrows, ww), F32)],
        compiler_params=_cparams(("parallel",)), name="sample_cmp_win",
    )(q, brg, kvc, win, wnew, tbc, tbw, gexp)


def _sample_sel_kernel(pg_ref, c31_ref, f0_ref, cache_ref, q_ref, brg_ref, snew_ref, code_ref, ocw_ref, tbl_ref, gexp_ref,
                       o_ref, buf, sem):
    s = pl.program_id(0)
    nseq = pl.num_programs(0)
    nblk = pg_ref.shape[1] // N_KV_HEADS
    page = cache_ref.shape[4]
    half = snew_ref.shape[2] // 2

    def copies(seq, slot):
        out = []
        for g in range(N_KV_HEADS):
            for k in range(nblk):
                out.append(pltpu.make_async_copy(cache_ref.at[pg_ref[seq, g * nblk + k], :, g],
                                                 buf.at[slot, :, g, :, pl.ds(k * page, page)], sem.at[slot]))
        return out

    @pl.when(s == 0)
    def _():
        for cp in copies(s, 0):
            cp.start()

    slot = s % 2

    @pl.when(s + 1 < nseq)
    def _():
        for cp in copies(s + 1, 1 - slot):
            cp.start()

    for cp in copies(s, slot):
        cp.wait()

    q_row = q_ref[0]
    new = snew_ref[0]
    groups = range(N_KV_HEADS)
    qgs = [_group_q(q_row, g) for g in groups]
    raw = [jnp.dot(qgs[g].astype(BF16), buf[slot, 0, g].astype(BF16), preferred_element_type=F32) for g in groups]
    stats = []
    for g in groups:
        heads = [g * Q_PER_KV + r for r in range(Q_PER_KV)]
        per_head = lambda ref: jnp.concatenate([jnp.full((1, 1), ref[h], F32) for h in heads]
                                               + [jnp.zeros((SUBLANES - Q_PER_KV, 1), F32)], axis=0)
        code = _rows8(code_ref[0, g:g + 1, :])
        near = jnp.concatenate([tbl_ref[g]] * nblk, axis=1)
        sc = jnp.where(code > 0.5, raw[g] + jnp.where(code > 1.5, near, per_head(c31_ref)), NEG)
        k_new = new[:, g * NSA_HEAD_DIM:(g + 1) * NSA_HEAD_DIM]
        s_new = jnp.sum(qgs[g] * _rows8(k_new), axis=-1, keepdims=True) + per_head(f0_ref)
        new_on = _rows8(code_ref[0, N_KV_HEADS + g:N_KV_HEADS + g + 1, 0:1]) > 0.5
        s_new = jnp.where(new_on, s_new, NEG)
        m = jnp.maximum(jnp.max(sc, axis=-1, keepdims=True), s_new)
        p = jnp.where(code > 0.5, jnp.exp2(sc - m), 0.0)
        p_new = jnp.where(new_on, jnp.exp2(s_new - m), 0.0)
        l = jnp.sum(p, axis=-1, keepdims=True) + p_new
        stats.append((p, p_new, jnp.where(l > 0.0, 1.0 / l, 0.0)))
    outs = []
    for g in groups:
        p, p_new, inv = stats[g]
        v_new = new[:, half + g * NSA_HEAD_DIM:half + (g + 1) * NSA_HEAD_DIM]
        o = _bdot_nt(p, buf[slot, 1, g]) + p_new * _rows8(v_new)
        outs.append(_heads_to_row(o * inv))
    gates = _dot3(_rows8(brg_ref[0]), gexp_ref[1])[0:1, :]
    o_ref[0] = ocw_ref[0] + jnp.concatenate(outs, axis=1) * gates


def _sample_sel(pages, c31, f0, cache_t, q, brg, snew, code, ocw, tbl, gexp):
    nseq, qw = q.shape[0], q.shape[2]
    nblk = pages.shape[1] // N_KV_HEADS
    page = cache_t.shape[4]
    per3 = lambda s, *_: (s, 0, 0)
    c3 = lambda s, *_: (0, 0, 0)
    grid_spec = pltpu.PrefetchScalarGridSpec(
        num_scalar_prefetch=1, grid=(nseq,),
        in_specs=[pl.BlockSpec(memory_space=pltpu.SMEM), pl.BlockSpec(memory_space=pltpu.SMEM),
                  pl.BlockSpec(memory_space=pl.ANY),
                  pl.BlockSpec((1, 1, qw), per3), pl.BlockSpec((1, 1, LANES), per3),
                  pl.BlockSpec((1, 1, snew.shape[2]), per3), pl.BlockSpec((1,) + code.shape[1:], per3),
                  pl.BlockSpec((1, 1, qw), per3), pl.BlockSpec(tbl.shape, c3), pl.BlockSpec(gexp.shape, c3)],
        out_specs=pl.BlockSpec((1, 1, qw), per3),
        scratch_shapes=[pltpu.VMEM((2, 2, N_KV_HEADS, NSA_HEAD_DIM, nblk * page), F32),
                        pltpu.SemaphoreType.DMA((2,))])
    return pl.pallas_call(
        _sample_sel_kernel, grid_spec=grid_spec, out_shape=jax.ShapeDtypeStruct((nseq, 1, qw), F32),
        compiler_params=_cparams(("arbitrary",)), name="sample_sel",
    )(pages, c31, f0, cache_t, q, brg, snew, code, ocw, tbl, gexp)


def _sample_merge_kernel(yt_ref, xs_ref, z_ref, sgs_ref, dsk_ref, nw_ref, onesg_ref, ws_ref, ynsa_ref, sgn_ref, x_ref,
                         wn_ref, wo_ref, h_ref):
    gw = onesg_ref.shape[0]
    y = yt_ref[...].T + dsk_ref[...] * xs_ref[...]
    zz = z_ref[...]
    y = y * (zz * _sigmoid(zz))
    ms = _seg_sum(y * y, onesg_ref[...]) * (1.0 / gw)
    y = (y * lax.rsqrt(ms + RMS_EPS)) * nw_ref[...]
    u = sgs_ref[...] * _bdot(y, ws_ref[...]) + sgn_ref[...] * _bdot(ynsa_ref[...], wn_ref[...])
    h_ref[...] = x_ref[...] + _bdot(u, wo_ref[...])


def _sample_merge(y_t, xs, z, sgs, dsk, nw, onesg, ws, ynsa, sgn, x, wn, wo):
    return pl.pallas_call(_sample_merge_kernel, out_shape=jax.ShapeDtypeStruct(x.shape, F32),
                          compiler_params=pltpu.CompilerParams(vmem_limit_bytes=VMEM_LIMIT), name="sample_merge",
                          )(y_t, xs, z, sgs, dsk, nw, onesg, ws, ynsa, sgn, x, wn, wo)


def _bucket_lut():
    n = np.arange(MAX_DISTANCE + 1)
    max_exact = N_BUCKETS // 2
    nf = np.maximum(n, 1).astype(np.float32)
    large = max_exact + (np.log(nf / max_exact) / math.log(MAX_DISTANCE / max_exact)
                         * (N_BUCKETS - max_exact)).astype(np.int32)
    return np.where(n < max_exact, n, np.minimum(large, N_BUCKETS - 1))


def _bias_of_dist(rel_bias, dist):
    lut = _bucket_lut()
    idx = lut[np.clip(dist, 0, MAX_DISTANCE)]
    onehot = np.eye(N_BUCKETS, dtype=np.float32)[idx] * (np.asarray(dist) >= 0)[..., None]
    b = jnp.einsum("...b,bh->h...", jnp.asarray(onehot), rel_bias.astype(F32), precision=lax.Precision.HIGHEST)
    return b * LOG2E


def _block_diag(w, reps):
    n, d, e = w.shape
    eye = jnp.eye(reps, dtype=w.dtype)
    return jnp.einsum("ab,nde->nadbe", eye, w).reshape(n, reps * d, reps * e)


def _ones_blocks(size, seg):
    return jnp.asarray(np.kron(np.eye(size // seg), np.ones((seg, seg))), BF16)


def _pad_cols(w, width):
    return jnp.pad(w, ((0, 0), (0, width - w.shape[1])))


def _prep(p):
    d_model = p["w_in"].shape[1]
    d_inner = p["w_ssm_out"].shape[1]
    n_heads = d_inner // SSM_HEAD_DIM
    conv_dim = p["conv_w"].shape[2]
    qw = N_Q_HEADS * NSA_HEAD_DIM
    kvw = 2 * N_KV_HEADS * NSA_HEAD_DIM
    splits = (d_inner, conv_dim, n_heads, qw, kvw, kvw, kvw, 3 * N_Q_HEADS, d_model, d_model)
    offs = np.concatenate([[0], np.cumsum(splits)])
    w_in = p["w_in"][0]
    seg = lambda k: w_in[:, offs[k]:offs[k + 1]]
    bf = lambda a: a.astype(BF16)
    o = {}
    o["w_ssm_in"] = [bf(seg(0)), bf(seg(1)), bf(_pad_cols(seg(2), LANES))]
    o["w_nsa_in"] = [bf(seg(3)), bf(seg(4)), bf(seg(5)), bf(seg(6)), bf(_pad_cols(seg(7), LANES)), bf(seg(8)),
                     bf(seg(9))]
    o["norm_mix"] = p["norm_mix_w"][0][None, :]
    kv_half = kvw // 2
    head_w = jnp.zeros((SUBLANES, qw), F32)
    head_w = head_w.at[0].set(jnp.tile(p["q_norm_w"][0], N_Q_HEADS))
    head_w = head_w.at[1, :kv_half].set(jnp.tile(p["k_sel_norm_w"][0], N_KV_HEADS))
    head_w = head_w.at[2, :kv_half].set(jnp.tile(p["k_win_norm_w"][0], N_KV_HEADS))
    o["head_w"] = head_w
    o["ones64"] = _ones_blocks(kv_half, NSA_HEAD_DIM)
    o["conv_w"] = p["conv_w"][0]
    o["conv_b"] = p["conv_b"][0][None, :]
    headp = jnp.zeros((SUBLANES, LANES), F32)
    headp = headp.at[0, :n_heads].set(p["dt_bias"][0]).at[1, :n_heads].set(p["a_log"][0])
    o["headp"] = headp
    o["dsk"] = jnp.repeat(p["d_skip"][0], SSM_HEAD_DIM)[None, :]
    o["ssm_nw"] = p["ssm_norm_w"][0][None, :]
    expand = np.zeros((LANES, d_inner), np.float32)
    for h in range(n_heads):
        expand[h, h * SSM_HEAD_DIM:(h + 1) * SSM_HEAD_DIM] = 1.0
    o["expand"] = jnp.asarray(expand, BF16)
    o["tri"] = jnp.asarray(np.tril(np.ones((SSD_CHUNK, SSD_CHUNK), np.float32)), BF16)
    o["onesg"] = jnp.ones((d_inner // N_SSM_GROUPS,) * 2, BF16)
    o["w_ssm_out"] = bf(p["w_ssm_out"][0])
    o["cmp_wk"] = bf(_block_diag(p["cmp_w_k"][0], N_KV_HEADS))
    o["cmp_wv"] = bf(_block_diag(p["cmp_w_v"][0], N_KV_HEADS))
    o["cmp_pek"] = jnp.tile(p["cmp_pe_k"][0], (1, N_KV_HEADS))
    o["cmp_pev"] = jnp.tile(p["cmp_pe_v"][0], (1, N_KV_HEADS))
    o["kc_nw"] = jnp.tile(p["k_cmp_norm_w"][0], N_KV_HEADS)[None, :]
    rel = p["rel_bias"]
    band = 2 * CMP_STRIDE
    i = np.arange(QT)[:, None]
    c = np.arange(band)[None, :]
    d_band = i + CMP_STRIDE * (band // 2) - CMP_STRIDE * c - (CMP_BLOCK - 1)
    d_band = np.concatenate([d_band, np.full((QT, 1), MAX_DISTANCE)], axis=1)
    tb = _bias_of_dist(rel, d_band)
    o["cmp_tb"] = jnp.pad(tb, ((0, 0), (0, 0), (0, 2 * band - tb.shape[2])))
    jk = np.arange(QT)[:, None]
    iq = np.arange(QT)[None, :]
    o["b0"] = _bias_of_dist(rel, iq - jk)
    o["b1"] = _bias_of_dist(rel, QT + iq - jk)
    o["c31"] = rel[N_BUCKETS - 1].astype(F32) * LOG2E
    gexp = np.zeros((3, LANES, qw), np.float32)
    for h in range(N_Q_HEADS):
        for k in range(3):
            gexp[k, 3 * h + k, h * NSA_HEAD_DIM:(h + 1) * NSA_HEAD_DIM] = 1.0
    o["gexp"] = jnp.asarray(gexp, BF16)
    o["w_nsa_out"] = bf(p["w_nsa_out"][0])
    o["w_out"] = bf(p["w_out"][0])
    o["norm_ffn"] = p["norm_ffn_w"][0][None, :]
    ne = p["w_router"].shape[2]
    o["w_router"] = _pad_cols(p["w_router"][0], LANES)
    o["b_router"] = jnp.full((1, LANES), NEG, F32).at[0, :ne].set(p["b_router"][0])
    o["w1"] = p["w_gate_up"][0]
    o["b1e"] = p["b_gate_up"][0][:, None, :]
    o["w2"] = p["w_down"][0]
    o["b2e"] = p["b_down"][0][:, None, :]
    return o


def _kv_layouts(kv, batch):
    n = kv.shape[0]
    s = n // batch
    half = kv.shape[1] // 2
    k = kv[:, :half].astype(BF16).reshape(batch, s, N_KV_HEADS, NSA_HEAD_DIM).transpose(0, 2, 1, 3)
    v = kv[:, half:].astype(BF16).reshape(batch, s // QT, QT, N_KV_HEADS, NSA_HEAD_DIM).transpose(0, 3, 1, 4, 2)
    extra = jnp.zeros(v.shape[:3] + (SUBLANES, QT), BF16).at[:, :, :, 0, :].set(1.0)
    return k, jnp.concatenate([v, extra], axis=3)


def _prompt_mixer(x, o, batch):
    z, xbc, dt = _proj(x, o["norm_mix"], o["ones64"], o["head_w"], o["w_ssm_in"], ["raw", "raw", "raw"], 256)
    q, kvc, kvs, kvw, brg, sg_ssm, sg_nsa = _proj(
        x, o["norm_mix"], o["ones64"], o["head_w"], o["w_nsa_in"], ["q", "raw", "ks", "kw", "sig", "sig", "sig"], 256)
    m_ssm, h_t = _ssd_prompt(xbc, z, dt, sg_ssm, o["conv_w"], o["conv_b"], o["headp"], o["dsk"], o["ssm_nw"],
                             o["expand"], o["tri"], o["w_ssm_out"], batch)
    kc, vc = _cmp_build(kvc, o["cmp_wk"], o["cmp_wv"], o["cmp_pek"], o["cmp_pev"], o["ones64"], o["kc_nw"], batch)
    ocmp, sel = _cmp_attn(q, brg, kc, vc, o["cmp_tb"], o["gexp"], batch)
    ks, vs_t = _kv_layouts(kvs, batch)
    kw, vw_t = _kv_layouts(kvw, batch)
    h = _nsa_main(o["c31"], q, brg, sel, ocmp, ks, vs_t, kw, vw_t, o["b0"], o["b1"], m_ssm, sg_nsa, x,
                  o["w_nsa_out"], o["w_out"], batch)
    return h, (kvc, kvs, kvw, h_t, xbc)


MOE_TOKEN_TILE = 384
MOE_GROUP_TILE = 512


def _moe(h_all, o):
    n, d = h_all.shape
    tm, tg = MOE_TOKEN_TILE, MOE_GROUP_TILE
    ne = o["w1"].shape[0]
    n_pad = -(-n // tm) * tm
    hp = jnp.pad(h_all, ((0, n_pad - n), (0, 0))) if n_pad != n else h_all
    hn, meta, cnt = _router(hp, o["norm_ffn"], o["w_router"], o["b_router"], tm, n)
    eid = meta[:n, META_EXPERT:META_EXPERT + TOP_K].astype(jnp.int32)
    rank = meta[:n, META_RANK:META_RANK + TOP_K].astype(jnp.int32)
    count = cnt[0, :ne].astype(jnp.int32)
    tiles = (count + tg - 1) // tg
    tile_end = jnp.cumsum(tiles)
    first_row = (tile_end - tiles) * tg
    pos = jnp.sum(jnp.where(eid[..., None] == jnp.arange(ne), first_row, 0), axis=-1) + rank
    n_tiles = -(-(n * TOP_K) // tg) + ne
    tile_expert = jnp.minimum(jnp.sum(tile_end[None, :] <= jnp.arange(n_tiles)[:, None], axis=1), ne - 1)
    sp = jnp.concatenate([tile_expert, tile_end[-1:]]).astype(jnp.int32)
    src = jnp.zeros((n_tiles * tg,), jnp.int32).at[pos.reshape(-1)].set(
        jnp.arange(n * TOP_K, dtype=jnp.int32) // TOP_K, unique_indices=True, mode="promise_in_bounds")
    y_sorted = _moe_group(sp, hn[src], o["w1"], o["b1e"], o["w2"], o["b2e"], tg)
    yk = y_sorted[pos.T.reshape(-1)].reshape(TOP_K, n, d // 2)
    if n_pad != n:
        yk = jnp.pad(yk, ((0, 0), (0, n_pad - n), (0, 0)))
    return _moe_combine(hp, yk, meta, tm)[:n]


def kernel(x_prompt, x_sample, cache_cmp, cache_sel, cache_win, state_ssm, state_conv, page_table, norm_mix_w, w_in,
           conv_w, conv_b, dt_bias, a_log, d_skip, ssm_norm_w, w_ssm_out, q_norm_w, k_cmp_norm_w, k_sel_norm_w,
           k_win_norm_w, cmp_pe_k, cmp_w_k, cmp_pe_v, cmp_w_v, rel_bias, w_nsa_out, w_out, norm_ffn_w, w_router,
           b_router, w_gate_up, b_gate_up, w_down, b_down):
    params = dict(norm_mix_w=norm_mix_w, w_in=w_in, conv_w=conv_w, conv_b=conv_b, dt_bias=dt_bias, a_log=a_log,
                  d_skip=d_skip, ssm_norm_w=ssm_norm_w, w_ssm_out=w_ssm_out, q_norm_w=q_norm_w,
                  k_cmp_norm_w=k_cmp_norm_w, k_sel_norm_w=k_sel_norm_w, k_win_norm_w=k_win_norm_w,
                  cmp_pe_k=cmp_pe_k, cmp_w_k=cmp_w_k, cmp_pe_v=cmp_pe_v, cmp_w_v=cmp_w_v, rel_bias=rel_bias,
                  w_nsa_out=w_nsa_out, w_out=w_out, norm_ffn_w=norm_ffn_w, w_router=w_router, b_router=b_router,
                  w_gate_up=w_gate_up, b_gate_up=b_gate_up, w_down=w_down, b_down=b_down)
    o = _prep(params)
    bsz, s, d = x_prompt.shape
    db, t, _ = x_sample.shape
    kvshape = (2, N_KV_HEADS, NSA_HEAD_DIM)
    hp, (kvc, kvs, kvw, h_t, xbc) = _prompt_mixer(x_prompt.reshape(bsz * s, d), o, bsz)
    assert t == 1, "the sample group decodes one token per sequence"
    hs, (kvc_s, kvs_s, win_s, h_s, conv_s) = _sample_mixer(
        x_sample.reshape(db, d), o, cache_cmp[0], cache_sel[0], cache_win[0], state_ssm[0], state_conv[0],
        page_table, rel_bias)
    y_all = _moe(jnp.concatenate([hp, hs], axis=0), o)
    wlen = min(WINDOW, s)
    n_heads = h_t.shape[2] // SSM_HEAD_DIM
    outs_p = (kvc.reshape((1, bsz, s) + kvshape), kvs.reshape((1, bsz, s) + kvshape),
              kvw.reshape((bsz, s) + kvshape)[None, :, s - wlen:],
              h_t.reshape(bsz, D_STATE, n_heads, SSM_HEAD_DIM).transpose(0, 2, 3, 1)[None],
              xbc.reshape(bsz, s, -1)[None, :, s - (D_CONV - 1):])
    outs_s = (kvc_s.reshape((1, db, t) + kvshape), kvs_s.reshape((1, db, t) + kvshape),
              win_s.reshape((1, db, win_s.shape[1]) + kvshape), h_s.reshape((1, db, n_heads, SSM_HEAD_DIM, D_STATE)),
              conv_s.reshape(1, db, D_CONV - 1, -1))
    return (y_all[:bsz * s].reshape(bsz, s, d), y_all[bsz * s:].reshape(db, t, d)) + outs_p + outs_s


def _sample_mixer(x, o, cache_cmp, cache_sel, cache_win, state_ssm, state_conv, page_table, rel):
    n = x.shape[0]
    z, xbc, dt = _proj(x, o["norm_mix"], o["ones64"], o["head_w"], o["w_ssm_in"], ["raw", "raw", "raw"], n)
    q, kvc, kvs, kvw, brg, sg_ssm, sg_nsa = _proj(
        x, o["norm_mix"], o["ones64"], o["head_w"], o["w_nsa_in"], ["q", "raw", "ks", "kw", "sig", "sig", "sig"], n)
    d_inner = z.shape[1]
    n_heads = d_inner // SSM_HEAD_DIM
    conv_new, xs, bm, ct, dtx_t, dec = _mamba_prep(xbc, state_conv.reshape(n, -1), dt, o["conv_w"], o["conv_b"],
                                                   o["headp"], o["expand"])
    h_new, y_t = _mamba_state(dec[:, :n_heads], state_ssm.reshape(n, d_inner, D_STATE), dtx_t, bm, ct)
    pool, page = cache_cmp.shape[0], cache_cmp.shape[1]
    n_pages = page_table.shape[1]
    past = n_pages * page
    tq = past
    assert page >= MAX_DISTANCE and page % SEL_BLOCK == 0 and past % CMP_STRIDE == 0
    half = kvc.shape[1] // 2
    nch = half // LANES
    cmp_t = jnp.transpose(cache_cmp, (0, 2, 3, 4, 1)).reshape(pool, 2 * nch, LANES, page)
    sel_t = jnp.transpose(cache_sel, (0, 2, 3, 4, 1))
    nstr = -(-(past // CMP_STRIDE + 1) // SUBLANES) * SUBLANES
    w_kv = jnp.stack([o["cmp_wk"], o["cmp_wv"]])
    pe_kv = jnp.stack([o["cmp_pek"], o["cmp_pev"]])
    tok = _page_cmp(page_table, cmp_t, kvc[:, None, :], w_kv, _cmp_const(pe_kv, w_kv), o["ones64"], o["kc_nw"], nstr)
    pad_heads = lambda b: jnp.pad(b.reshape(N_KV_HEADS, Q_PER_KV, -1), ((0, 0), (0, SUBLANES - Q_PER_KV), (0, 0)))
    tbc = pad_heads(_bias_of_dist(rel, tq - (CMP_STRIDE * np.arange(nstr) + CMP_BLOCK - 1)))
    wlen = cache_win.shape[1]
    wrows = -(-(wlen + 1) // SUBLANES) * SUBLANES
    past_w = past - wlen
    tbw = pad_heads(_bias_of_dist(rel, tq - (past_w + np.arange(wrows))))
    n_past_sel = past // SEL_BLOCK
    nsel = n_past_sel + 1
    nsel_pad = -(-nsel // LANES) * LANES
    o_cw, selmask, win_new = _sample_cw(q[:, None, :], brg[:, None, :], tok, cache_win.reshape(n, wlen, -1),
                                        kvw[:, None, :], tbc, tbw, o["gexp"], tq, past_w, nsel_pad)
    nblk = min(N_SEL_BLOCKS, nsel)
    picked = selmask[:, :N_KV_HEADS, :nsel] > 0.5
    order = jnp.cumsum(picked, axis=-1) - 1
    hit = picked[..., None] & (order[..., None] == jnp.arange(nblk))
    idx = jnp.sum(jnp.where(hit, jnp.arange(nsel)[:, None], 0), axis=2)
    is_past = idx < n_past_sel
    jp = jnp.minimum(idx, n_past_sel - 1)
    per_page = page // SEL_BLOCK
    pg = jp // per_page
    phys = jnp.take_along_axis(page_table, pg.reshape(n, -1), axis=1).astype(jnp.int32)
    lane_blk = (np.arange(page) // SEL_BLOCK)[None, None, None, :]
    attended = is_past[..., None] & (lane_blk == (jp % per_page)[..., None])
    code = jnp.where(attended, jnp.where((pg == n_pages - 1)[..., None], 2.0, 1.0), 0.0).reshape(n, N_KV_HEADS, -1)
    new_on = jnp.any(idx >= n_past_sel, axis=-1).astype(F32)
    code = jnp.concatenate([code, jnp.broadcast_to(new_on[..., None], code.shape)], axis=1)
    tbl = pad_heads(_bias_of_dist(rel, tq - ((n_pages - 1) * page + np.arange(page))))
    f0 = rel[_bucket_lut()[0]].astype(F32) * LOG2E
    y_nsa = _sample_sel(phys, o["c31"], f0, sel_t, q[:, None, :], brg[:, None, :], kvs[:, None, :], code, o_cw, tbl,
                        o["gexp"])
    h = _sample_merge(y_t, xs, z, sg_ssm, o["dsk"], o["ssm_nw"], o["onesg"], o["w_ssm_out"], y_nsa[:, 0], sg_nsa, x,
                      o["w_nsa_out"], o["w_out"])
    return h, (kvc, kvs, win_new, h_new, conv_new)
```
